```python
import jax, jax.numpy as jnp
from jax import lax
import numpy as np

D_MODEL = 2048
BATCH = 4
SEQ = 2048
DEPTH = 2
DEC_BATCH = 128
DEC_SEQ = 4
PAST_LEN = 16384
PAGE_SIZE = 128

MIX_WIDTH = D_MODEL
GLA_WIDTH = MIX_WIDTH // 2
HGRN_WIDTH = MIX_WIDTH - GLA_WIDTH
GLA_HEADS = 4
GLA_DK = GLA_WIDTH // 2 // GLA_HEADS
GLA_DV = GLA_WIDTH // GLA_HEADS
GLA_RANK = 16
GLA_TAU = 16.0
HGRN_EXPAND = 128
HGRN_HEADS = HGRN_WIDTH // HGRN_EXPAND
HGRN_DI = HGRN_WIDTH // HGRN_HEADS
D_FF = ((8 * D_MODEL // 3 + 255) // 256) * 256
CHUNK = 64
EPS = 1e-6

IN_SIZES = (GLA_HEADS * GLA_DK,
            GLA_HEADS * GLA_DK,
            GLA_WIDTH,
            GLA_RANK,
            GLA_WIDTH,
            HGRN_HEADS * HGRN_EXPAND,
            HGRN_HEADS * HGRN_EXPAND,
            HGRN_WIDTH,
            HGRN_WIDTH)
IN_WIDTH = sum(IN_SIZES)

kernel_name = "hymba_gla_hgrn2_macaron_step"


def _split_points():
    pts, acc = [], 0
    for s in IN_SIZES[:-1]:
        acc += s
        pts.append(acc)
    return pts


def rmsnorm(x, w):
    xf = x.astype(jnp.float32)
    y = xf * lax.rsqrt(jnp.mean(xf * xf, axis=-1, keepdims=True) + EPS)
    return (y * w.astype(jnp.float32)).astype(x.dtype)


def swiglu(x, w_in, w_out):
    gate, up = jnp.split(x @ w_in, 2, axis=-1)
    return (jax.nn.silu(gate) * up) @ w_out


def gated_linear_scan(q, k, v, log_a, s0):
    f32 = jnp.float32
    q, k, v, log_a, s0 = (z.astype(f32) for z in (q, k, v, log_a, s0))
    b, t, h, dk = q.shape
    dv = v.shape[-1]
    c = CHUNK if t % CHUNK == 0 else t
    n = t // c

    def to_chunks(z):
        return z.reshape(b, n, c, h, z.shape[-1]).transpose(1, 0, 3, 2, 4)

    qc, kc, vc, gc = to_chunks(q), to_chunks(k), to_chunks(v), to_chunks(log_a)
    causal = jnp.tril(jnp.ones((c, c), dtype=bool))[:, :, None]

    def step(s, inp):
        qi, ki, vi, gi = inp
        cum = jnp.cumsum(gi, axis=2)
        diff = cum[:, :, :, None, :] - cum[:, :, None, :, :]
        decay = jnp.where(causal, jnp.exp(jnp.where(causal, diff, 0.0)), 0.0)
        scores = jnp.sum(qi[:, :, :, None, :] * ki[:, :, None, :, :] * decay, axis=-1)
        o = (jnp.einsum('bhts,bhsv->bhtv', scores, vi)
             + jnp.einsum('bhtd,bhdv->bhtv', qi * jnp.exp(cum), s))
        last = cum[:, :, -1:, :]
        s_new = (jnp.exp(last[:, :, 0, :])[..., None] * s
                 + jnp.einsum('bhsd,bhsv->bhdv', ki * jnp.exp(last - cum), vi))
        return s_new, o

    s_fin, oc = lax.scan(step, s0, (qc, kc, vc, gc))
    o = oc.transpose(1, 0, 3, 2, 4).reshape(b, t, h, dv)
    return o, s_fin


def hybrid_mixer(h, w_in, gla_w_up, gla_b, gla_norm, lb, hgrn_norm, w_out, s_gla, s_hgrn):
    b, t, _ = h.shape
    proj = h @ w_in
    q, k, v, a_lr, r, hq, hf, hi, hg = jnp.split(proj, _split_points(), axis=-1)

    log_alpha = jax.nn.log_sigmoid((a_lr @ gla_w_up + gla_b).astype(jnp.float32)) / GLA_TAU
    o_gla, s_gla_new = gated_linear_scan(
        q.reshape(b, t, GLA_HEADS, GLA_DK) * (GLA_DK ** -0.5),
        k.reshape(b, t, GLA_HEADS, GLA_DK),
        v.reshape(b, t, GLA_HEADS, GLA_DV),
        log_alpha.reshape(b, t, GLA_HEADS, GLA_DK),
        s_gla)
    o_gla = rmsnorm(o_gla, gla_norm) * jax.nn.silu(r.reshape(b, t, GLA_HEADS, GLA_DV).astype(jnp.float32))
    o_gla = o_gla.reshape(b, t, GLA_WIDTH)

    lbf = lb.astype(jnp.float32)
    zf = hf.astype(jnp.float32)
    f = lbf + (1.0 - lbf) * jax.nn.sigmoid(zf)
    log_f = jnp.log(f)
    k_h = (1.0 - lbf) * jax.nn.sigmoid(-zf)
    o_h, s_hgrn_new = gated_linear_scan(
        jax.nn.silu(hq.reshape(b, t, HGRN_HEADS, HGRN_EXPAND)) * (HGRN_EXPAND ** -0.5),
        k_h.reshape(b, t, HGRN_HEADS, HGRN_EXPAND),
        hi.reshape(b, t, HGRN_HEADS, HGRN_DI),
        log_f.reshape(b, t, HGRN_HEADS, HGRN_EXPAND),
        s_hgrn)
    o_h = rmsnorm(o_h, hgrn_norm) * jax.nn.silu(hg.reshape(b, t, HGRN_HEADS, HGRN_DI).astype(jnp.float32))
    o_h = o_h.reshape(b, t, HGRN_WIDTH)

    merged = jnp.concatenate([o_gla, o_h], axis=-1).astype(h.dtype)
    return merged @ w_out, s_gla_new, s_hgrn_new


def trunk(x, s_gla, s_hgrn, norm_gains, ffn1_w_in, ffn1_w_out, ffn2_w_in, ffn2_w_out,
          mix_w_in, gla_w_gate_up, gla_b_gate, gla_norm, hgrn_gamma, hgrn_norm, mix_w_out):
    probs = jax.nn.softmax(hgrn_gamma.astype(jnp.float32), axis=0)
    lbs = jnp.cumsum(probs, axis=0) - probs[0:1]
    new_gla, new_hgrn = [], []
    for l in range(DEPTH):
        g = norm_gains[l]
        x = x + 0.5 * rmsnorm(swiglu(rmsnorm(x, g[0]), ffn1_w_in[l], ffn1_w_out[l]), g[1])
        m, sg, sh = hybrid_mixer(rmsnorm(x, g[2]), mix_w_in[l], gla_w_gate_up[l], gla_b_gate[l],
                                 gla_norm[l], lbs[l], hgrn_norm[l], mix_w_out[l], s_gla[l], s_hgrn[l])
        x = x + rmsnorm(m, g[3])
        x = x + 0.5 * rmsnorm(swiglu(rmsnorm(x, g[4]), ffn2_w_in[l], ffn2_w_out[l]), g[5])
        new_gla.append(sg)
        new_hgrn.append(sh)
    return x, jnp.stack(new_gla), jnp.stack(new_hgrn)


def setup_inputs(seed: int = 0) -> dict:
    key = jax.random.key(seed)
    ks = jax.random.split(key, 16)
    f32 = jnp.float32

    def nrm(k, shape, scale):
        return jax.random.normal(k, shape, f32) * scale

    return {
        "x_prompt": nrm(ks[0], (BATCH, SEQ, D_MODEL), 1.0),
        "x_sample": nrm(ks[1], (DEC_BATCH, DEC_SEQ, D_MODEL), 1.0),
        "state_gla": nrm(ks[2], (DEPTH, DEC_BATCH, GLA_HEADS, GLA_DK, GLA_DV), 1.0),
        "state_hgrn": nrm(ks[3], (DEPTH, DEC_BATCH, HGRN_HEADS, HGRN_EXPAND, HGRN_DI), 1.0),
        "norm_gains": 1.0 + nrm(ks[4], (DEPTH, 6, D_MODEL), 0.02),
        "ffn1_w_in": nrm(ks[5], (DEPTH, D_MODEL, 2 * D_FF), D_MODEL ** -0.5),
        "ffn1_w_out": nrm(ks[6], (DEPTH, D_FF, D_MODEL), D_FF ** -0.5),
        "ffn2_w_in": nrm(ks[7], (DEPTH, D_MODEL, 2 * D_FF), D_MODEL ** -0.5),
        "ffn2_w_out": nrm(ks[8], (DEPTH, D_FF, D_MODEL), D_FF ** -0.5),
        "mix_w_in": nrm(ks[9], (DEPTH, D_MODEL, IN_WIDTH), D_MODEL ** -0.5),
        "gla_w_gate_up": nrm(ks[10], (DEPTH, GLA_RANK, GLA_HEADS * GLA_DK), GLA_RANK ** -0.5),
        "gla_b_gate": nrm(ks[11], (DEPTH, GLA_HEADS * GLA_DK), 0.01),
        "gla_norm": 1.0 + nrm(ks[12], (DEPTH, GLA_DV), 0.02),
        "hgrn_gamma": nrm(ks[13], (DEPTH, HGRN_HEADS * HGRN_EXPAND), 0.1),
        "hgrn_norm": 1.0 + nrm(ks[14], (DEPTH, HGRN_DI), 0.02),
        "mix_w_out": nrm(ks[15], (DEPTH, MIX_WIDTH, D_MODEL), MIX_WIDTH ** -0.5),
    }


def reference(x_prompt, x_sample, state_gla, state_hgrn, norm_gains, ffn1_w_in, ffn1_w_out,
              ffn2_w_in, ffn2_w_out, mix_w_in, gla_w_gate_up, gla_b_gate, gla_norm,
              hgrn_gamma, hgrn_norm, mix_w_out):
    zero_gla = jnp.zeros((DEPTH, x_prompt.shape[0], GLA_HEADS, GLA_DK, GLA_DV), jnp.float32)
    zero_hgrn = jnp.zeros((DEPTH, x_prompt.shape[0], HGRN_HEADS, HGRN_EXPAND, HGRN_DI), jnp.float32)
    y_prompt, state_gla_prompt, state_hgrn_prompt = trunk(
        x_prompt, zero_gla, zero_hgrn, norm_gains, ffn1_w_in, ffn1_w_out, ffn2_w_in, ffn2_w_out,
        mix_w_in, gla_w_gate_up, gla_b_gate, gla_norm, hgrn_gamma, hgrn_norm, mix_w_out)
    y_sample, state_gla_sample, state_hgrn_sample = trunk(
        x_sample, state_gla, state_hgrn, norm_gains, ffn1_w_in, ffn1_w_out, ffn2_w_in, ffn2_w_out,
        mix_w_in, gla_w_gate_up, gla_b_gate, gla_norm, hgrn_gamma, hgrn_norm, mix_w_out)
    return (y_prompt, y_sample, state_gla_prompt, state_hgrn_prompt, state_gla_sample, state_hgrn_sample)
```

```python
import functools

import jax
import jax.numpy as jnp
from jax import lax
from jax.experimental import pallas as pl
from jax.experimental.pallas import tpu as pltpu

F32 = jnp.float32
BF16 = jnp.bfloat16

D_MODEL = 2048
BATCH = 4
SEQ = 2048
DEPTH = 2
DEC_BATCH = 128
DEC_SEQ = 4
GLA_HEADS = 4
GLA_DK = 128
GLA_DV = 256
GLA_WIDTH = GLA_HEADS * GLA_DV
GLA_RANK = 16
GLA_TAU = 16.0
HGRN_HEADS = 8
HGRN_DK = 128
HGRN_DV = 128
HGRN_WIDTH = HGRN_HEADS * HGRN_DV
D_FF = 5632
EPS = 1e-6

N_PROMPT = BATCH * SEQ
N_SAMPLE = DEC_BATCH * DEC_SEQ
N_ROWS = N_PROMPT + N_SAMPLE

LANES = 128
SUBLANES = 8
VMEM_LIMIT = 56 * 1024 * 1024

COL_Q = 0
COL_K = 512
COL_V = 1024
COL_R = 2048
COL_HQ = 3072
COL_HF = 4096
COL_HI = 5120
COL_HG = 6144
COL_A = 7168
PROJ_W = 7680

TM_IN = 1088
TN = 512
TM_OUT = 544
TK = 512
CHUNK = 128
SAMPLE_NB = 16


def _params(sem):
    return pltpu.CompilerParams(dimension_semantics=sem, vmem_limit_bytes=VMEM_LIMIT)


def _sigmoid(x):
    return 1.0 / (1.0 + jnp.exp(-x))


def _rms(y, g):
    return y * lax.rsqrt(jnp.mean(y * y, axis=-1, keepdims=True) + EPS) * g


def _rms_cast_kernel(x_ref, g_ref, h_ref):
    h_ref[...] = _rms(x_ref[...], g_ref[...]).astype(BF16)


def _rms_cast(x, gains, gi):
    return pl.pallas_call(
        _rms_cast_kernel,
        grid=(N_ROWS // TM_OUT,),
        in_specs=[pl.BlockSpec((TM_OUT, D_MODEL), lambda m: (m, 0)),
                  pl.BlockSpec((None, 1, D_MODEL), lambda m: (gi, 0, 0))],
        out_specs=pl.BlockSpec((TM_OUT, D_MODEL), lambda m: (m, 0)),
        out_shape=jax.ShapeDtypeStruct((N_ROWS, D_MODEL), BF16),
        compiler_params=_params(("arbitrary",)),
        name="rms_cast",
    )(x, gains)


def _ffn_in_kernel(h_ref, wg_ref, wu_ref, o_ref, wg_s, wu_s):
    @pl.when(pl.program_id(1) == 0)
    def _():
        wg_s[...] = wg_ref[...].astype(BF16)
        wu_s[...] = wu_ref[...].astype(BF16)

    h = h_ref[...]
    g = jnp.dot(h, wg_s[...], preferred_element_type=F32)
    u = jnp.dot(h, wu_s[...], preferred_element_type=F32)
    o_ref[...] = (g * _sigmoid(g) * u).astype(BF16)


def _ffn_in(h, w_in, layer):
    nt = D_FF // TN
    return pl.pallas_call(
        _ffn_in_kernel,
        grid=(nt, N_ROWS // TM_IN),
        in_specs=[pl.BlockSpec((TM_IN, D_MODEL), lambda n, m: (m, 0)),
                  pl.BlockSpec((None, D_MODEL, TN), lambda n, m: (layer, 0, n)),
                  pl.BlockSpec((None, D_MODEL, TN), lambda n, m: (layer, 0, n + nt))],
        out_specs=pl.BlockSpec((TM_IN, TN), lambda n, m: (m, n)),
        out_shape=jax.ShapeDtypeStruct((N_ROWS, D_FF), BF16),
        scratch_shapes=[pltpu.VMEM((D_MODEL, TN), BF16), pltpu.VMEM((D_MODEL, TN), BF16)],
        compiler_params=_params(("arbitrary", "arbitrary")),
        name="ffn_in",
    )(h, w_in, w_in)


def _mix_in_kernel(h_ref, w_ref, o_ref, w_s):
    @pl.when(pl.program_id(1) == 0)
    def _():
        w_s[...] = w_ref[...].astype(BF16)

    o_ref[...] = jnp.dot(h_ref[...], w_s[...], preferred_element_type=F32)


def _mix_in(h, w_perm, layer):
    return pl.pallas_call(
        _mix_in_kernel,
        grid=(PROJ_W // TN, N_ROWS // TM_IN),
        in_specs=[pl.BlockSpec((TM_IN, D_MODEL), lambda n, m: (m, 0)),
                  pl.BlockSpec((None, D_MODEL, TN), lambda n, m: (layer, 0, n))],
        out_specs=pl.BlockSpec((TM_IN, TN), lambda n, m: (m, n)),
        out_shape=jax.ShapeDtypeStruct((N_ROWS, PROJ_W), F32),
        scratch_shapes=[pltpu.VMEM((D_MODEL, TN), BF16)],
        compiler_params=_params(("arbitrary", "arbitrary")),
        name="mix_in",
    )(h, w_perm)


def _proj_out_kernel(a_ref, w_ref, x_ref, gp_ref, gn_ref, xo_ref, ho_ref, *, alpha, nk):
    k = pl.program_id(1)
    p = jnp.dot(a_ref[...], w_ref[...].astype(BF16), preferred_element_type=F32)

    @pl.when(k == 0)
    def _():
        xo_ref[...] = p

    @pl.when(jnp.logical_and(k > 0, k < nk - 1))
    def _():
        xo_ref[...] += p

    @pl.when(k == nk - 1)
    def _():
        y = xo_ref[...] + p
        xn = x_ref[...] + alpha * _rms(y, gp_ref[...])
        xo_ref[...] = xn
        ho_ref[...] = _rms(xn, gn_ref[...]).astype(BF16)


def _proj_out(a, w, layer, x, gains, gi_post, gi_next, alpha):
    kdim = a.shape[1]
    nk = kdim // TK
    kern = functools.partial(_proj_out_kernel, alpha=alpha, nk=nk)
    return pl.pallas_call(
        kern,
        grid=(N_ROWS // TM_OUT, nk),
        in_specs=[pl.BlockSpec((TM_OUT, TK), lambda m, k: (m, k)),
                  pl.BlockSpec((None, TK, D_MODEL), lambda m, k: (layer, k, 0)),
                  pl.BlockSpec((TM_OUT, D_MODEL), lambda m, k: (m, 0)),
                  pl.BlockSpec((None, 1, D_MODEL), lambda m, k: (gi_post, 0, 0)),
                  pl.BlockSpec((None, 1, D_MODEL), lambda m, k: (gi_next, 0, 0))],
        out_specs=[pl.BlockSpec((TM_OUT, D_MODEL), lambda m, k: (m, 0)),
                   pl.BlockSpec((TM_OUT, D_MODEL), lambda m, k: (m, 0))],
        out_shape=[jax.ShapeDtypeStruct((N_ROWS, D_MODEL), F32),
                   jax.ShapeDtypeStruct((N_ROWS, D_MODEL), BF16)],
        compiler_params=_params(("arbitrary", "arbitrary")),
        name="proj_out",
    )(a, w, x, gains, gains)


def _log_sigmoid(x):
    return jnp.minimum(x, 0.0) - jnp.log(1.0 + jnp.exp(-jnp.abs(x)))


def _gla_gate(a, wup_ref, bg_ref):
    pre = jnp.dot(a.astype(BF16), wup_ref[...].astype(BF16), preferred_element_type=F32)
    return _log_sigmoid(pre + bg_ref[...]) * (1.0 / GLA_TAU)


def _hgrn_lower_bound(gam_ref, layer):
    gam = gam_ref[...]
    mx = jnp.max(gam, axis=0, keepdims=True)
    e = jnp.exp(gam - mx)
    probs = e / jnp.sum(e, axis=0, keepdims=True)
    acc = probs[0:1, :]
    for j in range(1, layer + 1):
        acc = acc + probs[j:j + 1, :]
    return acc - probs[0:1, :]


def _hgrn_gate(z, lb):
    e = jnp.exp(-jnp.abs(z))
    r = 1.0 / (1.0 + e)
    er = e * r
    pos = z >= 0.0
    sig = jnp.where(pos, r, er)
    sig_neg = jnp.where(pos, er, r)
    return (1.0 - lb) * sig_neg, jnp.log(lb + (1.0 - lb) * sig)


def _silu(x):
    return x * _sigmoid(x)


def _chunk_levels(c):
    levels, m = [], 2 * SUBLANES
    while m <= c:
        levels.append(m)
        m *= 2
    return levels


def _chunk_scan(q, k, v, g, s_ref):
    c = q.shape[0]
    dv = v.shape[1]
    row = lax.broadcasted_iota(jnp.int32, (c, c), 0)
    col = lax.broadcasted_iota(jnp.int32, (c, c), 1)
    tri = (col <= row).astype(BF16)
    g_hi = g.astype(BF16)
    g_lo = (g - g_hi.astype(F32)).astype(BF16)
    cum = (jnp.dot(tri, g_hi, preferred_element_type=F32)
           + jnp.dot(tri, g_lo, preferred_element_type=F32))

    s_old = s_ref[...]
    vb = v.astype(BF16)
    o = jnp.dot((q * jnp.exp(cum)).astype(BF16), s_old.astype(BF16), preferred_element_type=F32)

    rowl = lax.broadcasted_iota(jnp.int32, (c, LANES), 0)
    a_off = jnp.zeros((c, c), F32)
    for m in _chunk_levels(c):
        hm = m // 2
        mid = cum.reshape(c // m, m, LANES)[:, hm - 1:hm, :]
        mid = jnp.broadcast_to(mid, (c // m, m, LANES)).reshape(c, LANES)
        second = jnp.bitwise_and(rowl, m - 1) >= hm
        x = (jnp.where(second, q, k) * jnp.exp(-jnp.abs(cum - mid))).astype(BF16)
        am = lax.dot_general(x, x, (((1,), (1,)), ((), ())), preferred_element_type=F32)
        sh = m.bit_length() - 1
        mask = ((lax.shift_right_logical(row, sh) == lax.shift_right_logical(col, sh))
                & (jnp.bitwise_and(row, m - 1) >= hm) & (jnp.bitwise_and(col, m - 1) < hm))
        a_off = jnp.where(mask, am, a_off)
    o = o + jnp.dot(a_off.astype(BF16), vb, preferred_element_type=F32)

    nb = c // SUBLANES
    cum3 = cum.reshape(nb, SUBLANES, LANES)
    q3 = q.reshape(nb, SUBLANES, LANES)
    k3 = k.reshape(nb, SUBLANES, LANES)
    v3 = v.reshape(nb, SUBLANES, dv)
    tl = lax.broadcasted_iota(jnp.int32, (nb, SUBLANES, 1), 1)
    o3 = jnp.zeros((nb, SUBLANES, dv), F32)
    for s in range(SUBLANES):
        w = jnp.exp(jnp.minimum(cum3 - cum3[:, s:s + 1, :], 0.0))
        a = jnp.sum(q3 * (k3[:, s:s + 1, :] * w), axis=-1, keepdims=True)
        a = jnp.where(tl >= s, a, 0.0)
        o3 = o3 + a * v3[:, s:s + 1, :]
    o = o + o3.reshape(c, dv)

    last = cum[c - 1:c, :]
    kt = k * jnp.exp(last - cum)
    kt_t = jnp.transpose(kt).astype(BF16)
    dec = jnp.broadcast_to(jnp.exp(last), (LANES, LANES))
    dec_col = jnp.sum(jnp.where(row == col, dec, 0.0), axis=1, keepdims=True)
    s_ref[...] = s_old * dec_col + jnp.dot(kt_t, vb, preferred_element_type=F32)
    return o


def _gla_prompt_kernel(q_ref, k_ref, v_ref, r_ref, a_ref, wup_ref, bg_ref, gn_ref, o_ref, s_ref):
    s_ref[...] = jnp.zeros_like(s_ref)

    def body(ci, carry):
        rows = pl.ds(pl.multiple_of(ci * CHUNK, CHUNK), CHUNK)
        q = q_ref[rows, :] * (GLA_DK ** -0.5)
        g = _gla_gate(a_ref[rows, :], wup_ref, bg_ref)
        o = _chunk_scan(q, k_ref[rows, :], v_ref[rows, :], g, s_ref)
        o_ref[rows, :] = (_rms(o, gn_ref[...]) * _silu(r_ref[rows, :])).astype(BF16)
        return carry

    lax.fori_loop(0, SEQ // CHUNK, body, 0)


def _gla_prompt(proj, wup, bg, gn, layer):
    cq, ck, cv, cr, ca = (COL_Q // LANES, COL_K // LANES, COL_V // GLA_DV, COL_R // GLA_DV,
                          COL_A // LANES)
    return pl.pallas_call(
        _gla_prompt_kernel,
        grid=(BATCH, GLA_HEADS),
        in_specs=[pl.BlockSpec((SEQ, GLA_DK), lambda b, h: (b, cq + h)),
                  pl.BlockSpec((SEQ, GLA_DK), lambda b, h: (b, ck + h)),
                  pl.BlockSpec((SEQ, GLA_DV), lambda b, h: (b, cv + h)),
                  pl.BlockSpec((SEQ, GLA_DV), lambda b, h: (b, cr + h)),
                  pl.BlockSpec((SEQ, LANES), lambda b, h: (b, ca)),
                  pl.BlockSpec((None, LANES, GLA_DK), lambda b, h: (layer, 0, h)),
                  pl.BlockSpec((None, 1, GLA_DK), lambda b, h: (layer, 0, h)),
                  pl.BlockSpec((None, 1, GLA_DV), lambda b, h: (layer, 0, 0))],
        out_specs=[pl.BlockSpec((SEQ, GLA_DV), lambda b, h: (b, h)),
                   pl.BlockSpec((None, None, GLA_DK, GLA_DV), lambda b, h: (b, h, 0, 0))],
        out_shape=[jax.ShapeDtypeStruct((N_PROMPT, GLA_WIDTH), BF16),
                   jax.ShapeDtypeStruct((BATCH, GLA_HEADS, GLA_DK, GLA_DV), F32)],
        compiler_params=_params(("arbitrary", "arbitrary")),
        name="gla_prompt",
    )(proj, proj, proj, proj, proj, wup, bg, gn)


def _hgrn_prompt_kernel(hq_ref, hf_ref, hi_ref, hg_ref, gam_ref, hn_ref, o_ref, s_ref, *, layer):
    s_ref[...] = jnp.zeros_like(s_ref)
    lb = _hgrn_lower_bound(gam_ref, layer)

    def body(ci, carry):
        rows = pl.ds(pl.multiple_of(ci * CHUNK, CHUNK), CHUNK)
        q = _silu(hq_ref[rows, :]) * (HGRN_DK ** -0.5)
        k, g = _hgrn_gate(hf_ref[rows, :], lb)
        o = _chunk_scan(q, k, hi_ref[rows, :], g, s_ref)
        o_ref[rows, :] = (_rms(o, hn_ref[...]) * _silu(hg_ref[rows, :])).astype(BF16)
        return carry

    lax.fori_loop(0, SEQ // CHUNK, body, 0)


def _hgrn_prompt(proj, gamma, hn, layer):
    cq, cf, ci, cg = (COL_HQ // LANES, COL_HF // LANES, COL_HI // LANES, COL_HG // LANES)
    return pl.pallas_call(
        functools.partial(_hgrn_prompt_kernel, layer=layer),
        grid=(BATCH, HGRN_HEADS),
        in_specs=[pl.BlockSpec((SEQ, HGRN_DK), lambda b, h: (b, cq + h)),
                  pl.BlockSpec((SEQ, HGRN_DK), lambda b, h: (b, cf + h)),
                  pl.BlockSpec((SEQ, HGRN_DV), lambda b, h: (b, ci + h)),
                  pl.BlockSpec((SEQ, HGRN_DV), lambda b, h: (b, cg + h)),
                  pl.BlockSpec((DEPTH, HGRN_DK), lambda b, h: (0, h)),
                  pl.BlockSpec((None, 1, HGRN_DV), lambda b, h: (layer, 0, 0))],
        out_specs=[pl.BlockSpec((SEQ, HGRN_DV), lambda b, h: (b, h)),
                   pl.BlockSpec((None, None, HGRN_DK, HGRN_DV), lambda b, h: (b, h, 0, 0))],
        out_shape=[jax.ShapeDtypeStruct((N_PROMPT, HGRN_WIDTH), BF16),
                   jax.ShapeDtypeStruct((BATCH, HGRN_HEADS, HGRN_DK, HGRN_DV), F32)],
        compiler_params=_params(("arbitrary", "arbitrary")),
        name="hgrn_prompt",
    )(proj, proj, proj, proj, gamma, hn)


NBLK = DEC_BATCH // SAMPLE_NB


def _sample_setup(q, k, v, g, oin_s, qt_s, kt_s, dt_s):
    cum = [g[0]]
    for t in range(1, DEC_SEQ):
        cum.append(cum[t - 1] + g[t])
    for t in range(DEC_SEQ):
        o = jnp.zeros_like(v[0])
        for s in range(t + 1):
            w = q[t] * k[s] if s == t else q[t] * (k[s] * jnp.exp(cum[t] - cum[s]))
            o = o + jnp.sum(w, axis=-1, keepdims=True) * v[s]
        oin_s[t] = o
    last = cum[DEC_SEQ - 1]
    for t in range(DEC_SEQ):
        qt = jnp.transpose(q[t] * jnp.exp(cum[t]))
        kt = jnp.transpose(k[t] * jnp.exp(last - cum[t]))
        for blk in range(NBLK):
            cols = slice(blk * SAMPLE_NB, (blk + 1) * SAMPLE_NB)
            qt_s[t, blk] = qt[:, cols]
            kt_s[t, blk] = kt[:, cols]
    dt = jnp.transpose(jnp.exp(last))
    for blk in range(NBLK):
        dt_s[blk] = dt[:, blk * SAMPLE_NB:(blk + 1) * SAMPLE_NB]


def _sample_step(bb, v_ref, sin_ref, sout_ref, oi_s, qt_s, kt_s, dt_s):
    dcols = dt_s[bb]
    qcols = [qt_s[t, bb] for t in range(DEC_SEQ)]
    kcols = [kt_s[t, bb] for t in range(DEC_SEQ)]
    for j in range(SAMPLE_NB):
        s_old = sin_ref[j]
        s_new = s_old * dcols[:, j:j + 1]
        for t in range(DEC_SEQ):
            oi_s[t, j:j + 1, :] = jnp.sum(qcols[t][:, j:j + 1] * s_old, axis=0, keepdims=True)
            vrow = v_ref[t, pl.ds(bb * SAMPLE_NB + j, 1), :]
            s_new = s_new + kcols[t][:, j:j + 1] * vrow
        sout_ref[j] = s_new


def _gla_sample_kernel(q_ref, k_ref, v_ref, r_ref, a_ref, wup_ref, bg_ref, gn_ref, sin_ref,
                       o_ref, sout_ref, oin_s, oi_s, qt_s, kt_s, dt_s):
    bb = pl.program_id(1)

    @pl.when(bb == 0)
    def _():
        q = [q_ref[t] * (GLA_DK ** -0.5) for t in range(DEC_SEQ)]
        k = [k_ref[t] for t in range(DEC_SEQ)]
        v = [v_ref[t] for t in range(DEC_SEQ)]
        g = [_gla_gate(a_ref[t], wup_ref, bg_ref) for t in range(DEC_SEQ)]
        _sample_setup(q, k, v, g, oin_s, qt_s, kt_s, dt_s)

    _sample_step(bb, v_ref, sin_ref, sout_ref, oi_s, qt_s, kt_s, dt_s)
    rows = pl.ds(pl.multiple_of(bb * SAMPLE_NB, SAMPLE_NB), SAMPLE_NB)
    for t in range(DEC_SEQ):
        o = oi_s[t] + oin_s[t, rows, :]
        o_ref[t, rows, :] = (_rms(o, gn_ref[...]) * _silu(r_ref[t, rows, :])).astype(BF16)


def _sample_scratch(dv):
    return [pltpu.VMEM((DEC_SEQ, DEC_BATCH, dv), F32),
            pltpu.VMEM((DEC_SEQ, SAMPLE_NB, dv), F32),
            pltpu.VMEM((DEC_SEQ, NBLK, LANES, SAMPLE_NB), F32),
            pltpu.VMEM((DEC_SEQ, NBLK, LANES, SAMPLE_NB), F32),
            pltpu.VMEM((NBLK, LANES, SAMPLE_NB), F32)]


def _gla_sample(proj_s, wup, bg, gn, state_in, state_prev, layer):
    cq, ck, cv, cr, ca = (COL_Q // LANES, COL_K // LANES, COL_V // GLA_DV, COL_R // GLA_DV,
                          COL_A // LANES)
    tile = lambda c: (lambda h, bb: (0, 0, c + h))
    st_spec = pl.BlockSpec((None, SAMPLE_NB, None, GLA_DK, GLA_DV),
                           lambda h, bb: (layer, bb, h, 0, 0))
    in_specs = [pl.BlockSpec((DEC_SEQ, DEC_BATCH, GLA_DK), tile(cq)),
                pl.BlockSpec((DEC_SEQ, DEC_BATCH, GLA_DK), tile(ck)),
                pl.BlockSpec((DEC_SEQ, DEC_BATCH, GLA_DV), tile(cv)),
                pl.BlockSpec((DEC_SEQ, DEC_BATCH, GLA_DV), tile(cr)),
                pl.BlockSpec((DEC_SEQ, DEC_BATCH, LANES), lambda h, bb: (0, 0, ca)),
                pl.BlockSpec((None, LANES, GLA_DK), lambda h, bb: (layer, 0, h)),
                pl.BlockSpec((None, 1, GLA_DK), lambda h, bb: (layer, 0, h)),
                pl.BlockSpec((None, 1, GLA_DV), lambda h, bb: (layer, 0, 0)),
                st_spec]
    args = [proj_s, proj_s, proj_s, proj_s, proj_s, wup, bg, gn, state_in]
    aliases = {}
    if state_prev is not None:
        in_specs.append(pl.BlockSpec(memory_space=pl.ANY))
        args.append(state_prev)
        aliases = {len(args) - 1: 1}
    kern = _gla_sample_kernel if state_prev is None else _drop_arg(_gla_sample_kernel, 9)
    return pl.pallas_call(
        kern,
        grid=(GLA_HEADS, NBLK),
        in_specs=in_specs,
        out_specs=[pl.BlockSpec((DEC_SEQ, DEC_BATCH, GLA_DV), lambda h, bb: (0, 0, h)), st_spec],
        out_shape=[jax.ShapeDtypeStruct((DEC_SEQ, DEC_BATCH, GLA_WIDTH), BF16),
                   jax.ShapeDtypeStruct(state_in.shape, F32)],
        scratch_shapes=_sample_scratch(GLA_DV),
        input_output_aliases=aliases,
        compiler_params=_params(("arbitrary", "arbitrary")),
        name="gla_sample",
    )(*args)


def _drop_arg(kern, idx):
    def wrapped(*refs):
        return kern(*refs[:idx], *refs[idx + 1:])
    return wrapped


def _hgrn_sample_kernel(hq_ref, hf_ref, hi_ref, hg_ref, gam_ref, hn_ref, sin_ref,
                        o_ref, sout_ref, oin_s, oi_s, qt_s, kt_s, dt_s, *, layer):
    bb = pl.program_id(1)

    @pl.when(bb == 0)
    def _():
        lb = _hgrn_lower_bound(gam_ref, layer)
        q = [_silu(hq_ref[t]) * (HGRN_DK ** -0.5) for t in range(DEC_SEQ)]
        kg = [_hgrn_gate(hf_ref[t], lb) for t in range(DEC_SEQ)]
        v = [hi_ref[t] for t in range(DEC_SEQ)]
        _sample_setup(q, [x[0] for x in kg], v, [x[1] for x in kg], oin_s, qt_s, kt_s, dt_s)

    _sample_step(bb, hi_ref, sin_ref, sout_ref, oi_s, qt_s, kt_s, dt_s)
    rows = pl.ds(pl.multiple_of(bb * SAMPLE_NB, SAMPLE_NB), SAMPLE_NB)
    for t in range(DEC_SEQ):
        o = oi_s[t] + oin_s[t, rows, :]
        o_ref[t, rows, :] = (_rms(o, hn_ref[...]) * _silu(hg_ref[t, rows, :])).astype(BF16)


def _hgrn_sample(proj_s, gamma, hn, state_in, state_prev, layer):
    cq, cf, ci, cg = (COL_HQ // LANES, COL_HF // LANES, COL_HI // LANES, COL_HG // LANES)
    tile = lambda c: (lambda h, bb: (0, 0, c + h))
    st_spec = pl.BlockSpec((None, SAMPLE_NB, None, HGRN_DK, HGRN_DV),
                           lambda h, bb: (layer, bb, h, 0, 0))
    in_specs = [pl.BlockSpec((DEC_SEQ, DEC_BATCH, HGRN_DK), tile(cq)),
                pl.BlockSpec((DEC_SEQ, DEC_BATCH, HGRN_DK), tile(cf)),
                pl.BlockSpec((DEC_SEQ, DEC_BATCH, HGRN_DV), tile(ci)),
                pl.BlockSpec((DEC_SEQ, DEC_BATCH, HGRN_DV), tile(cg)),
                pl.BlockSpec((DEPTH, HGRN_DK), lambda h, bb: (0, h)),
                pl.BlockSpec((None, 1, HGRN_DV), lambda h, bb: (layer, 0, 0)),
                st_spec]
    args = [proj_s, proj_s, proj_s, proj_s, gamma, hn, state_in]
    aliases = {}
    kern = functools.partial(_hgrn_sample_kernel, layer=layer)
    if state_prev is not None:
        in_specs.append(pl.BlockSpec(memory_space=pl.ANY))
        args.append(state_prev)
        aliases = {len(args) - 1: 1}
        kern = _drop_arg(kern, 7)
    return pl.pallas_call(
        kern,
        grid=(HGRN_HEADS, NBLK),
        in_specs=in_specs,
        out_specs=[pl.BlockSpec((DEC_SEQ, DEC_BATCH, HGRN_DV), lambda h, bb: (0, 0, h)), st_spec],
        out_shape=[jax.ShapeDtypeStruct((DEC_SEQ, DEC_BATCH, HGRN_WIDTH), BF16),
                   jax.ShapeDtypeStruct(state_in.shape, F32)],
        scratch_shapes=_sample_scratch(HGRN_DV),
        input_output_aliases=aliases,
        compiler_params=_params(("arbitrary", "arbitrary")),
        name="hgrn_sample",
    )(*args)


def _regroup_mix_w_in(w):
    pad = jnp.zeros(w.shape[:2] + (PROJ_W - w.shape[2],), w.dtype)
    return jnp.concatenate([w[..., :2048], w[..., 2064:], w[..., 2048:2064], pad], axis=-1)


def kernel(x_prompt, x_sample, state_gla, state_hgrn, norm_gains, ffn1_w_in, ffn1_w_out,
           ffn2_w_in, ffn2_w_out, mix_w_in, gla_w_gate_up, gla_b_gate, gla_norm, hgrn_gamma,
           hgrn_norm, mix_w_out):
    x = jnp.concatenate([x_prompt.reshape(N_PROMPT, D_MODEL),
                         jnp.transpose(x_sample, (1, 0, 2)).reshape(N_SAMPLE, D_MODEL)], axis=0)
    gains = norm_gains.reshape(DEPTH * 6, 1, D_MODEL)
    w_mix = _regroup_mix_w_in(mix_w_in)
    wup = jnp.pad(gla_w_gate_up, ((0, 0), (0, LANES - GLA_RANK), (0, 0)))
    bg = gla_b_gate.reshape(DEPTH, 1, GLA_HEADS * GLA_DK)
    gn = gla_norm.reshape(DEPTH, 1, GLA_DV)
    hn = hgrn_norm.reshape(DEPTH, 1, HGRN_DV)

    h = _rms_cast(x, gains, 0)
    gla_p, hgrn_p = [], []
    st_gla, st_hgrn = None, None
    for l in range(DEPTH):
        base = 6 * l
        act = _ffn_in(h, ffn1_w_in, l)
        x, h = _proj_out(act, ffn1_w_out, l, x, gains, base + 1, base + 2, 0.5)
        proj = _mix_in(h, w_mix, l)
        proj_s = proj[N_PROMPT:].reshape(DEC_SEQ, DEC_BATCH, PROJ_W)
        og, sg = _gla_prompt(proj, wup, bg, gn, l)
        oh, sh = _hgrn_prompt(proj, hgrn_gamma, hn, l)
        ogs, st_gla = _gla_sample(proj_s, wup, bg, gn, state_gla, st_gla, l)
        ohs, st_hgrn = _hgrn_sample(proj_s, hgrn_gamma, hn, state_hgrn, st_hgrn, l)
        gla_p.append(sg)
        hgrn_p.append(sh)
        merged = jnp.concatenate(
            [jnp.concatenate([og, oh], axis=1),
             jnp.concatenate([ogs, ohs], axis=2).reshape(N_SAMPLE, D_MODEL)], axis=0)
        x, h = _proj_out(merged, mix_w_out, l, x, gains, base + 3, base + 4, 1.0)
        act = _ffn_in(h, ffn2_w_in, l)
        x, h = _proj_out(act, ffn2_w_out, l, x, gains, base + 5, (base + 6) % (6 * DEPTH), 0.5)

    y_prompt = x[:N_PROMPT].reshape(BATCH, SEQ, D_MODEL)
    y_sample = jnp.transpose(x[N_PROMPT:].reshape(DEC_SEQ, DEC_BATCH, D_MODEL), (1, 0, 2))
    return (y_prompt, y_sample, jnp.stack(gla_p), jnp.stack(hgrn_p), st_gla, st_hgrn)
```

```python
import functools

import jax
import jax.numpy as jnp
from jax import lax
from jax.experimental import pallas as pl
from jax.experimental.pallas import tpu as pltpu

F32 = jnp.float32
BF16 = jnp.bfloat16

D_MODEL = 2048
BATCH = 4
SEQ = 2048
DEPTH = 2
DEC_BATCH = 128
DEC_SEQ = 4
GLA_HEADS = 4
GLA_DK = 128
GLA_DV = 256
GLA_WIDTH = GLA_HEADS * GLA_DV
GLA_RANK = 16
GLA_TAU = 16.0
HGRN_HEADS = 8
HGRN_DK = 128
HGRN_DV = 128
HGRN_WIDTH = HGRN_HEADS * HGRN_DV
D_FF = 5632
EPS = 1e-6

N_PROMPT = BATCH * SEQ
N_SAMPLE = DEC_BATCH * DEC_SEQ
N_ROWS = N_PROMPT + N_SAMPLE
SAMPLE_ROW_BLOCK = N_PROMPT // N_SAMPLE

LANES = 128
SUBLANES = 8
VMEM_LIMIT = 56 * 1024 * 1024

COL_Q = 0
COL_K = 512
COL_V = 1024
COL_R = 2048
COL_HQ = 3072
COL_HF = 4096
COL_HI = 5120
COL_HG = 6144
COL_A = 7168
PROJ_W = 7680

TM_IN = 1088
TN = 512
TM_OUT = 1088
TK = 512
OUT_NC = 512
OUT_RC = 272
TM_ROW = 544
CHUNK = 128
SAMPLE_NB = 16
NBLK = DEC_BATCH // SAMPLE_NB


def _params(sem):
    return pltpu.CompilerParams(dimension_semantics=sem, vmem_limit_bytes=VMEM_LIMIT)


def _sigmoid(x):
    return 1.0 / (1.0 + jnp.exp(-x))


def _silu(x):
    return x * _sigmoid(x)


def _rms(y, g):
    return y * lax.rsqrt(jnp.mean(y * y, axis=-1, keepdims=True) + EPS) * g


def _skip_carries(kern, n_in, n_carry):
    def wrapped(*refs):
        return kern(*refs[:n_in], *refs[n_in + n_carry:])
    return wrapped


def _rms_cast_kernel(x_ref, g_ref, h_ref):
    h_ref[...] = _rms(x_ref[...], g_ref[...]).astype(BF16)


def _rms_cast(x, gains, gi):
    return pl.pallas_call(
        _rms_cast_kernel,
        grid=(N_ROWS // TM_ROW,),
        in_specs=[pl.BlockSpec((TM_ROW, D_MODEL), lambda m: (m, 0)),
                  pl.BlockSpec((None, 1, D_MODEL), lambda m: (gi, 0, 0))],
        out_specs=pl.BlockSpec((TM_ROW, D_MODEL), lambda m: (m, 0)),
        out_shape=jax.ShapeDtypeStruct((N_ROWS, D_MODEL), BF16),
        compiler_params=_params(("arbitrary",)),
        name="rms_cast",
    )(x, gains)


def _ffn_in_kernel(h_ref, wg_ref, wu_ref, o_ref, wg_s, wu_s):
    @pl.when(pl.program_id(1) == 0)
    def _():
        wg_s[...] = wg_ref[...].astype(BF16)
        wu_s[...] = wu_ref[...].astype(BF16)

    h = h_ref[...]
    g = jnp.dot(h, wg_s[...], preferred_element_type=F32)
    u = jnp.dot(h, wu_s[...], preferred_element_type=F32)
    o_ref[...] = (g * _sigmoid(g) * u).astype(BF16)


def _ffn_in(h, w_in, layer):
    nt = D_FF // TN
    return pl.pallas_call(
        _ffn_in_kernel,
        grid=(nt, N_ROWS // TM_IN),
        in_specs=[pl.BlockSpec((TM_IN, D_MODEL), lambda n, m: (m, 0)),
                  pl.BlockSpec((None, D_MODEL, TN), lambda n, m: (layer, 0, n)),
                  pl.BlockSpec((None, D_MODEL, TN), lambda n, m: (layer, 0, n + nt))],
        out_specs=pl.BlockSpec((TM_IN, TN), lambda n, m: (m, n)),
        out_shape=jax.ShapeDtypeStruct((N_ROWS, D_FF), BF16),
        scratch_shapes=[pltpu.VMEM((D_MODEL, TN), BF16), pltpu.VMEM((D_MODEL, TN), BF16)],
        compiler_params=_params(("arbitrary", "arbitrary")),
        name="ffn_in",
    )(h, w_in, w_in)


def _mix_in_kernel(h_ref, w_ref, o_ref, z_ref, w_s):
    @pl.when(pl.program_id(1) == 0)
    def _():
        w_s[...] = w_ref[...].astype(BF16)

    @pl.when(pl.program_id(0) == 0)
    def _():
        z_ref[...] = jnp.zeros_like(z_ref)

    o_ref[...] = jnp.dot(h_ref[...], w_s[...], preferred_element_type=F32)


def _mix_in(h, w_perm, layer):
    last = N_ROWS // TM_IN - 1
    return pl.pallas_call(
        _mix_in_kernel,
        grid=(PROJ_W // TN, N_ROWS // TM_IN),
        in_specs=[pl.BlockSpec((TM_IN, D_MODEL), lambda n, m: (m, 0)),
                  pl.BlockSpec((None, D_MODEL, TN), lambda n, m: (layer, 0, n))],
        out_specs=[pl.BlockSpec((TM_IN, TN), lambda n, m: (m, n)),
                   pl.BlockSpec((TM_IN, D_MODEL), lambda n, m: (jnp.where(n == 0, m, last), 0))],
        out_shape=[jax.ShapeDtypeStruct((N_ROWS, PROJ_W), F32),
                   jax.ShapeDtypeStruct((N_ROWS, D_MODEL), BF16)],
        scratch_shapes=[pltpu.VMEM((D_MODEL, TN), BF16)],
        compiler_params=_params(("arbitrary", "arbitrary")),
        name="mix_in",
    )(h, w_perm)


def _proj_out_kernel(a_ref, w_ref, x_hbm, gp_ref, gn_ref, xo_ref, ho_ref, xbuf, sem, *, alpha, nk):
    m = pl.program_id(0)
    k = pl.program_id(1)

    def x_copy():
        rows = pl.ds(pl.multiple_of(m * TM_OUT, TM_OUT), TM_OUT)
        return pltpu.make_async_copy(x_hbm.at[rows, :], xbuf, sem)

    @pl.when(k == 0)
    def _():
        x_copy().start()
        xo_ref[...] = jnp.zeros_like(xo_ref)

    a = a_ref[...]
    for c in range(D_MODEL // OUT_NC):
        cols = slice(c * OUT_NC, (c + 1) * OUT_NC)
        xo_ref[:, cols] += jnp.dot(a, w_ref[:, cols].astype(BF16), preferred_element_type=F32)

    @pl.when(k == nk - 1)
    def _():
        x_copy().wait()

        def rows_body(i, carry):
            rows = pl.ds(pl.multiple_of(i * OUT_RC, OUT_RC), OUT_RC)
            xn = xbuf[rows, :] + alpha * _rms(xo_ref[rows, :], gp_ref[...])
            xo_ref[rows, :] = xn
            ho_ref[rows, :] = _rms(xn, gn_ref[...]).astype(BF16)
            return carry

        lax.fori_loop(0, TM_OUT // OUT_RC, rows_body, 0)


def _proj_out(a, w, layer, x, gains, gi_post, gi_next, alpha):
    kdim = a.shape[1]
    nk = kdim // TK
    kern = functools.partial(_proj_out_kernel, alpha=alpha, nk=nk)
    return pl.pallas_call(
        kern,
        grid=(N_ROWS // TM_OUT, nk),
        in_specs=[pl.BlockSpec((TM_OUT, TK), lambda m, k: (m, k)),
                  pl.BlockSpec((None, TK, D_MODEL), lambda m, k: (layer, k, 0)),
                  pl.BlockSpec(memory_space=pl.ANY),
                  pl.BlockSpec((None, 1, D_MODEL), lambda m, k: (gi_post, 0, 0)),
                  pl.BlockSpec((None, 1, D_MODEL), lambda m, k: (gi_next, 0, 0))],
        out_specs=[pl.BlockSpec((TM_OUT, D_MODEL), lambda m, k: (m, 0)),
                   pl.BlockSpec((TM_OUT, D_MODEL), lambda m, k: (m, 0))],
        out_shape=[jax.ShapeDtypeStruct((N_ROWS, D_MODEL), F32),
                   jax.ShapeDtypeStruct((N_ROWS, D_MODEL), BF16)],
        scratch_shapes=[pltpu.VMEM((TM_OUT, D_MODEL), F32), pltpu.SemaphoreType.DMA(())],
        compiler_params=_params(("arbitrary", "arbitrary")),
        name="proj_out",
    )(a, w, x, gains, gains)


def _log_sigmoid(x):
    return jnp.minimum(x, 0.0) - jnp.log(1.0 + jnp.exp(-jnp.abs(x)))


def _gla_gate(a, wup_ref, bg_ref):
    pre = jnp.dot(a.astype(BF16), wup_ref[...].astype(BF16), preferred_element_type=F32)
    return _log_sigmoid(pre + bg_ref[...]) * (1.0 / GLA_TAU)


def _hgrn_lower_bound(gam_ref, layer):
    gam = gam_ref[...]
    mx = jnp.max(gam, axis=0, keepdims=True)
    e = jnp.exp(gam - mx)
    probs = e / jnp.sum(e, axis=0, keepdims=True)
    acc = probs[0:1, :]
    for j in range(1, layer + 1):
        acc = acc + probs[j:j + 1, :]
    return acc - probs[0:1, :]


def _hgrn_gate(z, lb):
    e = jnp.exp(-jnp.abs(z))
    r = 1.0 / (1.0 + e)
    er = e * r
    pos = z >= 0.0
    sig = jnp.where(pos, r, er)
    sig_neg = jnp.where(pos, er, r)
    return (1.0 - lb) * sig_neg, jnp.log(lb + (1.0 - lb) * sig)


def _chunk_levels(c):
    levels, m = [], 2 * SUBLANES
    while m <= c:
        levels.append(m)
        m *= 2
    return levels


def _chunk_scan(q, k, v, g, s_ref):
    c = q.shape[0]
    dv = v.shape[1]
    row = lax.broadcasted_iota(jnp.int32, (c, c), 0)
    col = lax.broadcasted_iota(jnp.int32, (c, c), 1)
    tri = (col <= row).astype(BF16)
    g_hi = g.astype(BF16)
    g_lo = (g - g_hi.astype(F32)).astype(BF16)
    cum = (jnp.dot(tri, g_hi, preferred_element_type=F32)
           + jnp.dot(tri, g_lo, preferred_element_type=F32))

    s_old = s_ref[...]
    vb = v.astype(BF16)
    o = jnp.dot((q * jnp.exp(cum)).astype(BF16), s_old.astype(BF16), preferred_element_type=F32)

    rowl = lax.broadcasted_iota(jnp.int32, (c, LANES), 0)
    a_off = jnp.zeros((c, c), F32)
    for m in _chunk_levels(c):
        hm = m // 2
        mid = cum.reshape(c // m, m, LANES)[:, hm - 1:hm, :]
        mid = jnp.broadcast_to(mid, (c // m, m, LANES)).reshape(c, LANES)
        second = jnp.bitwise_and(rowl, m - 1) >= hm
        x = (jnp.where(second, q, k) * jnp.exp(-jnp.abs(cum - mid))).astype(BF16)
        am = lax.dot_general(x, x, (((1,), (1,)), ((), ())), preferred_element_type=F32)
        sh = m.bit_length() - 1
        mask = ((lax.shift_right_logical(row, sh) == lax.shift_right_logical(col, sh))
                & (jnp.bitwise_and(row, m - 1) >= hm) & (jnp.bitwise_and(col, m - 1) < hm))
        a_off = jnp.where(mask, am, a_off)
    o = o + jnp.dot(a_off.astype(BF16), vb, preferred_element_type=F32)

    nb = c // SUBLANES
    cum3 = cum.reshape(nb, SUBLANES, LANES)
    q3 = q.reshape(nb, SUBLANES, LANES)
    k3 = k.reshape(nb, SUBLANES, LANES)
    v3 = v.reshape(nb, SUBLANES, dv)
    tl = lax.broadcasted_iota(jnp.int32, (nb, SUBLANES, 1), 1)
    o3 = jnp.zeros((nb, SUBLANES, dv), F32)
    for s in range(SUBLANES):
        w = jnp.exp(jnp.minimum(cum3 - cum3[:, s:s + 1, :], 0.0))
        a = jnp.sum(q3 * (k3[:, s:s + 1, :] * w), axis=-1, keepdims=True)
        a = jnp.where(tl >= s, a, 0.0)
        o3 = o3 + a * v3[:, s:s + 1, :]
    o = o + o3.reshape(c, dv)

    last = cum[c - 1:c, :]
    kt = k * jnp.exp(last - cum)
    kt_t = jnp.transpose(kt).astype(BF16)
    dec = jnp.broadcast_to(jnp.exp(last), (LANES, LANES))
    dec_col = jnp.sum(jnp.where(row == col, dec, 0.0), axis=1, keepdims=True)
    s_ref[...] = s_old * dec_col + jnp.dot(kt_t, vb, preferred_element_type=F32)
    return o


def _gla_prompt_kernel(q_ref, k_ref, v_ref, r_ref, a_ref, wup_ref, bg_ref, gn_ref, o_ref, s_ref):
    s_ref[...] = jnp.zeros_like(s_ref)

    def body(ci, carry):
        rows = pl.ds(pl.multiple_of(ci * CHUNK, CHUNK), CHUNK)
        q = q_ref[rows, :] * (GLA_DK ** -0.5)
        g = _gla_gate(a_ref[rows, :], wup_ref, bg_ref)
        o = _chunk_scan(q, k_ref[rows, :], v_ref[rows, :], g, s_ref)
        o_ref[rows, :] = (_rms(o, gn_ref[...]) * _silu(r_ref[rows, :])).astype(BF16)
        return carry

    lax.fori_loop(0, SEQ // CHUNK, body, 0, unroll=2)


def _gla_prompt(proj, wup, bg, gn, merged, layer):
    cq, ck, cv, cr, ca = (COL_Q // LANES, COL_K // LANES, COL_V // GLA_DV, COL_R // GLA_DV,
                          COL_A // LANES)
    return pl.pallas_call(
        _skip_carries(_gla_prompt_kernel, 8, 1),
        grid=(BATCH, GLA_HEADS),
        in_specs=[pl.BlockSpec((SEQ, GLA_DK), lambda b, h: (b, cq + h)),
                  pl.BlockSpec((SEQ, GLA_DK), lambda b, h: (b, ck + h)),
                  pl.BlockSpec((SEQ, GLA_DV), lambda b, h: (b, cv + h)),
                  pl.BlockSpec((SEQ, GLA_DV), lambda b, h: (b, cr + h)),
                  pl.BlockSpec((SEQ, LANES), lambda b, h: (b, ca)),
                  pl.BlockSpec((None, LANES, GLA_DK), lambda b, h: (layer, 0, h)),
                  pl.BlockSpec((None, 1, GLA_DK), lambda b, h: (layer, 0, h)),
                  pl.BlockSpec((None, 1, GLA_DV), lambda b, h: (layer, 0, 0)),
                  pl.BlockSpec(memory_space=pl.ANY)],
        out_specs=[pl.BlockSpec((SEQ, GLA_DV), lambda b, h: (b, h)),
                   pl.BlockSpec((None, None, GLA_DK, GLA_DV), lambda b, h: (b, h, 0, 0))],
        out_shape=[jax.ShapeDtypeStruct((N_ROWS, D_MODEL), BF16),
                   jax.ShapeDtypeStruct((BATCH, GLA_HEADS, GLA_DK, GLA_DV), F32)],
        input_output_aliases={8: 0},
        compiler_params=_params(("arbitrary", "arbitrary")),
        name="gla_prompt",
    )(proj, proj, proj, proj, proj, wup, bg, gn, merged)


def _hgrn_prompt_kernel(hq_ref, hf_ref, hi_ref, hg_ref, gam_ref, hn_ref, o_ref, s_ref, *, layer):
    s_ref[...] = jnp.zeros_like(s_ref)
    lb = _hgrn_lower_bound(gam_ref, layer)

    def body(ci, carry):
        rows = pl.ds(pl.multiple_of(ci * CHUNK, CHUNK), CHUNK)
        q = _silu(hq_ref[rows, :]) * (HGRN_DK ** -0.5)
        k, g = _hgrn_gate(hf_ref[rows, :], lb)
        o = _chunk_scan(q, k, hi_ref[rows, :], g, s_ref)
        o_ref[rows, :] = (_rms(o, hn_ref[...]) * _silu(hg_ref[rows, :])).astype(BF16)
        return carry

    lax.fori_loop(0, SEQ // CHUNK, body, 0, unroll=2)


def _hgrn_prompt(proj, gamma, hn, merged, layer):
    cq, cf, ci, cg = (COL_HQ // LANES, COL_HF // LANES, COL_HI // LANES, COL_HG // LANES)
    co = GLA_WIDTH // HGRN_DV
    kern = _skip_carries(functools.partial(_hgrn_prompt_kernel, layer=layer), 6, 1)
    return pl.pallas_call(
        kern,
        grid=(BATCH, HGRN_HEADS),
        in_specs=[pl.BlockSpec((SEQ, HGRN_DK), lambda b, h: (b, cq + h)),
                  pl.BlockSpec((SEQ, HGRN_DK), lambda b, h: (b, cf + h)),
                  pl.BlockSpec((SEQ, HGRN_DV), lambda b, h: (b, ci + h)),
                  pl.BlockSpec((SEQ, HGRN_DV), lambda b, h: (b, cg + h)),
                  pl.BlockSpec((DEPTH, HGRN_DK), lambda b, h: (0, h)),
                  pl.BlockSpec((None, 1, HGRN_DV), lambda b, h: (layer, 0, 0)),
                  pl.BlockSpec(memory_space=pl.ANY)],
        out_specs=[pl.BlockSpec((SEQ, HGRN_DV), lambda b, h: (b, co + h)),
                   pl.BlockSpec((None, None, HGRN_DK, HGRN_DV), lambda b, h: (b, h, 0, 0))],
        out_shape=[jax.ShapeDtypeStruct((N_ROWS, D_MODEL), BF16),
                   jax.ShapeDtypeStruct((BATCH, HGRN_HEADS, HGRN_DK, HGRN_DV), F32)],
        input_output_aliases={6: 0},
        compiler_params=_params(("arbitrary", "arbitrary")),
        name="hgrn_prompt",
    )(proj, proj, proj, proj, gamma, hn, merged)


def _sample_setup(q, k, v, g, oin_s, qh_s, kt_s, dt_s):
    cum = [g[0]]
    for t in range(1, DEC_SEQ):
        cum.append(cum[t - 1] + g[t])
    for t in range(DEC_SEQ):
        o = jnp.zeros_like(v[0])
        for s in range(t + 1):
            w = q[t] * k[s] if s == t else q[t] * (k[s] * jnp.exp(cum[t] - cum[s]))
            o = o + jnp.sum(w, axis=-1, keepdims=True) * v[s]
        oin_s[t] = o
    last = cum[DEC_SEQ - 1]
    for t in range(DEC_SEQ):
        qh_s[t] = q[t] * jnp.exp(cum[t])
        kt = jnp.transpose(k[t] * jnp.exp(last - cum[t]))
        for blk in range(NBLK):
            kt_s[blk, :, t * SAMPLE_NB:(t + 1) * SAMPLE_NB] = (
                kt[:, blk * SAMPLE_NB:(blk + 1) * SAMPLE_NB])
    dt = jnp.transpose(jnp.exp(last))
    for blk in range(NBLK):
        dt_s[blk] = dt[:, blk * SAMPLE_NB:(blk + 1) * SAMPLE_NB]


def _sample_step(bb, v_ref, sin_ref, sout_ref, qh_s, kt_s, dt_s, layer):
    if layer == 0:
        for later in range(1, DEPTH):
            sout_ref[later] = jnp.zeros(sout_ref.shape[1:], F32)
        sout_ref = sout_ref.at[0]
    nb = SAMPLE_NB
    nrow = DEC_SEQ * nb
    r0 = pl.multiple_of(bb * nb, nb)
    qh = jnp.concatenate([qh_s[t, pl.ds(r0, nb), :] for t in range(DEC_SEQ)], axis=0)
    vv = jnp.concatenate(
        [v_ref[pl.ds(pl.multiple_of(t * DEC_BATCH + r0, nb), nb), :] for t in range(DEC_SEQ)],
        axis=0)
    rj = jnp.bitwise_and(lax.broadcasted_iota(jnp.int32, (nrow, nb * LANES), 0), nb - 1)
    cb = lax.shift_right_logical(lax.broadcasted_iota(jnp.int32, (nrow, nb * LANES), 1), 7)
    q_bd = jnp.where(rj == cb, jnp.concatenate([qh] * nb, axis=1), 0.0).astype(BF16)
    s_old = sin_ref[...]
    s_stack = s_old.reshape(nb * LANES, s_old.shape[-1])
    o = jnp.dot(q_bd, s_stack.astype(BF16), preferred_element_type=F32)

    kt64 = kt_s[bb]
    lane_j = jnp.bitwise_and(lax.broadcasted_iota(jnp.int32, kt64.shape, 1), nb - 1)
    kt_bd = jnp.concatenate([jnp.where(lane_j == j, kt64, 0.0) for j in range(nb)],
                            axis=0).astype(BF16)
    upd = jnp.dot(kt_bd, vv.astype(BF16), preferred_element_type=F32)
    dcols = dt_s[bb]
    for j in range(nb):
        sout_ref[j] = s_old[j] * dcols[:, j:j + 1] + upd[j * LANES:(j + 1) * LANES, :]
    return o


def _sample_emit(bb, o, oin_s, gate_ref, norm_ref, o_ref):
    nb = SAMPLE_NB
    r0 = pl.multiple_of(bb * nb, nb)
    for t in range(DEC_SEQ):
        rows = pl.ds(pl.multiple_of(t * DEC_BATCH + r0, nb), nb)
        ot = o[t * nb:(t + 1) * nb, :] + oin_s[t, pl.ds(r0, nb), :]
        o_ref[rows, :] = (_rms(ot, norm_ref[...]) * _silu(gate_ref[rows, :])).astype(BF16)


def _tiles(ref):
    return [ref[t * DEC_BATCH:(t + 1) * DEC_BATCH, :] for t in range(DEC_SEQ)]


def _gla_sample_kernel(q_ref, k_ref, v_ref, r_ref, a_ref, wup_ref, bg_ref, gn_ref, sin_ref,
                       o_ref, sout_ref, oin_s, qh_s, kt_s, dt_s, *, layer):
    bb = pl.program_id(1)

    @pl.when(bb == 0)
    def _():
        q = [x * (GLA_DK ** -0.5) for x in _tiles(q_ref)]
        g = [_gla_gate(x, wup_ref, bg_ref) for x in _tiles(a_ref)]
        _sample_setup(q, _tiles(k_ref), _tiles(v_ref), g, oin_s, qh_s, kt_s, dt_s)

    o = _sample_step(bb, v_ref, sin_ref, sout_ref, qh_s, kt_s, dt_s, layer)
    _sample_emit(bb, o, oin_s, r_ref, gn_ref, o_ref)


def _hgrn_sample_kernel(hq_ref, hf_ref, hi_ref, hg_ref, gam_ref, hn_ref, sin_ref,
                        o_ref, sout_ref, oin_s, qh_s, kt_s, dt_s, *, layer):
    bb = pl.program_id(1)

    @pl.when(bb == 0)
    def _():
        lb = _hgrn_lower_bound(gam_ref, layer)
        q = [_silu(x) * (HGRN_DK ** -0.5) for x in _tiles(hq_ref)]
        kg = [_hgrn_gate(x, lb) for x in _tiles(hf_ref)]
        _sample_setup(q, [x[0] for x in kg], _tiles(hi_ref), [x[1] for x in kg],
                      oin_s, qh_s, kt_s, dt_s)

    o = _sample_step(bb, hi_ref, sin_ref, sout_ref, qh_s, kt_s, dt_s, layer)
    _sample_emit(bb, o, oin_s, hg_ref, hn_ref, o_ref)


def _sample_scratch(dv):
    return [pltpu.VMEM((DEC_SEQ, DEC_BATCH, dv), F32),
            pltpu.VMEM((DEC_SEQ, DEC_BATCH, LANES), F32),
            pltpu.VMEM((NBLK, LANES, DEC_SEQ * SAMPLE_NB), F32),
            pltpu.VMEM((NBLK, LANES, SAMPLE_NB), F32)]


def _sample_call(kern, n_in, in_specs, args, heads, dv, col_out, state_in, merged, state_prev,
                 layer, name):
    st_spec = pl.BlockSpec((None, SAMPLE_NB, None, LANES, dv), lambda h, bb: (layer, bb, h, 0, 0))
    st_out_spec = st_spec
    if state_prev is None:
        st_out_spec = pl.BlockSpec((DEPTH, SAMPLE_NB, None, LANES, dv),
                                   lambda h, bb: (0, bb, h, 0, 0))
    carries = [merged] if state_prev is None else [merged, state_prev]
    in_specs = in_specs + [st_spec] + [pl.BlockSpec(memory_space=pl.ANY)] * len(carries)
    args = args + [state_in] + carries
    aliases = {n_in + 1 + i: i for i in range(len(carries))}
    return pl.pallas_call(
        _skip_carries(kern, n_in + 1, len(carries)),
        grid=(heads, NBLK),
        in_specs=in_specs,
        out_specs=[pl.BlockSpec((N_SAMPLE, dv), lambda h, bb: (SAMPLE_ROW_BLOCK, col_out + h)),
                   st_out_spec],
        out_shape=[jax.ShapeDtypeStruct((N_ROWS, D_MODEL), BF16),
                   jax.ShapeDtypeStruct(state_in.shape, F32)],
        scratch_shapes=_sample_scratch(dv),
        input_output_aliases=aliases,
        compiler_params=_params(("arbitrary", "arbitrary")),
        name=name,
    )(*args)


def _sample_tile(width, col):
    return pl.BlockSpec((N_SAMPLE, width), lambda h, bb: (SAMPLE_ROW_BLOCK, col + h))


def _gla_sample(proj, wup, bg, gn, state_in, merged, state_prev, layer):
    in_specs = [_sample_tile(GLA_DK, COL_Q // LANES),
                _sample_tile(GLA_DK, COL_K // LANES),
                _sample_tile(GLA_DV, COL_V // GLA_DV),
                _sample_tile(GLA_DV, COL_R // GLA_DV),
                pl.BlockSpec((N_SAMPLE, LANES), lambda h, bb: (SAMPLE_ROW_BLOCK, COL_A // LANES)),
                pl.BlockSpec((None, LANES, GLA_DK), lambda h, bb: (layer, 0, h)),
                pl.BlockSpec((None, 1, GLA_DK), lambda h, bb: (layer, 0, h)),
                pl.BlockSpec((None, 1, GLA_DV), lambda h, bb: (layer, 0, 0))]
    args = [proj, proj, proj, proj, proj, wup, bg, gn]
    return _sample_call(functools.partial(_gla_sample_kernel, layer=layer), 8, in_specs, args,
                        GLA_HEADS, GLA_DV, 0, state_in, merged, state_prev, layer, "gla_sample")


def _hgrn_sample(proj, gamma, hn, state_in, merged, state_prev, layer):
    in_specs = [_sample_tile(HGRN_DK, COL_HQ // LANES),
                _sample_tile(HGRN_DK, COL_HF // LANES),
                _sample_tile(HGRN_DV, COL_HI // LANES),
                _sample_tile(HGRN_DV, COL_HG // LANES),
                pl.BlockSpec((DEPTH, HGRN_DK), lambda h, bb: (0, h)),
                pl.BlockSpec((None, 1, HGRN_DV), lambda h, bb: (layer, 0, 0))]
    args = [proj, proj, proj, proj, gamma, hn]
    return _sample_call(functools.partial(_hgrn_sample_kernel, layer=layer), 6, in_specs, args,
                        HGRN_HEADS, HGRN_DV, GLA_WIDTH // HGRN_DV, state_in, merged, state_prev,
                        layer, "hgrn_sample")


def _regroup_mix_w_in(w):
    pad = jnp.zeros(w.shape[:2] + (PROJ_W - w.shape[2],), w.dtype)
    return jnp.concatenate([w[..., :2048], w[..., 2064:], w[..., 2048:2064], pad], axis=-1)


def kernel(x_prompt, x_sample, state_gla, state_hgrn, norm_gains, ffn1_w_in, ffn1_w_out,
           ffn2_w_in, ffn2_w_out, mix_w_in, gla_w_gate_up, gla_b_gate, gla_norm, hgrn_gamma,
           hgrn_norm, mix_w_out):
    x = jnp.concatenate([x_prompt.reshape(N_PROMPT, D_MODEL),
                         jnp.transpose(x_sample, (1, 0, 2)).reshape(N_SAMPLE, D_MODEL)], axis=0)
    gains = norm_gains.reshape(DEPTH * 6, 1, D_MODEL)
    w_mix = _regroup_mix_w_in(mix_w_in)
    wup = jnp.pad(gla_w_gate_up, ((0, 0), (0, LANES - GLA_RANK), (0, 0)))
    bg = gla_b_gate.reshape(DEPTH, 1, GLA_HEADS * GLA_DK)
    gn = gla_norm.reshape(DEPTH, 1, GLA_DV)
    hn = hgrn_norm.reshape(DEPTH, 1, HGRN_DV)

    h = _rms_cast(x, gains, 0)
    gla_p, hgrn_p = [], []
    st_gla, st_hgrn = None, None
    for l in range(DEPTH):
        base = 6 * l
        act = _ffn_in(h, ffn1_w_in, l)
        x, h = _proj_out(act, ffn1_w_out, l, x, gains, base + 1, base + 2, 0.5)
        proj, merged = _mix_in(h, w_mix, l)
        merged, sg = _gla_prompt(proj, wup, bg, gn, merged, l)
        merged, sh = _hgrn_prompt(proj, hgrn_gamma, hn, merged, l)
        merged, st_gla = _gla_sample(proj, wup, bg, gn, state_gla, merged, st_gla, l)
        merged, st_hgrn = _hgrn_sample(proj, hgrn_gamma, hn, state_hgrn, merged, st_hgrn, l)
        gla_p.append(sg)
        hgrn_p.append(sh)
        x, h = _proj_out(merged, mix_w_out, l, x, gains, base + 3, base + 4, 1.0)
        act = _ffn_in(h, ffn2_w_in, l)
        x, h = _proj_out(act, ffn2_w_out, l, x, gains, base + 5, (base + 6) % (6 * DEPTH), 0.5)

    y_prompt = x[:N_PROMPT].reshape(BATCH, SEQ, D_MODEL)
    y_sample = jnp.transpose(x[N_PROMPT:].reshape(DEC_SEQ, DEC_BATCH, D_MODEL), (1, 0, 2))
    return (y_prompt, y_sample, jnp.stack(gla_p), jnp.stack(hgrn_p), st_gla, st_hgrn)
```

```python
import functools

import jax
import jax.numpy as jnp
from jax import lax
from jax.experimental import pallas as pl
from jax.experimental.pallas import tpu as pltpu

F32 = jnp.float32
BF16 = jnp.bfloat16

D_MODEL = 2048
BATCH = 4
SEQ = 2048
DEPTH = 2
DEC_BATCH = 128
DEC_SEQ = 4
GLA_HEADS = 4
GLA_DK = 128
GLA_DV = 256
GLA_WIDTH = GLA_HEADS * GLA_DV
GLA_RANK = 16
GLA_TAU = 16.0
HGRN_HEADS = 8
HGRN_DK = 128
HGRN_DV = 128
HGRN_WIDTH = HGRN_HEADS * HGRN_DV
D_FF = 5632
EPS = 1e-6

N_PROMPT = BATCH * SEQ
N_SAMPLE = DEC_BATCH * DEC_SEQ
N_ROWS = N_PROMPT + N_SAMPLE
SAMPLE_ROW_BLOCK = N_PROMPT // N_SAMPLE

LANES = 128
SUBLANES = 8
VMEM_LIMIT = 60 * 1024 * 1024

COL_Q = 0
COL_K = 512
COL_V = 1024
COL_R = 2048
COL_HQ = 3072
COL_HF = 4096
COL_HI = 5120
COL_HG = 6144
COL_A = 7168
PROJ_W = 7680

TM_FFN = 1088
TM_IN = 1088
TN = 512
FFN_NC = 256
TM_OUT = 1088
TK = 512
TK_WIDE = 1024
OUT_NC = 512
MIX_WO_ROWS = D_MODEL // (N_ROWS // TM_IN)
OUT_RC = 272
TM_ROW = 544
CHUNK = 128
SCAN_HEADS = 2
SCAN_UNROLL = 2
SAMPLE_NB = 16
NBLK = DEC_BATCH // SAMPLE_NB


def _params(sem):
    return pltpu.CompilerParams(dimension_semantics=sem, vmem_limit_bytes=VMEM_LIMIT)


def _sigmoid(x):
    return 1.0 / (1.0 + jnp.exp(-x))


def _silu(x):
    return x * _sigmoid(x)


def _rms(y, g):
    return y * lax.rsqrt(jnp.mean(y * y, axis=-1, keepdims=True) + EPS) * g


def _skip_carries(kern, n_in, n_carry):
    def wrapped(*refs):
        return kern(*refs[:n_in], *refs[n_in + n_carry:])
    return wrapped


def _rms_cast_kernel(x_ref, g_ref, h_ref):
    h_ref[...] = _rms(x_ref[...], g_ref[...]).astype(BF16)


def _rms_cast(x, gains, gi):
    return pl.pallas_call(
        _rms_cast_kernel,
        grid=(N_ROWS // TM_ROW,),
        in_specs=[pl.BlockSpec((TM_ROW, D_MODEL), lambda m: (m, 0)),
                  pl.BlockSpec((None, 1, D_MODEL), lambda m: (gi, 0, 0))],
        out_specs=pl.BlockSpec((TM_ROW, D_MODEL), lambda m: (m, 0)),
        out_shape=jax.ShapeDtypeStruct((N_ROWS, D_MODEL), BF16),
        compiler_params=_params(("arbitrary",)),
        name="rms_cast",
    )(x, gains)


def _ffn_in_kernel(h_ref, wg_ref, wu_ref, wo_ref, o_ref, wob_ref, wg_s, wu_s):
    @pl.when(pl.program_id(1) == 0)
    def _():
        wg_s[...] = wg_ref[...].astype(BF16)
        wu_s[...] = wu_ref[...].astype(BF16)

    wob_ref[...] = wo_ref[...].astype(BF16)
    h = h_ref[...]
    for c in range(TN // FFN_NC):
        cols = slice(c * FFN_NC, (c + 1) * FFN_NC)
        g = jnp.dot(h, wg_s[:, cols], preferred_element_type=F32)
        u = jnp.dot(h, wu_s[:, cols], preferred_element_type=F32)
        o_ref[:, cols] = (g * _sigmoid(g) * u).astype(BF16)


def _ffn_in(h, w_in, w_out, layer):
    nt = D_FF // TN
    nm = N_ROWS // TM_FFN
    slab = D_FF // (nt * nm)
    return pl.pallas_call(
        _ffn_in_kernel,
        grid=(nt, nm),
        in_specs=[pl.BlockSpec((TM_FFN, D_MODEL), lambda n, m: (m, 0)),
                  pl.BlockSpec((None, D_MODEL, TN), lambda n, m: (layer, 0, n)),
                  pl.BlockSpec((None, D_MODEL, TN), lambda n, m: (layer, 0, n + nt)),
                  pl.BlockSpec((None, slab, D_MODEL), lambda n, m: (layer, n * nm + m, 0))],
        out_specs=[pl.BlockSpec((TM_FFN, TN), lambda n, m: (m, n)),
                   pl.BlockSpec((slab, D_MODEL), lambda n, m: (n * nm + m, 0))],
        out_shape=[jax.ShapeDtypeStruct((N_ROWS, D_FF), BF16),
                   jax.ShapeDtypeStruct((D_FF, D_MODEL), BF16)],
        scratch_shapes=[pltpu.VMEM((D_MODEL, TN), BF16), pltpu.VMEM((D_MODEL, TN), BF16)],
        compiler_params=_params(("arbitrary", "arbitrary")),
        name="ffn_in",
    )(h, w_in, w_in, w_out)


def _mix_in_kernel(h_ref, w_ref, wo_ref, o_ref, z_ref, wob_ref, w_s):
    @pl.when(pl.program_id(1) == 0)
    def _():
        w_s[...] = w_ref[...].astype(BF16)

    @pl.when(pl.program_id(0) == 0)
    def _():
        z_ref[...] = jnp.zeros_like(z_ref)
        wob_ref[...] = wo_ref[...].astype(BF16)

    o_ref[...] = jnp.dot(h_ref[...], w_s[...], preferred_element_type=F32)


def _mix_in(h, w_perm, w_out, layer):
    last = N_ROWS // TM_IN - 1
    first_sweep = lambda n, m: jnp.where(n == 0, m, last)
    return pl.pallas_call(
        _mix_in_kernel,
        grid=(PROJ_W // TN, N_ROWS // TM_IN),
        in_specs=[pl.BlockSpec((TM_IN, D_MODEL), lambda n, m: (m, 0)),
                  pl.BlockSpec((None, D_MODEL, TN), lambda n, m: (layer, 0, n)),
                  pl.BlockSpec((None, MIX_WO_ROWS, D_MODEL),
                               lambda n, m: (layer, first_sweep(n, m), 0))],
        out_specs=[pl.BlockSpec((TM_IN, TN), lambda n, m: (m, n)),
                   pl.BlockSpec((TM_IN, D_MODEL), lambda n, m: (first_sweep(n, m), 0)),
                   pl.BlockSpec((MIX_WO_ROWS, D_MODEL), lambda n, m: (first_sweep(n, m), 0))],
        out_shape=[jax.ShapeDtypeStruct((N_ROWS, PROJ_W), F32),
                   jax.ShapeDtypeStruct((N_ROWS, D_MODEL), BF16),
                   jax.ShapeDtypeStruct((D_MODEL, D_MODEL), BF16)],
        scratch_shapes=[pltpu.VMEM((D_MODEL, TN), BF16)],
        compiler_params=_params(("arbitrary", "arbitrary")),
        name="mix_in",
    )(h, w_perm, w_out)


def _proj_out_kernel(a_ref, w_ref, x_hbm, gp_ref, gn_ref, xo_ref, ho_ref, xbuf, sem, *, alpha, nk):
    m = pl.program_id(0)
    k = pl.program_id(1)

    def x_copy():
        rows = pl.ds(pl.multiple_of(m * TM_OUT, TM_OUT), TM_OUT)
        return pltpu.make_async_copy(x_hbm.at[rows, :], xbuf, sem)

    @pl.when(k == 0)
    def _():
        x_copy().start()
        xo_ref[...] = jnp.zeros_like(xo_ref)

    a = a_ref[...]
    for c in range(D_MODEL // OUT_NC):
        cols = slice(c * OUT_NC, (c + 1) * OUT_NC)
        xo_ref[:, cols] += jnp.dot(a, w_ref[:, cols], preferred_element_type=F32)

    @pl.when(k == nk - 1)
    def _():
        x_copy().wait()

        def rows_body(i, carry):
            rows = pl.ds(pl.multiple_of(i * OUT_RC, OUT_RC), OUT_RC)
            xn = xbuf[rows, :] + alpha * _rms(xo_ref[rows, :], gp_ref[...])
            xo_ref[rows, :] = xn
            ho_ref[rows, :] = _rms(xn, gn_ref[...]).astype(BF16)
            return carry

        lax.fori_loop(0, TM_OUT // OUT_RC, rows_body, 0)


def _proj_out(a, w, x, gains, gi_post, gi_next, alpha):
    kdim = a.shape[1]
    tk = TK_WIDE if kdim % TK_WIDE == 0 else TK
    nk = kdim // tk
    kern = functools.partial(_proj_out_kernel, alpha=alpha, nk=nk)
    return pl.pallas_call(
        kern,
        grid=(N_ROWS // TM_OUT, nk),
        in_specs=[pl.BlockSpec((TM_OUT, tk), lambda m, k: (m, k)),
                  pl.BlockSpec((tk, D_MODEL), lambda m, k: (k, 0)),
                  pl.BlockSpec(memory_space=pl.ANY),
                  pl.BlockSpec((None, 1, D_MODEL), lambda m, k: (gi_post, 0, 0)),
                  pl.BlockSpec((None, 1, D_MODEL), lambda m, k: (gi_next, 0, 0))],
        out_specs=[pl.BlockSpec((TM_OUT, D_MODEL), lambda m, k: (m, 0)),
                   pl.BlockSpec((TM_OUT, D_MODEL), lambda m, k: (m, 0))],
        out_shape=[jax.ShapeDtypeStruct((N_ROWS, D_MODEL), F32),
                   jax.ShapeDtypeStruct((N_ROWS, D_MODEL), BF16)],
        scratch_shapes=[pltpu.VMEM((TM_OUT, D_MODEL), F32), pltpu.SemaphoreType.DMA(())],
        compiler_params=_params(("arbitrary", "arbitrary")),
        name="proj_out",
    )(a, w, x, gains, gains)


def _log_sigmoid(x):
    return jnp.minimum(x, 0.0) - jnp.log(1.0 + jnp.exp(-jnp.abs(x)))


def _gla_gate(a, wup_ref, bg_ref):
    pre = jnp.dot(a.astype(BF16), wup_ref[...].astype(BF16), preferred_element_type=F32)
    return _log_sigmoid(pre + bg_ref[...]) * (1.0 / GLA_TAU)


def _hgrn_lower_bound(gam_ref, layer):
    gam = gam_ref[...]
    mx = jnp.max(gam, axis=0, keepdims=True)
    e = jnp.exp(gam - mx)
    probs = e / jnp.sum(e, axis=0, keepdims=True)
    acc = probs[0:1, :]
    for j in range(1, layer + 1):
        acc = acc + probs[j:j + 1, :]
    return acc - probs[0:1, :]


def _hgrn_gate(z, lb):
    e = jnp.exp(-jnp.abs(z))
    r = 1.0 / (1.0 + e)
    er = e * r
    pos = z >= 0.0
    sig = jnp.where(pos, r, er)
    sig_neg = jnp.where(pos, er, r)
    return (1.0 - lb) * sig_neg, jnp.log(lb + (1.0 - lb) * sig)


def _chunk_levels(c):
    levels, m = [], 2
    while m <= c:
        levels.append(m)
        m *= 2
    return levels


def _midpoint_distance(cum, g, m):
    c = cum.shape[0]
    hm = m // 2
    if m == 2:
        odd = jnp.bitwise_and(lax.broadcasted_iota(jnp.int32, cum.shape, 0), 1) == 1
        return jnp.where(odd, jnp.abs(g), 0.0)
    if m < 2 * SUBLANES:
        cum3 = cum.reshape(c // SUBLANES, SUBLANES, LANES)
        sub = lax.broadcasted_iota(jnp.int32, cum3.shape, 1)
        mid = None
        for b0 in range(0, SUBLANES, m):
            cand = jnp.broadcast_to(cum3[:, b0 + hm - 1:b0 + hm, :], cum3.shape)
            mid = cand if mid is None else jnp.where(sub >= b0, cand, mid)
        return jnp.abs(cum3 - mid).reshape(c, LANES)
    mid = cum.reshape(c // m, m, LANES)[:, hm - 1:hm, :]
    mid = jnp.broadcast_to(mid, (c // m, m, LANES)).reshape(c, LANES)
    return jnp.abs(cum - mid)


def _chunk_scan(qs, ks, vs, gs, s_refs):
    heads = range(len(qs))
    c = qs[0].shape[0]
    nt = (((1,), (1,)), ((), ()))
    row = lax.broadcasted_iota(jnp.int32, (c, c), 0)
    col = lax.broadcasted_iota(jnp.int32, (c, c), 1)
    eye = row == col
    tri = (col <= row).astype(BF16)
    g_hi = [g.astype(BF16) for g in gs]
    g_lo = [(gs[i] - g_hi[i].astype(F32)).astype(BF16) for i in heads]
    cums = [jnp.dot(tri, g_hi[i], preferred_element_type=F32)
            + jnp.dot(tri, g_lo[i], preferred_element_type=F32) for i in heads]

    s_old = [s_refs[i][...] for i in heads]
    vb = [v.astype(BF16) for v in vs]
    o = [jnp.dot((qs[i] * jnp.exp(cums[i])).astype(BF16), s_old[i].astype(BF16),
                 preferred_element_type=F32) for i in heads]

    for i in heads:
        last = cums[i][c - 1:c, :]
        kt_t = jnp.transpose(ks[i] * jnp.exp(last - cums[i])).astype(BF16)
        dec = jnp.broadcast_to(jnp.exp(last), (LANES, LANES))
        dec_col = jnp.sum(jnp.where(eye, dec, 0.0), axis=1, keepdims=True)
        s_refs[i][...] = s_old[i] * dec_col + jnp.dot(kt_t, vb[i], preferred_element_type=F32)

    rowl = lax.broadcasted_iota(jnp.int32, (c, LANES), 0)
    a = [jnp.where(eye, lax.dot_general(qs[i].astype(BF16), ks[i].astype(BF16), nt,
                                        preferred_element_type=F32), 0.0) for i in heads]
    for m in _chunk_levels(c):
        hm = m // 2
        sh = m.bit_length() - 1
        second = jnp.bitwise_and(rowl, m - 1) >= hm
        mask = ((lax.shift_right_logical(row, sh) == lax.shift_right_logical(col, sh))
                & (jnp.bitwise_and(row, m - 1) >= hm) & (jnp.bitwise_and(col, m - 1) < hm))
        x = [(jnp.where(second, qs[i], ks[i])
              * jnp.exp(-_midpoint_distance(cums[i], gs[i], m))).astype(BF16) for i in heads]
        am = [lax.dot_general(x[i], x[i], nt, preferred_element_type=F32) for i in heads]
        a = [jnp.where(mask, am[i], a[i]) for i in heads]
    return [o[i] + jnp.dot(a[i].astype(BF16), vb[i], preferred_element_type=F32) for i in heads]


def _head_cols(ref, rows, i, width):
    return ref[rows, i * width:(i + 1) * width]


def _gla_prompt_kernel(q_ref, k_ref, v_ref, r_ref, a_ref, wup_ref, bg_ref, gn_ref, o_ref, s_ref):
    heads = range(SCAN_HEADS)
    s_ref[...] = jnp.zeros_like(s_ref)

    def body(ci, carry):
        rows = pl.ds(pl.multiple_of(ci * CHUNK, CHUNK), CHUNK)
        a = a_ref[rows, :].astype(BF16)
        pre = jnp.dot(a, wup_ref[...].astype(BF16), preferred_element_type=F32) + bg_ref[...]
        gs = [_log_sigmoid(pre[:, i * GLA_DK:(i + 1) * GLA_DK]) * (1.0 / GLA_TAU) for i in heads]
        qs = [_head_cols(q_ref, rows, i, GLA_DK) * (GLA_DK ** -0.5) for i in heads]
        ks = [_head_cols(k_ref, rows, i, GLA_DK) for i in heads]
        vs = [_head_cols(v_ref, rows, i, GLA_DV) for i in heads]
        os = _chunk_scan(qs, ks, vs, gs, [s_ref.at[i] for i in heads])
        for i in heads:
            gate = _silu(_head_cols(r_ref, rows, i, GLA_DV))
            o_ref[rows, i * GLA_DV:(i + 1) * GLA_DV] = (_rms(os[i], gn_ref[...]) * gate).astype(BF16)
        return carry

    lax.fori_loop(0, SEQ // CHUNK, body, 0, unroll=SCAN_UNROLL)


def _gla_prompt(proj, wup, bg, gn, merged, layer):
    p = SCAN_HEADS
    dk, dv = p * GLA_DK, p * GLA_DV
    cq, ck, cv, cr, ca = (COL_Q // dk, COL_K // dk, COL_V // dv, COL_R // dv, COL_A // LANES)
    return pl.pallas_call(
        _skip_carries(_gla_prompt_kernel, 8, 1),
        grid=(BATCH, GLA_HEADS // p),
        in_specs=[pl.BlockSpec((SEQ, dk), lambda b, h: (b, cq + h)),
                  pl.BlockSpec((SEQ, dk), lambda b, h: (b, ck + h)),
                  pl.BlockSpec((SEQ, dv), lambda b, h: (b, cv + h)),
                  pl.BlockSpec((SEQ, dv), lambda b, h: (b, cr + h)),
                  pl.BlockSpec((SEQ, LANES), lambda b, h: (b, ca)),
                  pl.BlockSpec((None, LANES, dk), lambda b, h: (layer, 0, h)),
                  pl.BlockSpec((None, 1, dk), lambda b, h: (layer, 0, h)),
                  pl.BlockSpec((None, 1, GLA_DV), lambda b, h: (layer, 0, 0)),
                  pl.BlockSpec(memory_space=pl.ANY)],
        out_specs=[pl.BlockSpec((SEQ, dv), lambda b, h: (b, h)),
                   pl.BlockSpec((None, p, GLA_DK, GLA_DV), lambda b, h: (b, h, 0, 0))],
        out_shape=[jax.ShapeDtypeStruct((N_ROWS, D_MODEL), BF16),
                   jax.ShapeDtypeStruct((BATCH, GLA_HEADS, GLA_DK, GLA_DV), F32)],
        input_output_aliases={8: 0},
        compiler_params=_params(("arbitrary", "arbitrary")),
        name="gla_prompt",
    )(proj, proj, proj, proj, proj, wup, bg, gn, merged)


def _hgrn_prompt_kernel(hq_ref, hf_ref, hi_ref, hg_ref, gam_ref, hn_ref, o_ref, s_ref, *, layer):
    heads = range(SCAN_HEADS)
    s_ref[...] = jnp.zeros_like(s_ref)
    lb = _hgrn_lower_bound(gam_ref, layer)

    def body(ci, carry):
        rows = pl.ds(pl.multiple_of(ci * CHUNK, CHUNK), CHUNK)
        qs = [_silu(_head_cols(hq_ref, rows, i, HGRN_DK)) * (HGRN_DK ** -0.5) for i in heads]
        kg = [_hgrn_gate(_head_cols(hf_ref, rows, i, HGRN_DK),
                         lb[:, i * HGRN_DK:(i + 1) * HGRN_DK]) for i in heads]
        vs = [_head_cols(hi_ref, rows, i, HGRN_DV) for i in heads]
        os = _chunk_scan(qs, [x[0] for x in kg], vs, [x[1] for x in kg],
                         [s_ref.at[i] for i in heads])
        for i in heads:
            gate = _silu(_head_cols(hg_ref, rows, i, HGRN_DV))
            o_ref[rows, i * HGRN_DV:(i + 1) * HGRN_DV] = (
                _rms(os[i], hn_ref[...]) * gate).astype(BF16)
        return carry

    lax.fori_loop(0, SEQ // CHUNK, body, 0, unroll=SCAN_UNROLL)


def _hgrn_prompt(proj, gamma, hn, merged, layer):
    p = SCAN_HEADS
    dk, dv = p * HGRN_DK, p * HGRN_DV
    cq, cf, ci, cg = (COL_HQ // dk, COL_HF // dk, COL_HI // dv, COL_HG // dv)
    co = GLA_WIDTH // dv
    kern = _skip_carries(functools.partial(_hgrn_prompt_kernel, layer=layer), 6, 1)
    return pl.pallas_call(
        kern,
        grid=(BATCH, HGRN_HEADS // p),
        in_specs=[pl.BlockSpec((SEQ, dk), lambda b, h: (b, cq + h)),
                  pl.BlockSpec((SEQ, dk), lambda b, h: (b, cf + h)),
                  pl.BlockSpec((SEQ, dv), lambda b, h: (b, ci + h)),
                  pl.BlockSpec((SEQ, dv), lambda b, h: (b, cg + h)),
                  pl.BlockSpec((DEPTH, dk), lambda b, h: (0, h)),
                  pl.BlockSpec((None, 1, HGRN_DV), lambda b, h: (layer, 0, 0)),
                  pl.BlockSpec(memory_space=pl.ANY)],
        out_specs=[pl.BlockSpec((SEQ, dv), lambda b, h: (b, co + h)),
                   pl.BlockSpec((None, p, HGRN_DK, HGRN_DV), lambda b, h: (b, h, 0, 0))],
        out_shape=[jax.ShapeDtypeStruct((N_ROWS, D_MODEL), BF16),
                   jax.ShapeDtypeStruct((BATCH, HGRN_HEADS, HGRN_DK, HGRN_DV), F32)],
        input_output_aliases={6: 0},
        compiler_params=_params(("arbitrary", "arbitrary")),
        name="hgrn_prompt",
    )(proj, proj, proj, proj, gamma, hn, merged)


def _sample_setup(q, k, v, g, oin_s, qh_s, kt_s, dt_s):
    cum = [g[0]]
    for t in range(1, DEC_SEQ):
        cum.append(cum[t - 1] + g[t])
    for t in range(DEC_SEQ):
        o = jnp.zeros_like(v[0])
        for s in range(t + 1):
            w = q[t] * k[s] if s == t else q[t] * (k[s] * jnp.exp(cum[t] - cum[s]))
            o = o + jnp.sum(w, axis=-1, keepdims=True) * v[s]
        oin_s[t] = o
    last = cum[DEC_SEQ - 1]
    for t in range(DEC_SEQ):
        qh_s[t] = q[t] * jnp.exp(cum[t])
        kt = jnp.transpose(k[t] * jnp.exp(last - cum[t]))
        for blk in range(NBLK):
            kt_s[blk, :, t * SAMPLE_NB:(t + 1) * SAMPLE_NB] = (
                kt[:, blk * SAMPLE_NB:(blk + 1) * SAMPLE_NB])
    dt = jnp.transpose(jnp.exp(last))
    for blk in range(NBLK):
        dt_s[blk] = dt[:, blk * SAMPLE_NB:(blk + 1) * SAMPLE_NB]


def _sample_step(bb, v_ref, sin_ref, sout_ref, qh_s, kt_s, dt_s, layer):
    if layer == 0:
        for later in range(1, DEPTH):
            sout_ref[later] = jnp.zeros(sout_ref.shape[1:], F32)
        sout_ref = sout_ref.at[0]
    nb = SAMPLE_NB
    nrow = DEC_SEQ * nb
    r0 = pl.multiple_of(bb * nb, nb)
    qh = jnp.concatenate([qh_s[t, pl.ds(r0, nb), :] for t in range(DEC_SEQ)], axis=0)
    vv = jnp.concatenate(
        [v_ref[pl.ds(pl.multiple_of(t * DEC_BATCH + r0, nb), nb), :] for t in range(DEC_SEQ)],
        axis=0)
    rj = jnp.bitwise_and(lax.broadcasted_iota(jnp.int32, (nrow, nb * LANES), 0), nb - 1)
    cb = lax.shift_right_logical(lax.broadcasted_iota(jnp.int32, (nrow, nb * LANES), 1), 7)
    q_bd = jnp.where(rj == cb, jnp.concatenate([qh] * nb, axis=1), 0.0).astype(BF16)
    s_old = sin_ref[...]
    s_stack = s_old.reshape(nb * LANES, s_old.shape[-1])
    o = jnp.dot(q_bd, s_stack.astype(BF16), preferred_element_type=F32)

    kt64 = kt_s[bb]
    lane_j = jnp.bitwise_and(lax.broadcasted_iota(jnp.int32, kt64.shape, 1), nb - 1)
    kt_bd = jnp.concatenate([jnp.where(lane_j == j, kt64, 0.0) for j in range(nb)],
                            axis=0).astype(BF16)
    upd = jnp.dot(kt_bd, vv.astype(BF16), preferred_element_type=F32)
    dcols = dt_s[bb]
    for j in range(nb):
        sout_ref[j] = s_old[j] * dcols[:, j:j + 1] + upd[j * LANES:(j + 1) * LANES, :]
    return o


def _sample_emit(bb, o, oin_s, gate_ref, norm_ref, o_ref):
    nb = SAMPLE_NB
    r0 = pl.multiple_of(bb * nb, nb)
    for t in range(DEC_SEQ):
        rows = pl.ds(pl.multiple_of(t * DEC_BATCH + r0, nb), nb)
        ot = o[t * nb:(t + 1) * nb, :] + oin_s[t, pl.ds(r0, nb), :]
        o_ref[rows, :] = (_rms(ot, norm_ref[...]) * _silu(gate_ref[rows, :])).astype(BF16)


def _tiles(ref):
    return [ref[t * DEC_BATCH:(t + 1) * DEC_BATCH, :] for t in range(DEC_SEQ)]


def _gla_sample_kernel(q_ref, k_ref, v_ref, r_ref, a_ref, wup_ref, bg_ref, gn_ref, sin_ref,
                       o_ref, sout_ref, oin_s, qh_s, kt_s, dt_s, *, layer):
    bb = pl.program_id(1)

    @pl.when(bb == 0)
    def _():
        q = [x * (GLA_DK ** -0.5) for x in _tiles(q_ref)]
        g = [_gla_gate(x, wup_ref, bg_ref) for x in _tiles(a_ref)]
        _sample_setup(q, _tiles(k_ref), _tiles(v_ref), g, oin_s, qh_s, kt_s, dt_s)

    o = _sample_step(bb, v_ref, sin_ref, sout_ref, qh_s, kt_s, dt_s, layer)
    _sample_emit(bb, o, oin_s, r_ref, gn_ref, o_ref)


def _hgrn_sample_kernel(hq_ref, hf_ref, hi_ref, hg_ref, gam_ref, hn_ref, sin_ref,
                        o_ref, sout_ref, oin_s, qh_s, kt_s, dt_s, *, layer):
    bb = pl.program_id(1)

    @pl.when(bb == 0)
    def _():
        lb = _hgrn_lower_bound(gam_ref, layer)
        q = [_silu(x) * (HGRN_DK ** -0.5) for x in _tiles(hq_ref)]
        kg = [_hgrn_gate(x, lb) for x in _tiles(hf_ref)]
        _sample_setup(q, [x[0] for x in kg], _tiles(hi_ref), [x[1] for x in kg],
                      oin_s, qh_s, kt_s, dt_s)

    o = _sample_step(bb, hi_ref, sin_ref, sout_ref, qh_s, kt_s, dt_s, layer)
    _sample_emit(bb, o, oin_s, hg_ref, hn_ref, o_ref)


def _sample_scratch(dv):
    return [pltpu.VMEM((DEC_SEQ, DEC_BATCH, dv), F32),
            pltpu.VMEM((DEC_SEQ, DEC_BATCH, LANES), F32),
            pltpu.VMEM((NBLK, LANES, DEC_SEQ * SAMPLE_NB), F32),
            pltpu.VMEM((NBLK, LANES, SAMPLE_NB), F32)]


def _sample_call(kern, n_in, in_specs, args, heads, dv, col_out, state_in, merged, state_prev,
                 layer, name):
    st_spec = pl.BlockSpec((None, SAMPLE_NB, None, LANES, dv), lambda h, bb: (layer, bb, h, 0, 0))
    st_out_spec = st_spec
    if state_prev is None:
        st_out_spec = pl.BlockSpec((DEPTH, SAMPLE_NB, None, LANES, dv),
                                   lambda h, bb: (0, bb, h, 0, 0))
    carries = [merged] if state_prev is None else [merged, state_prev]
    in_specs = in_specs + [st_spec] + [pl.BlockSpec(memory_space=pl.ANY)] * len(carries)
    args = args + [state_in] + carries
    aliases = {n_in + 1 + i: i for i in range(len(carries))}
    return pl.pallas_call(
        _skip_carries(kern, n_in + 1, len(carries)),
        grid=(heads, NBLK),
        in_specs=in_specs,
        out_specs=[pl.BlockSpec((N_SAMPLE, dv), lambda h, bb: (SAMPLE_ROW_BLOCK, col_out + h)),
                   st_out_spec],
        out_shape=[jax.ShapeDtypeStruct((N_ROWS, D_MODEL), BF16),
                   jax.ShapeDtypeStruct(state_in.shape, F32)],
        scratch_shapes=_sample_scratch(dv),
        input_output_aliases=aliases,
        compiler_params=_params(("arbitrary", "arbitrary")),
        name=name,
    )(*args)


def _sample_tile(width, col):
    return pl.BlockSpec((N_SAMPLE, width), lambda h, bb: (SAMPLE_ROW_BLOCK, col + h))


def _gla_sample(proj, wup, bg, gn, state_in, merged, state_prev, layer):
    in_specs = [_sample_tile(GLA_DK, COL_Q // LANES),
                _sample_tile(GLA_DK, COL_K // LANES),
                _sample_tile(GLA_DV, COL_V // GLA_DV),
                _sample_tile(GLA_DV, COL_R // GLA_DV),
                pl.BlockSpec((N_SAMPLE, LANES), lambda h, bb: (SAMPLE_ROW_BLOCK, COL_A // LANES)),
                pl.BlockSpec((None, LANES, GLA_DK), lambda h, bb: (layer, 0, h)),
                pl.BlockSpec((None, 1, GLA_DK), lambda h, bb: (layer, 0, h)),
                pl.BlockSpec((None, 1, GLA_DV), lambda h, bb: (layer, 0, 0))]
    args = [proj, proj, proj, proj, proj, wup, bg, gn]
    return _sample_call(functools.partial(_gla_sample_kernel, layer=layer), 8, in_specs, args,
                        GLA_HEADS, GLA_DV, 0, state_in, merged, state_prev, layer, "gla_sample")


def _hgrn_sample(proj, gamma, hn, state_in, merged, state_prev, layer):
    in_specs = [_sample_tile(HGRN_DK, COL_HQ // LANES),
                _sample_tile(HGRN_DK, COL_HF // LANES),
                _sample_tile(HGRN_DV, COL_HI // LANES),
                _sample_tile(HGRN_DV, COL_HG // LANES),
                pl.BlockSpec((DEPTH, HGRN_DK), lambda h, bb: (0, h)),
                pl.BlockSpec((None, 1, HGRN_DV), lambda h, bb: (layer, 0, 0))]
    args = [proj, proj, proj, proj, gamma, hn]
    return _sample_call(functools.partial(_hgrn_sample_kernel, layer=layer), 6, in_specs, args,
                        HGRN_HEADS, HGRN_DV, GLA_WIDTH // HGRN_DV, state_in, merged, state_prev,
                        layer, "hgrn_sample")


def _regroup_mix_w_in(w):
    pad = jnp.zeros(w.shape[:2] + (PROJ_W - w.shape[2],), w.dtype)
    return jnp.concatenate([w[..., :2048], w[..., 2064:], w[..., 2048:2064], pad], axis=-1)


def kernel(x_prompt, x_sample, state_gla, state_hgrn, norm_gains, ffn1_w_in, ffn1_w_out,
           ffn2_w_in, ffn2_w_out, mix_w_in, gla_w_gate_up, gla_b_gate, gla_norm, hgrn_gamma,
           hgrn_norm, mix_w_out):
    x = jnp.concatenate([x_prompt.reshape(N_PROMPT, D_MODEL),
                         jnp.transpose(x_sample, (1, 0, 2)).reshape(N_SAMPLE, D_MODEL)], axis=0)
    gains = norm_gains.reshape(DEPTH * 6, 1, D_MODEL)
    w_mix = _regroup_mix_w_in(mix_w_in)
    wup = jnp.pad(gla_w_gate_up, ((0, 0), (0, LANES - GLA_RANK), (0, 0)))
    bg = gla_b_gate.reshape(DEPTH, 1, GLA_HEADS * GLA_DK)
    gn = gla_norm.reshape(DEPTH, 1, GLA_DV)
    hn = hgrn_norm.reshape(DEPTH, 1, HGRN_DV)

    h = _rms_cast(x, gains, 0)
    gla_p, hgrn_p = [], []
    st_gla, st_hgrn = None, None
    for l in range(DEPTH):
        base = 6 * l
        act, wo = _ffn_in(h, ffn1_w_in, ffn1_w_out, l)
        x, h = _proj_out(act, wo, x, gains, base + 1, base + 2, 0.5)
        proj, merged, wo_mix = _mix_in(h, w_mix, mix_w_out, l)
        merged, sg = _gla_prompt(proj, wup, bg, gn, merged, l)
        merged, sh = _hgrn_prompt(proj, hgrn_gamma, hn, merged, l)
        merged, st_gla = _gla_sample(proj, wup, bg, gn, state_gla, merged, st_gla, l)
        merged, st_hgrn = _hgrn_sample(proj, hgrn_gamma, hn, state_hgrn, merged, st_hgrn, l)
        gla_p.append(sg)
        hgrn_p.append(sh)
        x, h = _proj_out(merged, wo_mix, x, gains, base + 3, base + 4, 1.0)
        act, wo = _ffn_in(h, ffn2_w_in, ffn2_w_out, l)
        x, h = _proj_out(act, wo, x, gains, base + 5, (base + 6) % (6 * DEPTH), 0.5)

    y_prompt = x[:N_PROMPT].reshape(BATCH, SEQ, D_MODEL)
    y_sample = jnp.transpose(x[N_PROMPT:].reshape(DEC_SEQ, DEC_BATCH, D_MODEL), (1, 0, 2))
    return (y_prompt, y_sample, jnp.stack(gla_p), jnp.stack(hgrn_p), st_gla, st_hgrn)
```

```python
import functools

import jax
import jax.numpy as jnp
from jax import lax
from jax.experimental import pallas as pl
from jax.experimental.pallas import tpu as pltpu

F32 = jnp.float32
BF16 = jnp.bfloat16

D_MODEL = 2048
BATCH = 4
SEQ = 2048
DEPTH = 2
DEC_BATCH = 128
DEC_SEQ = 4
GLA_HEADS = 4
GLA_DK = 128
GLA_DV = 256
GLA_WIDTH = GLA_HEADS * GLA_DV
GLA_RANK = 16
GLA_TAU = 16.0
HGRN_HEADS = 8
HGRN_DK = 128
HGRN_DV = 128
HGRN_WIDTH = HGRN_HEADS * HGRN_DV
D_FF = 5632
EPS = 1e-6

N_PROMPT = BATCH * SEQ
N_SAMPLE = DEC_BATCH * DEC_SEQ
N_ROWS = N_PROMPT + N_SAMPLE
SAMPLE_ROW_BLOCK = N_PROMPT // N_SAMPLE

LANES = 128
SUBLANES = 8
VMEM_LIMIT = 60 * 1024 * 1024

COL_Q = 0
COL_K = 512
COL_V = 1024
COL_R = 2048
COL_HQ = 3072
COL_HF = 4096
COL_HI = 5120
COL_HG = 6144
PROJ_W = 7168
GATE_COL = 2048
MIX_ALIGNED_TILES = GATE_COL // 1024

TM_FFN = 1088
TM_IN = 1088
TN = 512
MIX_TN = 1024
MIX_RC = 256
FFN_NC = 256
TM_OUT = 1088
TK_CHOICES = (1408, 1024, 512)
OUT_NC = 512
MIX_WO_ROWS = D_MODEL // (N_ROWS // TM_IN)
OUT_RC = 272
LAST_TILE = N_ROWS // TM_OUT - 1
TAIL_PROMPT_ROWS = N_PROMPT - LAST_TILE * TM_OUT
TM_ROW = 512
CHUNK = 128
SCAN_HEADS = 2
SCAN_UNROLL = 2
SAMPLE_NB = 16
NBLK = DEC_BATCH // SAMPLE_NB


def _params(sem):
    return pltpu.CompilerParams(dimension_semantics=sem, vmem_limit_bytes=VMEM_LIMIT)


def _sigmoid(x):
    return 1.0 / (1.0 + jnp.exp(-x))


def _silu(x):
    return x * _sigmoid(x)


def _rms(y, g):
    return y * lax.rsqrt(jnp.mean(y * y, axis=-1, keepdims=True) + EPS) * g


def _skip_carries(kern, n_in, n_carry):
    def wrapped(*refs):
        return kern(*refs[:n_in], *refs[n_in + n_carry:])
    return wrapped


def _rms_cast_kernel(xp_ref, xs_ref, g_ref, h_ref):
    prompt_tiles = N_PROMPT // TM_ROW

    @pl.when(pl.program_id(0) < prompt_tiles)
    def _():
        h_ref[...] = _rms(xp_ref[...], g_ref[...]).astype(BF16)

    @pl.when(pl.program_id(0) >= prompt_tiles)
    def _():
        h_ref[...] = _rms(xs_ref[...], g_ref[...]).astype(BF16)


def _rms_cast(x_prompt, x_sample, gains, gi):
    prompt_tiles = N_PROMPT // TM_ROW
    return pl.pallas_call(
        _rms_cast_kernel,
        grid=(N_ROWS // TM_ROW,),
        in_specs=[pl.BlockSpec((TM_ROW, D_MODEL), lambda m: (jnp.minimum(m, prompt_tiles - 1), 0)),
                  pl.BlockSpec((TM_ROW, D_MODEL), lambda m: (jnp.maximum(m - prompt_tiles, 0), 0)),
                  pl.BlockSpec((None, 1, D_MODEL), lambda m: (gi, 0, 0))],
        out_specs=pl.BlockSpec((TM_ROW, D_MODEL), lambda m: (m, 0)),
        out_shape=jax.ShapeDtypeStruct((N_ROWS, D_MODEL), BF16),
        compiler_params=_params(("arbitrary",)),
        name="rms_cast",
    )(x_prompt, x_sample, gains)


def _ffn_in_kernel(h_ref, wg_ref, wu_ref, wo_ref, o_ref, wob_ref, wg_s, wu_s):
    @pl.when(pl.program_id(1) == 0)
    def _():
        wg_s[...] = wg_ref[...].astype(BF16)
        wu_s[...] = wu_ref[...].astype(BF16)

    wob_ref[...] = wo_ref[...].astype(BF16)
    h = h_ref[...]
    for c in range(TN // FFN_NC):
        cols = slice(c * FFN_NC, (c + 1) * FFN_NC)
        g = jnp.dot(h, wg_s[:, cols], preferred_element_type=F32)
        u = jnp.dot(h, wu_s[:, cols], preferred_element_type=F32)
        o_ref[:, cols] = (g * _sigmoid(g) * u).astype(BF16)


def _ffn_in(h, w_in, w_out, layer):
    nt = D_FF // TN
    nm = N_ROWS // TM_FFN
    slab = D_FF // (nt * nm)
    return pl.pallas_call(
        _ffn_in_kernel,
        grid=(nt, nm),
        in_specs=[pl.BlockSpec((TM_FFN, D_MODEL), lambda n, m: (m, 0)),
                  pl.BlockSpec((None, D_MODEL, TN), lambda n, m: (layer, 0, n)),
                  pl.BlockSpec((None, D_MODEL, TN), lambda n, m: (layer, 0, n + nt)),
                  pl.BlockSpec((None, slab, D_MODEL), lambda n, m: (layer, n * nm + m, 0))],
        out_specs=[pl.BlockSpec((TM_FFN, TN), lambda n, m: (m, n)),
                   pl.BlockSpec((slab, D_MODEL), lambda n, m: (n * nm + m, 0))],
        out_shape=[jax.ShapeDtypeStruct((N_ROWS, D_FF), BF16),
                   jax.ShapeDtypeStruct((D_FF, D_MODEL), BF16)],
        scratch_shapes=[pltpu.VMEM((D_MODEL, TN), BF16), pltpu.VMEM((D_MODEL, TN), BF16)],
        compiler_params=_params(("arbitrary", "arbitrary")),
        name="ffn_in",
    )(h, w_in, w_in, w_out)


def _mix_in_kernel(h_ref, wa_ref, wb_ref, o_ref, a_ref, w_s, wg_s):
    n = pl.program_id(0)
    first_row_tile = pl.program_id(1) == 0
    gate_tile = n == MIX_ALIGNED_TILES

    @pl.when(jnp.logical_and(first_row_tile, n < MIX_ALIGNED_TILES))
    def _():
        w_s[...] = wa_ref[...].astype(BF16)

    @pl.when(jnp.logical_and(first_row_tile, n >= MIX_ALIGNED_TILES))
    def _():
        def copy_rows(i, carry):
            rows = pl.ds(pl.multiple_of(i * MIX_RC, MIX_RC), MIX_RC)
            window = jnp.concatenate([wa_ref[rows, :], wb_ref[rows, :]], axis=1)
            w_s[rows, :] = window[:, GLA_RANK:GLA_RANK + MIX_TN].astype(BF16)
            return carry

        lax.fori_loop(0, D_MODEL // MIX_RC, copy_rows, 0)

    @pl.when(jnp.logical_and(first_row_tile, gate_tile))
    def _():
        lane = lax.broadcasted_iota(jnp.int32, (D_MODEL, LANES), 1)
        wg_s[...] = jnp.where(lane < GLA_RANK, wa_ref[:, :LANES], 0.0).astype(BF16)

    h = h_ref[...]
    o_ref[...] = jnp.dot(h, w_s[...], preferred_element_type=F32)

    @pl.when(gate_tile)
    def _():
        a_ref[...] = jnp.dot(h, wg_s[...], preferred_element_type=F32)


def _mix_in(h, w_in, layer):
    last = N_ROWS // TM_IN - 1
    lanes_per_tile = MIX_TN // LANES

    def gate_block(n, m):
        return jnp.where(n == MIX_ALIGNED_TILES, m, jnp.where(n < MIX_ALIGNED_TILES, 0, last))

    return pl.pallas_call(
        _mix_in_kernel,
        grid=(PROJ_W // MIX_TN, N_ROWS // TM_IN),
        in_specs=[pl.BlockSpec((TM_IN, D_MODEL), lambda n, m: (m, 0)),
                  pl.BlockSpec((None, D_MODEL, MIX_TN), lambda n, m: (layer, 0, n)),
                  pl.BlockSpec((None, D_MODEL, LANES),
                               lambda n, m: (layer, 0, (n + 1) * lanes_per_tile))],
        out_specs=[pl.BlockSpec((TM_IN, MIX_TN), lambda n, m: (m, n)),
                   pl.BlockSpec((TM_IN, LANES), lambda n, m: (gate_block(n, m), 0))],
        out_shape=[jax.ShapeDtypeStruct((N_ROWS, PROJ_W), F32),
                   jax.ShapeDtypeStruct((N_ROWS, LANES), F32)],
        scratch_shapes=[pltpu.VMEM((D_MODEL, MIX_TN), BF16), pltpu.VMEM((D_MODEL, LANES), BF16)],
        compiler_params=_params(("arbitrary", "arbitrary")),
        name="mix_in",
    )(h, w_in, w_in)


def _mix_aux_kernel(wo_ref, z_ref, wob_ref):
    z_ref[...] = jnp.zeros_like(z_ref)
    wob_ref[...] = wo_ref[...].astype(BF16)


def _mix_aux(w_out, layer):
    return pl.pallas_call(
        _mix_aux_kernel,
        grid=(N_ROWS // TM_IN,),
        in_specs=[pl.BlockSpec((None, MIX_WO_ROWS, D_MODEL), lambda m: (layer, m, 0))],
        out_specs=[pl.BlockSpec((TM_IN, D_MODEL), lambda m: (m, 0)),
                   pl.BlockSpec((MIX_WO_ROWS, D_MODEL), lambda m: (m, 0))],
        out_shape=[jax.ShapeDtypeStruct((N_ROWS, D_MODEL), BF16),
                   jax.ShapeDtypeStruct((D_MODEL, D_MODEL), BF16)],
        compiler_params=_params(("arbitrary",)),
        name="mix_aux",
    )(w_out)


def _proj_out_kernel(*refs, alpha, nk, split_x, final):
    refs = list(refs)
    a_ref, w_ref, x_hbm = refs[:3]
    xs_hbm = refs[3] if split_x else None
    gp_ref = refs[3 + split_x]
    gn_ref = None if final else refs[4 + split_x]
    xo_ref, o2_ref, xbuf, sem = refs[-4:]
    m = pl.program_id(0)
    k = pl.program_id(1)

    def tile_copy():
        rows = pl.ds(pl.multiple_of(m * TM_OUT, TM_OUT), TM_OUT)
        return pltpu.make_async_copy(x_hbm.at[rows, :], xbuf, sem.at[0])

    def tail_copies():
        head = pltpu.make_async_copy(x_hbm.at[pl.ds(LAST_TILE * TM_OUT, TAIL_PROMPT_ROWS), :],
                                     xbuf.at[pl.ds(0, TAIL_PROMPT_ROWS), :], sem.at[0])
        tail = pltpu.make_async_copy(xs_hbm, xbuf.at[pl.ds(TAIL_PROMPT_ROWS, N_SAMPLE), :],
                                     sem.at[1])
        return head, tail

    def residual_copies(act):
        if not split_x:
            act(tile_copy())
            return

        @pl.when(m < LAST_TILE)
        def _():
            act(tile_copy())

        @pl.when(m == LAST_TILE)
        def _():
            for c in tail_copies():
                act(c)

    @pl.when(k == 0)
    def _():
        residual_copies(lambda c: c.start())
        xo_ref[...] = jnp.zeros_like(xo_ref)

    a = a_ref[...]
    for c in range(D_MODEL // OUT_NC):
        cols = slice(c * OUT_NC, (c + 1) * OUT_NC)
        xo_ref[:, cols] += jnp.dot(a, w_ref[:, cols], preferred_element_type=F32)

    @pl.when(k == nk - 1)
    def _():
        residual_copies(lambda c: c.wait())

        def rows_body(i, carry):
            rows = pl.ds(pl.multiple_of(i * OUT_RC, OUT_RC), OUT_RC)
            xn = xbuf[rows, :] + alpha * _rms(xo_ref[rows, :], gp_ref[...])
            xo_ref[rows, :] = xn
            if not final:
                o2_ref[rows, :] = _rms(xn, gn_ref[...]).astype(BF16)
            return carry

        lax.fori_loop(0, TM_OUT // OUT_RC, rows_body, 0)
        if final:
            @pl.when(m == LAST_TILE)
            def _():
                o2_ref[...] = xo_ref[pl.ds(TAIL_PROMPT_ROWS, N_SAMPLE), :]


def _proj_out(a, w, x, gains, gi_post, gi_next, alpha, x_sample=None):
    kdim = a.shape[1]
    tk = next(t for t in TK_CHOICES if kdim % t == 0)
    nk = kdim // tk
    split_x = x_sample is not None
    final = gi_next is None
    kern = functools.partial(_proj_out_kernel, alpha=alpha, nk=nk, split_x=split_x, final=final)
    gain = lambda gi: pl.BlockSpec((None, 1, D_MODEL), lambda m, k: (gi, 0, 0))
    in_specs = [pl.BlockSpec((TM_OUT, tk), lambda m, k: (m, k)),
                pl.BlockSpec((tk, D_MODEL), lambda m, k: (k, 0)),
                pl.BlockSpec(memory_space=pl.ANY)]
    args = [a, w, x]
    if split_x:
        in_specs.append(pl.BlockSpec(memory_space=pl.ANY))
        args.append(x_sample)
    in_specs.append(gain(gi_post))
    args.append(gains)
    row_tile = pl.BlockSpec((TM_OUT, D_MODEL), lambda m, k: (m, 0))
    if final:
        out_specs = [row_tile, pl.BlockSpec((N_SAMPLE, D_MODEL), lambda m, k: (0, 0))]
        out_shape = [jax.ShapeDtypeStruct((N_PROMPT, D_MODEL), F32),
                     jax.ShapeDtypeStruct((N_SAMPLE, D_MODEL), F32)]
    else:
        in_specs.append(gain(gi_next))
        args.append(gains)
        out_specs = [row_tile, row_tile]
        out_shape = [jax.ShapeDtypeStruct((N_ROWS, D_MODEL), F32),
                     jax.ShapeDtypeStruct((N_ROWS, D_MODEL), BF16)]
    return pl.pallas_call(
        kern,
        grid=(N_ROWS // TM_OUT, nk),
        in_specs=in_specs,
        out_specs=out_specs,
        out_shape=out_shape,
        scratch_shapes=[pltpu.VMEM((TM_OUT, D_MODEL), F32), pltpu.SemaphoreType.DMA((2,))],
        compiler_params=_params(("arbitrary", "arbitrary")),
        name="proj_out",
    )(*args)


def _log_sigmoid(x):
    return jnp.minimum(x, 0.0) - jnp.log(1.0 + jnp.exp(-jnp.abs(x)))


def _gla_gate(a, wup_ref, bg_ref):
    pre = jnp.dot(a.astype(BF16), wup_ref[...].astype(BF16), preferred_element_type=F32)
    return _log_sigmoid(pre + bg_ref[...]) * (1.0 / GLA_TAU)


def _hgrn_lower_bound(gam_ref, layer):
    gam = gam_ref[...]
    mx = jnp.max(gam, axis=0, keepdims=True)
    e = jnp.exp(gam - mx)
    probs = e / jnp.sum(e, axis=0, keepdims=True)
    acc = probs[0:1, :]
    for j in range(1, layer + 1):
        acc = acc + probs[j:j + 1, :]
    return acc - probs[0:1, :]


def _hgrn_gate(z, lb):
    e = jnp.exp(-jnp.abs(z))
    r = 1.0 / (1.0 + e)
    er = e * r
    pos = z >= 0.0
    sig = jnp.where(pos, r, er)
    sig_neg = jnp.where(pos, er, r)
    return (1.0 - lb) * sig_neg, jnp.log(lb + (1.0 - lb) * sig)


def _chunk_levels(c):
    levels, m = [], 2
    while m <= c:
        levels.append(m)
        m *= 2
    return levels


def _midpoint_distance(cum, g, m):
    c = cum.shape[0]
    hm = m // 2
    if m == 2:
        odd = jnp.bitwise_and(lax.broadcasted_iota(jnp.int32, cum.shape, 0), 1) == 1
        return jnp.where(odd, jnp.abs(g), 0.0)
    if m < 2 * SUBLANES:
        cum3 = cum.reshape(c // SUBLANES, SUBLANES, LANES)
        sub = lax.broadcasted_iota(jnp.int32, cum3.shape, 1)
        mid = None
        for b0 in range(0, SUBLANES, m):
            cand = jnp.broadcast_to(cum3[:, b0 + hm - 1:b0 + hm, :], cum3.shape)
            mid = cand if mid is None else jnp.where(sub >= b0, cand, mid)
        return jnp.abs(cum3 - mid).reshape(c, LANES)
    mid = cum.reshape(c // m, m, LANES)[:, hm - 1:hm, :]
    mid = jnp.broadcast_to(mid, (c // m, m, LANES)).reshape(c, LANES)
    return jnp.abs(cum - mid)


def _chunk_scan(qs, ks, vs, gs, s_refs):
    heads = range(len(qs))
    c = qs[0].shape[0]
    nt = (((1,), (1,)), ((), ()))
    row = lax.broadcasted_iota(jnp.int32, (c, c), 0)
    col = lax.broadcasted_iota(jnp.int32, (c, c), 1)
    eye = row == col
    tri = (col <= row).astype(BF16)
    g_hi = [g.astype(BF16) for g in gs]
    g_lo = [(gs[i] - g_hi[i].astype(F32)).astype(BF16) for i in heads]
    cums = [jnp.dot(tri, g_hi[i], preferred_element_type=F32)
            + jnp.dot(tri, g_lo[i], preferred_element_type=F32) for i in heads]

    s_old = [s_refs[i][...] for i in heads]
    vb = [v.astype(BF16) for v in vs]
    o = [jnp.dot((qs[i] * jnp.exp(cums[i])).astype(BF16), s_old[i].astype(BF16),
                 preferred_element_type=F32) for i in heads]

    for i in heads:
        last = cums[i][c - 1:c, :]
        kt_t = jnp.transpose(ks[i] * jnp.exp(last - cums[i])).astype(BF16)
        dec = jnp.broadcast_to(jnp.exp(last), (LANES, LANES))
        dec_col = jnp.sum(jnp.where(eye, dec, 0.0), axis=1, keepdims=True)
        s_refs[i][...] = s_old[i] * dec_col + jnp.dot(kt_t, vb[i], preferred_element_type=F32)

    rowl = lax.broadcasted_iota(jnp.int32, (c, LANES), 0)
    a = [jnp.where(eye, lax.dot_general(qs[i].astype(BF16), ks[i].astype(BF16), nt,
                                        preferred_element_type=F32), 0.0) for i in heads]
    for m in _chunk_levels(c):
        hm = m // 2
        sh = m.bit_length() - 1
        second = jnp.bitwise_and(rowl, m - 1) >= hm
        mask = ((lax.shift_right_logical(row, sh) == lax.shift_right_logical(col, sh))
                & (jnp.bitwise_and(row, m - 1) >= hm) & (jnp.bitwise_and(col, m - 1) < hm))
        x = [(jnp.where(second, qs[i], ks[i])
              * jnp.exp(-_midpoint_distance(cums[i], gs[i], m))).astype(BF16) for i in heads]
        am = [lax.dot_general(x[i], x[i], nt, preferred_element_type=F32) for i in heads]
        a = [jnp.where(mask, am[i], a[i]) for i in heads]
    return [o[i] + jnp.dot(a[i].astype(BF16), vb[i], preferred_element_type=F32) for i in heads]


def _head_cols(ref, rows, i, width):
    return ref[rows, i * width:(i + 1) * width]


def _gla_prompt_kernel(q_ref, k_ref, v_ref, r_ref, a_ref, wup_ref, bg_ref, gn_ref, o_ref, s_ref):
    heads = range(SCAN_HEADS)
    s_ref[...] = jnp.zeros_like(s_ref)

    def body(ci, carry):
        rows = pl.ds(pl.multiple_of(ci * CHUNK, CHUNK), CHUNK)
        a = a_ref[rows, :].astype(BF16)
        pre = jnp.dot(a, wup_ref[...].astype(BF16), preferred_element_type=F32) + bg_ref[...]
        gs = [_log_sigmoid(pre[:, i * GLA_DK:(i + 1) * GLA_DK]) * (1.0 / GLA_TAU) for i in heads]
        qs = [_head_cols(q_ref, rows, i, GLA_DK) * (GLA_DK ** -0.5) for i in heads]
        ks = [_head_cols(k_ref, rows, i, GLA_DK) for i in heads]
        vs = [_head_cols(v_ref, rows, i, GLA_DV) for i in heads]
        os = _chunk_scan(qs, ks, vs, gs, [s_ref.at[i] for i in heads])
        for i in heads:
            gate = _silu(_head_cols(r_ref, rows, i, GLA_DV))
            o_ref[rows, i * GLA_DV:(i + 1) * GLA_DV] = (_rms(os[i], gn_ref[...]) * gate).astype(BF16)
        return carry

    lax.fori_loop(0, SEQ // CHUNK, body, 0, unroll=SCAN_UNROLL)


def _gla_prompt(proj, alr, wup, bg, gn, merged, layer):
    p = SCAN_HEADS
    dk, dv = p * GLA_DK, p * GLA_DV
    cq, ck, cv, cr = (COL_Q // dk, COL_K // dk, COL_V // dv, COL_R // dv)
    return pl.pallas_call(
        _skip_carries(_gla_prompt_kernel, 8, 1),
        grid=(BATCH, GLA_HEADS // p),
        in_specs=[pl.BlockSpec((SEQ, dk), lambda b, h: (b, cq + h)),
                  pl.BlockSpec((SEQ, dk), lambda b, h: (b, ck + h)),
                  pl.BlockSpec((SEQ, dv), lambda b, h: (b, cv + h)),
                  pl.BlockSpec((SEQ, dv), lambda b, h: (b, cr + h)),
                  pl.BlockSpec((SEQ, LANES), lambda b, h: (b, 0)),
                  pl.BlockSpec((None, LANES, dk), lambda b, h: (layer, 0, h)),
                  pl.BlockSpec((None, 1, dk), lambda b, h: (layer, 0, h)),
                  pl.BlockSpec((None, 1, GLA_DV), lambda b, h: (layer, 0, 0)),
                  pl.BlockSpec(memory_space=pl.ANY)],
        out_specs=[pl.BlockSpec((SEQ, dv), lambda b, h: (b, h)),
                   pl.BlockSpec((None, p, GLA_DK, GLA_DV), lambda b, h: (b, h, 0, 0))],
        out_shape=[jax.ShapeDtypeStruct((N_ROWS, D_MODEL), BF16),
                   jax.ShapeDtypeStruct((BATCH, GLA_HEADS, GLA_DK, GLA_DV), F32)],
        input_output_aliases={8: 0},
        compiler_params=_params(("arbitrary", "arbitrary")),
        name="gla_prompt",
    )(proj, proj, proj, proj, alr, wup, bg, gn, merged)


def _hgrn_prompt_kernel(hq_ref, hf_ref, hi_ref, hg_ref, gam_ref, hn_ref, o_ref, s_ref, *, layer):
    heads = range(SCAN_HEADS)
    s_ref[...] = jnp.zeros_like(s_ref)
    lb = _hgrn_lower_bound(gam_ref, layer)

    def body(ci, carry):
        rows = pl.ds(pl.multiple_of(ci * CHUNK, CHUNK), CHUNK)
        qs = [_silu(_head_cols(hq_ref, rows, i, HGRN_DK)) * (HGRN_DK ** -0.5) for i in heads]
        kg = [_hgrn_gate(_head_cols(hf_ref, rows, i, HGRN_DK),
                         lb[:, i * HGRN_DK:(i + 1) * HGRN_DK]) for i in heads]
        vs = [_head_cols(hi_ref, rows, i, HGRN_DV) for i in heads]
        os = _chunk_scan(qs, [x[0] for x in kg], vs, [x[1] for x in kg],
                         [s_ref.at[i] for i in heads])
        for i in heads:
            gate = _silu(_head_cols(hg_ref, rows, i, HGRN_DV))
            o_ref[rows, i * HGRN_DV:(i + 1) * HGRN_DV] = (
                _rms(os[i], hn_ref[...]) * gate).astype(BF16)
        return carry

    lax.fori_loop(0, SEQ // CHUNK, body, 0, unroll=SCAN_UNROLL)


def _hgrn_prompt(proj, gamma, hn, merged, layer):
    p = SCAN_HEADS
    dk, dv = p * HGRN_DK, p * HGRN_DV
    cq, cf, ci, cg = (COL_HQ // dk, COL_HF // dk, COL_HI // dv, COL_HG // dv)
    co = GLA_WIDTH // dv
    kern = _skip_carries(functools.partial(_hgrn_prompt_kernel, layer=layer), 6, 1)
    return pl.pallas_call(
        kern,
        grid=(BATCH, HGRN_HEADS // p),
        in_specs=[pl.BlockSpec((SEQ, dk), lambda b, h: (b, cq + h)),
                  pl.BlockSpec((SEQ, dk), lambda b, h: (b, cf + h)),
                  pl.BlockSpec((SEQ, dv), lambda b, h: (b, ci + h)),
                  pl.BlockSpec((SEQ, dv), lambda b, h: (b, cg + h)),
                  pl.BlockSpec((DEPTH, dk), lambda b, h: (0, h)),
                  pl.BlockSpec((None, 1, HGRN_DV), lambda b, h: (layer, 0, 0)),
                  pl.BlockSpec(memory_space=pl.ANY)],
        out_specs=[pl.BlockSpec((SEQ, dv), lambda b, h: (b, co + h)),
                   pl.BlockSpec((None, p, HGRN_DK, HGRN_DV), lambda b, h: (b, h, 0, 0))],
        out_shape=[jax.ShapeDtypeStruct((N_ROWS, D_MODEL), BF16),
                   jax.ShapeDtypeStruct((BATCH, HGRN_HEADS, HGRN_DK, HGRN_DV), F32)],
        input_output_aliases={6: 0},
        compiler_params=_params(("arbitrary", "arbitrary")),
        name="hgrn_prompt",
    )(proj, proj, proj, proj, gamma, hn, merged)


def _sample_setup(q, k, v, g, oin_s, qh_s, kt_s, dt_s):
    cum = [g[0]]
    for t in range(1, DEC_SEQ):
        cum.append(cum[t - 1] + g[t])
    for t in range(DEC_SEQ):
        o = jnp.zeros_like(v[0])
        for s in range(t + 1):
            w = q[t] * k[s] if s == t else q[t] * (k[s] * jnp.exp(cum[t] - cum[s]))
            o = o + jnp.sum(w, axis=-1, keepdims=True) * v[s]
        oin_s[t] = o
    last = cum[DEC_SEQ - 1]
    for t in range(DEC_SEQ):
        qh_s[t] = q[t] * jnp.exp(cum[t])
        kt = jnp.transpose(k[t] * jnp.exp(last - cum[t]))
        for blk in range(NBLK):
            kt_s[blk, :, t * SAMPLE_NB:(t + 1) * SAMPLE_NB] = (
                kt[:, blk * SAMPLE_NB:(blk + 1) * SAMPLE_NB])
    dt = jnp.transpose(jnp.exp(last))
    for blk in range(NBLK):
        dt_s[blk] = dt[:, blk * SAMPLE_NB:(blk + 1) * SAMPLE_NB]


def _sample_step(bb, v_ref, sin_ref, sout_ref, qh_s, kt_s, dt_s, layer):
    if layer == 0:
        for later in range(1, DEPTH):
            sout_ref[later] = jnp.zeros(sout_ref.shape[1:], F32)
        sout_ref = sout_ref.at[0]
    nb = SAMPLE_NB
    nrow = DEC_SEQ * nb
    r0 = pl.multiple_of(bb * nb, nb)
    qh = jnp.concatenate([qh_s[t, pl.ds(r0, nb), :] for t in range(DEC_SEQ)], axis=0)
    vv = jnp.concatenate(
        [v_ref[pl.ds(pl.multiple_of(t * DEC_BATCH + r0, nb), nb), :] for t in range(DEC_SEQ)],
        axis=0)
    rj = jnp.bitwise_and(lax.broadcasted_iota(jnp.int32, (nrow, nb * LANES), 0), nb - 1)
    cb = lax.shift_right_logical(lax.broadcasted_iota(jnp.int32, (nrow, nb * LANES), 1), 7)
    q_bd = jnp.where(rj == cb, jnp.concatenate([qh] * nb, axis=1), 0.0).astype(BF16)
    s_old = sin_ref[...]
    s_stack = s_old.reshape(nb * LANES, s_old.shape[-1])
    o = jnp.dot(q_bd, s_stack.astype(BF16), preferred_element_type=F32)

    kt64 = kt_s[bb]
    lane_j = jnp.bitwise_and(lax.broadcasted_iota(jnp.int32, kt64.shape, 1), nb - 1)
    kt_bd = jnp.concatenate([jnp.where(lane_j == j, kt64, 0.0) for j in range(nb)],
                            axis=0).astype(BF16)
    upd = jnp.dot(kt_bd, vv.astype(BF16), preferred_element_type=F32)
    dcols = dt_s[bb]
    for j in range(nb):
        sout_ref[j] = s_old[j] * dcols[:, j:j + 1] + upd[j * LANES:(j + 1) * LANES, :]
    return o


def _sample_emit(bb, o, oin_s, gate_ref, norm_ref, o_ref):
    nb = SAMPLE_NB
    r0 = pl.multiple_of(bb * nb, nb)
    for t in range(DEC_SEQ):
        rows = pl.ds(pl.multiple_of(t * DEC_BATCH + r0, nb), nb)
        ot = o[t * nb:(t + 1) * nb, :] + oin_s[t, pl.ds(r0, nb), :]
        o_ref[rows, :] = (_rms(ot, norm_ref[...]) * _silu(gate_ref[rows, :])).astype(BF16)


def _tiles(ref):
    return [ref[t * DEC_BATCH:(t + 1) * DEC_BATCH, :] for t in range(DEC_SEQ)]


def _gla_sample_kernel(q_ref, k_ref, v_ref, r_ref, a_ref, wup_ref, bg_ref, gn_ref, sin_ref,
                       o_ref, sout_ref, oin_s, qh_s, kt_s, dt_s, *, layer):
    bb = pl.program_id(1)

    @pl.when(bb == 0)
    def _():
        q = [x * (GLA_DK ** -0.5) for x in _tiles(q_ref)]
        g = [_gla_gate(x, wup_ref, bg_ref) for x in _tiles(a_ref)]
        _sample_setup(q, _tiles(k_ref), _tiles(v_ref), g, oin_s, qh_s, kt_s, dt_s)

    o = _sample_step(bb, v_ref, sin_ref, sout_ref, qh_s, kt_s, dt_s, layer)
    _sample_emit(bb, o, oin_s, r_ref, gn_ref, o_ref)


def _hgrn_sample_kernel(hq_ref, hf_ref, hi_ref, hg_ref, gam_ref, hn_ref, sin_ref,
                        o_ref, sout_ref, oin_s, qh_s, kt_s, dt_s, *, layer):
    bb = pl.program_id(1)

    @pl.when(bb == 0)
    def _():
        lb = _hgrn_lower_bound(gam_ref, layer)
        q = [_silu(x) * (HGRN_DK ** -0.5) for x in _tiles(hq_ref)]
        kg = [_hgrn_gate(x, lb) for x in _tiles(hf_ref)]
        _sample_setup(q, [x[0] for x in kg], _tiles(hi_ref), [x[1] for x in kg],
                      oin_s, qh_s, kt_s, dt_s)

    o = _sample_step(bb, hi_ref, sin_ref, sout_ref, qh_s, kt_s, dt_s, layer)
    _sample_emit(bb, o, oin_s, hg_ref, hn_ref, o_ref)


def _sample_scratch(dv):
    return [pltpu.VMEM((DEC_SEQ, DEC_BATCH, dv), F32),
            pltpu.VMEM((DEC_SEQ, DEC_BATCH, LANES), F32),
            pltpu.VMEM((NBLK, LANES, DEC_SEQ * SAMPLE_NB), F32),
            pltpu.VMEM((NBLK, LANES, SAMPLE_NB), F32)]


def _sample_call(kern, n_in, in_specs, args, heads, dv, col_out, state_in, merged, state_prev,
                 layer, name):
    st_spec = pl.BlockSpec((None, SAMPLE_NB, None, LANES, dv), lambda h, bb: (layer, bb, h, 0, 0))
    st_out_spec = st_spec
    if state_prev is None:
        st_out_spec = pl.BlockSpec((DEPTH, SAMPLE_NB, None, LANES, dv),
                                   lambda h, bb: (0, bb, h, 0, 0))
    carries = [merged] if state_prev is None else [merged, state_prev]
    in_specs = in_specs + [st_spec] + [pl.BlockSpec(memory_space=pl.ANY)] * len(carries)
    args = args + [state_in] + carries
    aliases = {n_in + 1 + i: i for i in range(len(carries))}
    return pl.pallas_call(
        _skip_carries(kern, n_in + 1, len(carries)),
        grid=(heads, NBLK),
        in_specs=in_specs,
        out_specs=[pl.BlockSpec((N_SAMPLE, dv), lambda h, bb: (SAMPLE_ROW_BLOCK, col_out + h)),
                   st_out_spec],
        out_shape=[jax.ShapeDtypeStruct((N_ROWS, D_MODEL), BF16),
                   jax.ShapeDtypeStruct(state_in.shape, F32)],
        scratch_shapes=_sample_scratch(dv),
        input_output_aliases=aliases,
        compiler_params=_params(("arbitrary", "arbitrary")),
        name=name,
    )(*args)


def _sample_tile(width, col):
    return pl.BlockSpec((N_SAMPLE, width), lambda h, bb: (SAMPLE_ROW_BLOCK, col + h))


def _gla_sample(proj, alr, wup, bg, gn, state_in, merged, state_prev, layer):
    in_specs = [_sample_tile(GLA_DK, COL_Q // LANES),
                _sample_tile(GLA_DK, COL_K // LANES),
                _sample_tile(GLA_DV, COL_V // GLA_DV),
                _sample_tile(GLA_DV, COL_R // GLA_DV),
                pl.BlockSpec((N_SAMPLE, LANES), lambda h, bb: (SAMPLE_ROW_BLOCK, 0)),
                pl.BlockSpec((None, LANES, GLA_DK), lambda h, bb: (layer, 0, h)),
                pl.BlockSpec((None, 1, GLA_DK), lambda h, bb: (layer, 0, h)),
                pl.BlockSpec((None, 1, GLA_DV), lambda h, bb: (layer, 0, 0))]
    args = [proj, proj, proj, proj, alr, wup, bg, gn]
    return _sample_call(functools.partial(_gla_sample_kernel, layer=layer), 8, in_specs, args,
                        GLA_HEADS, GLA_DV, 0, state_in, merged, state_prev, layer, "gla_sample")


def _hgrn_sample(proj, gamma, hn, state_in, merged, state_prev, layer):
    in_specs = [_sample_tile(HGRN_DK, COL_HQ // LANES),
                _sample_tile(HGRN_DK, COL_HF // LANES),
                _sample_tile(HGRN_DV, COL_HI // LANES),
                _sample_tile(HGRN_DV, COL_HG // LANES),
                pl.BlockSpec((DEPTH, HGRN_DK), lambda h, bb: (0, h)),
                pl.BlockSpec((None, 1, HGRN_DV), lambda h, bb: (layer, 0, 0))]
    args = [proj, proj, proj, proj, gamma, hn]
    return _sample_call(functools.partial(_hgrn_sample_kernel, layer=layer), 6, in_specs, args,
                        HGRN_HEADS, HGRN_DV, GLA_WIDTH // HGRN_DV, state_in, merged, state_prev,
                        layer, "hgrn_sample")


def kernel(x_prompt, x_sample, state_gla, state_hgrn, norm_gains, ffn1_w_in, ffn1_w_out,
           ffn2_w_in, ffn2_w_out, mix_w_in, gla_w_gate_up, gla_b_gate, gla_norm, hgrn_gamma,
           hgrn_norm, mix_w_out):
    xp = x_prompt.reshape(N_PROMPT, D_MODEL)
    xs = jnp.transpose(x_sample, (1, 0, 2)).reshape(N_SAMPLE, D_MODEL)
    gains = norm_gains.reshape(DEPTH * 6, 1, D_MODEL)
    wup = jnp.pad(gla_w_gate_up, ((0, 0), (0, LANES - GLA_RANK), (0, 0)))
    bg = gla_b_gate.reshape(DEPTH, 1, GLA_HEADS * GLA_DK)
    gn = gla_norm.reshape(DEPTH, 1, GLA_DV)
    hn = hgrn_norm.reshape(DEPTH, 1, HGRN_DV)

    h = _rms_cast(xp, xs, gains, 0)
    x = None
    gla_p, hgrn_p = [], []
    st_gla, st_hgrn = None, None
    for l in range(DEPTH):
        base = 6 * l
        act, wo = _ffn_in(h, ffn1_w_in, ffn1_w_out, l)
        if l == 0:
            x, h = _proj_out(act, wo, xp, gains, base + 1, base + 2, 0.5, x_sample=xs)
        else:
            x, h = _proj_out(act, wo, x, gains, base + 1, base + 2, 0.5)
        proj, alr = _mix_in(h, mix_w_in, l)
        merged, wo_mix = _mix_aux(mix_w_out, l)
        merged, sg = _gla_prompt(proj, alr, wup, bg, gn, merged, l)
        merged, sh = _hgrn_prompt(proj, hgrn_gamma, hn, merged, l)
        merged, st_gla = _gla_sample(proj, alr, wup, bg, gn, state_gla, merged, st_gla, l)
        merged, st_hgrn = _hgrn_sample(proj, hgrn_gamma, hn, state_hgrn, merged, st_hgrn, l)
        gla_p.append(sg)
        hgrn_p.append(sh)
        x, h = _proj_out(merged, wo_mix, x, gains, base + 3, base + 4, 1.0)
        act, wo = _ffn_in(h, ffn2_w_in, ffn2_w_out, l)
        if l + 1 < DEPTH:
            x, h = _proj_out(act, wo, x, gains, base + 5, base + 6, 0.5)
        else:
            y_prompt_rows, y_sample_rows = _proj_out(act, wo, x, gains, base + 5, None, 0.5)

    y_prompt = y_prompt_rows.reshape(BATCH, SEQ, D_MODEL)
    y_sample = jnp.transpose(y_sample_rows.reshape(DEC_SEQ, DEC_BATCH, D_MODEL), (1, 0, 2))
    return (y_prompt, y_sample, jnp.stack(gla_p), jnp.stack(hgrn_p), st_gla, st_hgrn)
```

```python
import functools

import jax
import jax.numpy as jnp
from jax import lax
from jax.experimental import pallas as pl
from jax.experimental.pallas import tpu as pltpu

F32 = jnp.float32
BF16 = jnp.bfloat16

D_MODEL = 2048
BATCH = 4
SEQ = 2048
DEPTH = 2
DEC_BATCH = 128
DEC_SEQ = 4
GLA_HEADS = 4
GLA_DK = 128
GLA_DV = 256
GLA_WIDTH = GLA_HEADS * GLA_DV
GLA_RANK = 16
GLA_TAU = 16.0
HGRN_HEADS = 8
HGRN_DK = 128
HGRN_DV = 128
HGRN_WIDTH = HGRN_HEADS * HGRN_DV
D_FF = 5632
EPS = 1e-6

N_PROMPT = BATCH * SEQ
N_SAMPLE = DEC_BATCH * DEC_SEQ
N_ROWS = N_PROMPT + N_SAMPLE
SAMPLE_ROW_BLOCK = N_PROMPT // N_SAMPLE

LANES = 128
SUBLANES = 8
VMEM_LIMIT = 60 * 1024 * 1024

COL_Q = 0
COL_K = 512
COL_V = 1024
COL_R = 2048
COL_HQ = 3072
COL_HF = 4096
COL_HI = 5120
COL_HG = 6144
PROJ_W = 7168
GATE_COL = 2048
MIX_ALIGNED_TILES = GATE_COL // 1024

TM_FFN = 1088
TM_IN = 1088
TN = 512
MIX_TN = 1024
FFN_NC = 256
TM_OUT = 1088
TK_CHOICES = (1408, 1024, 512)
OUT_NC = 512
MIX_WO_ROWS = D_MODEL // (N_ROWS // TM_IN)
OUT_RC = 272
LAST_TILE = N_ROWS // TM_OUT - 1
TAIL_PROMPT_ROWS = N_PROMPT - LAST_TILE * TM_OUT
TM_ROW = 512
CHUNK = 128
SCAN_HEADS = 2
SCAN_UNROLL = 2
SAMPLE_NB = 16
NBLK = DEC_BATCH // SAMPLE_NB


def _params(sem):
    return pltpu.CompilerParams(dimension_semantics=sem, vmem_limit_bytes=VMEM_LIMIT)


def _sigmoid(x):
    return 1.0 / (1.0 + jnp.exp(-x))


def _silu(x):
    return x * _sigmoid(x)


def _rms(y, g):
    return y * lax.rsqrt(jnp.mean(y * y, axis=-1, keepdims=True) + EPS) * g


def _skip_carries(kern, n_in, n_carry):
    def wrapped(*refs):
        return kern(*refs[:n_in], *refs[n_in + n_carry:])
    return wrapped


def _rms_cast_kernel(xp_ref, xs_ref, g_ref, h_ref):
    prompt_tiles = N_PROMPT // TM_ROW

    @pl.when(pl.program_id(0) < prompt_tiles)
    def _():
        h_ref[...] = _rms(xp_ref[...], g_ref[...]).astype(BF16)

    @pl.when(pl.program_id(0) >= prompt_tiles)
    def _():
        h_ref[...] = _rms(xs_ref[...], g_ref[...]).astype(BF16)


def _rms_cast(x_prompt, x_sample, gains, gi):
    prompt_tiles = N_PROMPT // TM_ROW
    return pl.pallas_call(
        _rms_cast_kernel,
        grid=(N_ROWS // TM_ROW,),
        in_specs=[pl.BlockSpec((TM_ROW, D_MODEL), lambda m: (jnp.minimum(m, prompt_tiles - 1), 0)),
                  pl.BlockSpec((TM_ROW, D_MODEL), lambda m: (jnp.maximum(m - prompt_tiles, 0), 0)),
                  pl.BlockSpec((None, 1, D_MODEL), lambda m: (gi, 0, 0))],
        out_specs=pl.BlockSpec((TM_ROW, D_MODEL), lambda m: (m, 0)),
        out_shape=jax.ShapeDtypeStruct((N_ROWS, D_MODEL), BF16),
        compiler_params=_params(("arbitrary",)),
        name="rms_cast",
    )(x_prompt, x_sample, gains)


def _ffn_in_kernel(h_ref, wg_ref, wu_ref, wo_ref, o_ref, wob_ref, wg_s, wu_s):
    @pl.when(pl.program_id(1) == 0)
    def _():
        wg_s[...] = wg_ref[...].astype(BF16)
        wu_s[...] = wu_ref[...].astype(BF16)

    wob_ref[...] = wo_ref[...].astype(BF16)
    h = h_ref[...]
    for c in range(TN // FFN_NC):
        cols = slice(c * FFN_NC, (c + 1) * FFN_NC)
        g = jnp.dot(h, wg_s[:, cols], preferred_element_type=F32)
        u = jnp.dot(h, wu_s[:, cols], preferred_element_type=F32)
        o_ref[:, cols] = (g * _sigmoid(g) * u).astype(BF16)


def _ffn_in(h, w_in, w_out, layer):
    nt = D_FF // TN
    nm = N_ROWS // TM_FFN
    slab = D_FF // (nt * nm)
    return pl.pallas_call(
        _ffn_in_kernel,
        grid=(nt, nm),
        in_specs=[pl.BlockSpec((TM_FFN, D_MODEL), lambda n, m: (m, 0)),
                  pl.BlockSpec((None, D_MODEL, TN), lambda n, m: (layer, 0, n)),
                  pl.BlockSpec((None, D_MODEL, TN), lambda n, m: (layer, 0, n + nt)),
                  pl.BlockSpec((None, slab, D_MODEL), lambda n, m: (layer, n * nm + m, 0))],
        out_specs=[pl.BlockSpec((TM_FFN, TN), lambda n, m: (m, n)),
                   pl.BlockSpec((slab, D_MODEL), lambda n, m: (n * nm + m, 0))],
        out_shape=[jax.ShapeDtypeStruct((N_ROWS, D_FF), BF16),
                   jax.ShapeDtypeStruct((D_FF, D_MODEL), BF16)],
        scratch_shapes=[pltpu.VMEM((D_MODEL, TN), BF16), pltpu.VMEM((D_MODEL, TN), BF16)],
        compiler_params=_params(("arbitrary", "arbitrary")),
        name="ffn_in",
    )(h, w_in, w_in, w_out)


def _mix_in_kernel(h_ref, wa_ref, wb_ref, o_ref, a_ref, w_s, wg_s):
    nt = (((1,), (1,)), ((), ()))
    n = pl.program_id(0)
    first_row_tile = pl.program_id(1) == 0
    gate_tile = n == MIX_ALIGNED_TILES
    kept = MIX_TN - GLA_RANK

    @pl.when(jnp.logical_and(first_row_tile, n < MIX_ALIGNED_TILES))
    def _():
        w_s[...] = wa_ref[...].astype(BF16)

    @pl.when(jnp.logical_and(first_row_tile, n >= MIX_ALIGNED_TILES))
    def _():
        w_s[0:kept, :] = wa_ref[GLA_RANK:MIX_TN, :].astype(BF16)
        w_s[kept:MIX_TN, :] = wb_ref[0:GLA_RANK, :].astype(BF16)

    @pl.when(jnp.logical_and(first_row_tile, gate_tile))
    def _():
        wg_s[...] = jnp.zeros_like(wg_s)
        wg_s[0:GLA_RANK, :] = wa_ref[0:GLA_RANK, :].astype(BF16)

    h = h_ref[...]
    o_ref[...] = lax.dot_general(h, w_s[...], nt, preferred_element_type=F32)

    @pl.when(gate_tile)
    def _():
        a_ref[...] = lax.dot_general(h, wg_s[...], nt, preferred_element_type=F32)


def _mix_in(h, w_in_t, layer):
    last = N_ROWS // TM_IN - 1
    slabs_per_tile = MIX_TN // LANES

    def gate_block(n, m):
        return jnp.where(n == MIX_ALIGNED_TILES, m, jnp.where(n < MIX_ALIGNED_TILES, 0, last))

    return pl.pallas_call(
        _mix_in_kernel,
        grid=(PROJ_W // MIX_TN, N_ROWS // TM_IN),
        in_specs=[pl.BlockSpec((TM_IN, D_MODEL), lambda n, m: (m, 0)),
                  pl.BlockSpec((None, MIX_TN, D_MODEL), lambda n, m: (layer, n, 0)),
                  pl.BlockSpec((None, LANES, D_MODEL),
                               lambda n, m: (layer, (n + 1) * slabs_per_tile, 0))],
        out_specs=[pl.BlockSpec((TM_IN, MIX_TN), lambda n, m: (m, n)),
                   pl.BlockSpec((TM_IN, LANES), lambda n, m: (gate_block(n, m), 0))],
        out_shape=[jax.ShapeDtypeStruct((N_ROWS, PROJ_W), F32),
                   jax.ShapeDtypeStruct((N_ROWS, LANES), F32)],
        scratch_shapes=[pltpu.VMEM((MIX_TN, D_MODEL), BF16), pltpu.VMEM((LANES, D_MODEL), BF16)],
        compiler_params=_params(("arbitrary", "arbitrary")),
        name="mix_in",
    )(h, w_in_t, w_in_t)


def _mix_aux_kernel(wo_ref, z_ref, wob_ref):
    z_ref[...] = jnp.zeros_like(z_ref)
    wob_ref[...] = wo_ref[...].astype(BF16)


def _mix_aux(w_out, layer):
    return pl.pallas_call(
        _mix_aux_kernel,
        grid=(N_ROWS // TM_IN,),
        in_specs=[pl.BlockSpec((None, MIX_WO_ROWS, D_MODEL), lambda m: (layer, m, 0))],
        out_specs=[pl.BlockSpec((TM_IN, D_MODEL), lambda m: (m, 0)),
                   pl.BlockSpec((MIX_WO_ROWS, D_MODEL), lambda m: (m, 0))],
        out_shape=[jax.ShapeDtypeStruct((N_ROWS, D_MODEL), BF16),
                   jax.ShapeDtypeStruct((D_MODEL, D_MODEL), BF16)],
        compiler_params=_params(("arbitrary",)),
        name="mix_aux",
    )(w_out)


def _proj_out_kernel(*refs, alpha, nk, split_x, final):
    refs = list(refs)
    a_ref, w_ref, x_hbm = refs[:3]
    xs_hbm = refs[3] if split_x else None
    gp_ref = refs[3 + split_x]
    gn_ref = None if final else refs[4 + split_x]
    xo_ref, o2_ref, xbuf, sem = refs[-4:]
    m = pl.program_id(0)
    k = pl.program_id(1)

    def tile_copy():
        rows = pl.ds(pl.multiple_of(m * TM_OUT, TM_OUT), TM_OUT)
        return pltpu.make_async_copy(x_hbm.at[rows, :], xbuf, sem.at[0])

    def tail_copies():
        head = pltpu.make_async_copy(x_hbm.at[pl.ds(LAST_TILE * TM_OUT, TAIL_PROMPT_ROWS), :],
                                     xbuf.at[pl.ds(0, TAIL_PROMPT_ROWS), :], sem.at[0])
        tail = pltpu.make_async_copy(xs_hbm, xbuf.at[pl.ds(TAIL_PROMPT_ROWS, N_SAMPLE), :],
                                     sem.at[1])
        return head, tail

    def residual_copies(act):
        if not split_x:
            act(tile_copy())
            return

        @pl.when(m < LAST_TILE)
        def _():
            act(tile_copy())

        @pl.when(m == LAST_TILE)
        def _():
            for c in tail_copies():
                act(c)

    @pl.when(k == 0)
    def _():
        residual_copies(lambda c: c.start())
        xo_ref[...] = jnp.zeros_like(xo_ref)

    a = a_ref[...]
    for c in range(D_MODEL // OUT_NC):
        cols = slice(c * OUT_NC, (c + 1) * OUT_NC)
        xo_ref[:, cols] += jnp.dot(a, w_ref[:, cols], preferred_element_type=F32)

    @pl.when(k == nk - 1)
    def _():
        residual_copies(lambda c: c.wait())

        def rows_body(i, carry):
            rows = pl.ds(pl.multiple_of(i * OUT_RC, OUT_RC), OUT_RC)
            xn = xbuf[rows, :] + alpha * _rms(xo_ref[rows, :], gp_ref[...])
            xo_ref[rows, :] = xn
            if not final:
                o2_ref[rows, :] = _rms(xn, gn_ref[...]).astype(BF16)
            return carry

        lax.fori_loop(0, TM_OUT // OUT_RC, rows_body, 0)
        if final:
            @pl.when(m == LAST_TILE)
            def _():
                o2_ref[...] = xo_ref[pl.ds(TAIL_PROMPT_ROWS, N_SAMPLE), :]


def _proj_out(a, w, x, gains, gi_post, gi_next, alpha, x_sample=None):
    kdim = a.shape[1]
    tk = next(t for t in TK_CHOICES if kdim % t == 0)
    nk = kdim // tk
    split_x = x_sample is not None
    final = gi_next is None
    kern = functools.partial(_proj_out_kernel, alpha=alpha, nk=nk, split_x=split_x, final=final)
    gain = lambda gi: pl.BlockSpec((None, 1, D_MODEL), lambda m, k: (gi, 0, 0))
    in_specs = [pl.BlockSpec((TM_OUT, tk), lambda m, k: (m, k)),
                pl.BlockSpec((tk, D_MODEL), lambda m, k: (k, 0)),
                pl.BlockSpec(memory_space=pl.ANY)]
    args = [a, w, x]
    if split_x:
        in_specs.append(pl.BlockSpec(memory_space=pl.ANY))
        args.append(x_sample)
    in_specs.append(gain(gi_post))
    args.append(gains)
    row_tile = pl.BlockSpec((TM_OUT, D_MODEL), lambda m, k: (m, 0))
    if final:
        out_specs = [row_tile, pl.BlockSpec((N_SAMPLE, D_MODEL), lambda m, k: (0, 0))]
        out_shape = [jax.ShapeDtypeStruct((N_PROMPT, D_MODEL), F32),
                     jax.ShapeDtypeStruct((N_SAMPLE, D_MODEL), F32)]
    else:
        in_specs.append(gain(gi_next))
        args.append(gains)
        out_specs = [row_tile, row_tile]
        out_shape = [jax.ShapeDtypeStruct((N_ROWS, D_MODEL), F32),
                     jax.ShapeDtypeStruct((N_ROWS, D_MODEL), BF16)]
    return pl.pallas_call(
        kern,
        grid=(N_ROWS // TM_OUT, nk),
        in_specs=in_specs,
        out_specs=out_specs,
        out_shape=out_shape,
        scratch_shapes=[pltpu.VMEM((TM_OUT, D_MODEL), F32), pltpu.SemaphoreType.DMA((2,))],
        compiler_params=_params(("arbitrary", "arbitrary")),
        name="proj_out",
    )(*args)


def _log_sigmoid(x):
    return jnp.minimum(x, 0.0) - jnp.log(1.0 + jnp.exp(-jnp.abs(x)))


def _gla_gate(a, wup_ref, bg_ref):
    pre = jnp.dot(a.astype(BF16), wup_ref[...].astype(BF16), preferred_element_type=F32)
    return _log_sigmoid(pre + bg_ref[...]) * (1.0 / GLA_TAU)


def _hgrn_lower_bound(gam_ref, layer):
    gam = gam_ref[...]
    mx = jnp.max(gam, axis=0, keepdims=True)
    e = jnp.exp(gam - mx)
    probs = e / jnp.sum(e, axis=0, keepdims=True)
    acc = probs[0:1, :]
    for j in range(1, layer + 1):
        acc = acc + probs[j:j + 1, :]
    return acc - probs[0:1, :]


def _hgrn_gate(z, lb):
    e = jnp.exp(-jnp.abs(z))
    r = 1.0 / (1.0 + e)
    er = e * r
    pos = z >= 0.0
    sig = jnp.where(pos, r, er)
    sig_neg = jnp.where(pos, er, r)
    return (1.0 - lb) * sig_neg, jnp.log(lb + (1.0 - lb) * sig)


def _chunk_levels(c):
    levels, m = [], 2
    while m <= c:
        levels.append(m)
        m *= 2
    return levels


def _midpoint_distance(cum, g, m):
    c = cum.shape[0]
    hm = m // 2
    if m == 2:
        odd = jnp.bitwise_and(lax.broadcasted_iota(jnp.int32, cum.shape, 0), 1) == 1
        return jnp.where(odd, jnp.abs(g), 0.0)
    if m < 2 * SUBLANES:
        cum3 = cum.reshape(c // SUBLANES, SUBLANES, LANES)
        sub = lax.broadcasted_iota(jnp.int32, cum3.shape, 1)
        mid = None
        for b0 in range(0, SUBLANES, m):
            cand = jnp.broadcast_to(cum3[:, b0 + hm - 1:b0 + hm, :], cum3.shape)
            mid = cand if mid is None else jnp.where(sub >= b0, cand, mid)
        return jnp.abs(cum3 - mid).reshape(c, LANES)
    mid = cum.reshape(c // m, m, LANES)[:, hm - 1:hm, :]
    mid = jnp.broadcast_to(mid, (c // m, m, LANES)).reshape(c, LANES)
    return jnp.abs(cum - mid)


def _pair_levels(c):
    row = lax.broadcasted_iota(jnp.int32, (c, c), 0)
    col = lax.broadcasted_iota(jnp.int32, (c, c), 1)
    diff = jnp.bitwise_xor(row, col)
    lvl = jnp.zeros((c, c), jnp.int32)
    for j in range(c.bit_length() - 1):
        lvl = lvl + (diff >= (1 << j)).astype(jnp.int32)
    return jnp.where(col > row, -1, lvl)


def _chunk_scan(qs, ks, vs, gs, s_refs, lvl):
    heads = range(len(qs))
    c = qs[0].shape[0]
    nt = (((1,), (1,)), ((), ()))
    eye = lvl == 0
    tri = (lvl >= 0).astype(BF16)
    g_hi = [g.astype(BF16) for g in gs]
    g_lo = [(gs[i] - g_hi[i].astype(F32)).astype(BF16) for i in heads]
    cums = [jnp.dot(tri, g_hi[i], preferred_element_type=F32)
            + jnp.dot(tri, g_lo[i], preferred_element_type=F32) for i in heads]

    s_old = [s_refs[i][...] for i in heads]
    vb = [v.astype(BF16) for v in vs]
    o = [jnp.dot((qs[i] * jnp.exp(cums[i])).astype(BF16), s_old[i].astype(BF16),
                 preferred_element_type=F32) for i in heads]

    for i in heads:
        last = cums[i][c - 1:c, :]
        kt_t = jnp.transpose(ks[i] * jnp.exp(last - cums[i])).astype(BF16)
        dec = jnp.broadcast_to(jnp.exp(last), (LANES, LANES))
        dec_col = jnp.sum(jnp.where(eye, dec, 0.0), axis=1, keepdims=True)
        s_refs[i][...] = s_old[i] * dec_col + jnp.dot(kt_t, vb[i], preferred_element_type=F32)

    rowl = lax.broadcasted_iota(jnp.int32, (c, LANES), 0)
    a = [jnp.where(eye, lax.dot_general(qs[i].astype(BF16), ks[i].astype(BF16), nt,
                                        preferred_element_type=F32), 0.0) for i in heads]
    for m in _chunk_levels(c):
        second = jnp.bitwise_and(rowl, m // 2) != 0
        mask = lvl == m.bit_length() - 1
        x = [(jnp.where(second, qs[i], ks[i])
              * jnp.exp(-_midpoint_distance(cums[i], gs[i], m))).astype(BF16) for i in heads]
        am = [lax.dot_general(x[i], x[i], nt, preferred_element_type=F32) for i in heads]
        a = [jnp.where(mask, am[i], a[i]) for i in heads]
    return o, [x.astype(BF16) for x in a], vb


def _head_cols(ref, rows, i, width):
    return ref[rows, i * width:(i + 1) * width]


def _skewed_chunk_loop(front, finish, pend_refs):
    oi_s, pa_s, vb_s = pend_refs
    heads = range(SCAN_HEADS)

    def stash(parts):
        ois, pas, vbs = parts
        for i in heads:
            oi_s[i] = ois[i]
            pa_s[i] = pas[i]
            vb_s[i] = vbs[i]

    def issue():
        return [oi_s[i] + jnp.dot(pa_s[i], vb_s[i], preferred_element_type=F32) for i in heads]

    stash(front(0))

    def body(ci, carry):
        os = issue()
        stash(front(ci))
        finish(ci - 1, os)
        return carry

    n = SEQ // CHUNK
    lax.fori_loop(1, n, body, 0, unroll=SCAN_UNROLL)
    finish(n - 1, issue())


def _chunk_rows(ci):
    return pl.ds(pl.multiple_of(ci * CHUNK, CHUNK), CHUNK)


def _gla_prompt_kernel(q_ref, k_ref, v_ref, r_ref, a_ref, wup_ref, bg_ref, gn_ref, o_ref, s_ref,
                       oi_s, pa_s, vb_s):
    heads = range(SCAN_HEADS)
    s_ref[...] = jnp.zeros_like(s_ref)
    lvl = _pair_levels(CHUNK)

    def front(ci):
        rows = _chunk_rows(ci)
        a = a_ref[rows, :].astype(BF16)
        pre = jnp.dot(a, wup_ref[...].astype(BF16), preferred_element_type=F32) + bg_ref[...]
        gs = [_log_sigmoid(pre[:, i * GLA_DK:(i + 1) * GLA_DK]) * (1.0 / GLA_TAU) for i in heads]
        qs = [_head_cols(q_ref, rows, i, GLA_DK) * (GLA_DK ** -0.5) for i in heads]
        ks = [_head_cols(k_ref, rows, i, GLA_DK) for i in heads]
        vs = [_head_cols(v_ref, rows, i, GLA_DV) for i in heads]
        return _chunk_scan(qs, ks, vs, gs, [s_ref.at[i] for i in heads], lvl)

    def finish(ci, os):
        rows = _chunk_rows(ci)
        for i in heads:
            gate = _silu(_head_cols(r_ref, rows, i, GLA_DV))
            o_ref[rows, i * GLA_DV:(i + 1) * GLA_DV] = (_rms(os[i], gn_ref[...]) * gate).astype(BF16)

    _skewed_chunk_loop(front, finish, (oi_s, pa_s, vb_s))


def _pending_scratch(dv):
    return [pltpu.VMEM((SCAN_HEADS, CHUNK, dv), F32),
            pltpu.VMEM((SCAN_HEADS, CHUNK, CHUNK), BF16),
            pltpu.VMEM((SCAN_HEADS, CHUNK, dv), BF16)]


def _gla_prompt(proj, alr, wup, bg, gn, merged, layer):
    p = SCAN_HEADS
    dk, dv = p * GLA_DK, p * GLA_DV
    cq, ck, cv, cr = (COL_Q // dk, COL_K // dk, COL_V // dv, COL_R // dv)
    return pl.pallas_call(
        _skip_carries(_gla_prompt_kernel, 8, 1),
        grid=(BATCH, GLA_HEADS // p),
        in_specs=[pl.BlockSpec((SEQ, dk), lambda b, h: (b, cq + h)),
                  pl.BlockSpec((SEQ, dk), lambda b, h: (b, ck + h)),
                  pl.BlockSpec((SEQ, dv), lambda b, h: (b, cv + h)),
                  pl.BlockSpec((SEQ, dv), lambda b, h: (b, cr + h)),
                  pl.BlockSpec((SEQ, LANES), lambda b, h: (b, 0)),
                  pl.BlockSpec((None, LANES, dk), lambda b, h: (layer, 0, h)),
                  pl.BlockSpec((None, 1, dk), lambda b, h: (layer, 0, h)),
                  pl.BlockSpec((None, 1, GLA_DV), lambda b, h: (layer, 0, 0)),
                  pl.BlockSpec(memory_space=pl.ANY)],
        out_specs=[pl.BlockSpec((SEQ, dv), lambda b, h: (b, h)),
                   pl.BlockSpec((None, p, GLA_DK, GLA_DV), lambda b, h: (b, h, 0, 0))],
        out_shape=[jax.ShapeDtypeStruct((N_ROWS, D_MODEL), BF16),
                   jax.ShapeDtypeStruct((BATCH, GLA_HEADS, GLA_DK, GLA_DV), F32)],
        scratch_shapes=_pending_scratch(GLA_DV),
        input_output_aliases={8: 0},
        compiler_params=_params(("arbitrary", "arbitrary")),
        name="gla_prompt",
    )(proj, proj, proj, proj, alr, wup, bg, gn, merged)


def _hgrn_prompt_kernel(hq_ref, hf_ref, hi_ref, hg_ref, gam_ref, hn_ref, o_ref, s_ref,
                        oi_s, pa_s, vb_s, *, layer):
    heads = range(SCAN_HEADS)
    s_ref[...] = jnp.zeros_like(s_ref)
    lvl = _pair_levels(CHUNK)
    lb = _hgrn_lower_bound(gam_ref, layer)

    def front(ci):
        rows = _chunk_rows(ci)
        qs = [_silu(_head_cols(hq_ref, rows, i, HGRN_DK)) * (HGRN_DK ** -0.5) for i in heads]
        kg = [_hgrn_gate(_head_cols(hf_ref, rows, i, HGRN_DK),
                         lb[:, i * HGRN_DK:(i + 1) * HGRN_DK]) for i in heads]
        vs = [_head_cols(hi_ref, rows, i, HGRN_DV) for i in heads]
        return _chunk_scan(qs, [x[0] for x in kg], vs, [x[1] for x in kg],
                           [s_ref.at[i] for i in heads], lvl)

    def finish(ci, os):
        rows = _chunk_rows(ci)
        for i in heads:
            gate = _silu(_head_cols(hg_ref, rows, i, HGRN_DV))
            o_ref[rows, i * HGRN_DV:(i + 1) * HGRN_DV] = (
                _rms(os[i], hn_ref[...]) * gate).astype(BF16)

    _skewed_chunk_loop(front, finish, (oi_s, pa_s, vb_s))


def _hgrn_prompt(proj, gamma, hn, merged, layer):
    p = SCAN_HEADS
    dk, dv = p * HGRN_DK, p * HGRN_DV
    cq, cf, ci, cg = (COL_HQ // dk, COL_HF // dk, COL_HI // dv, COL_HG // dv)
    co = GLA_WIDTH // dv
    kern = _skip_carries(functools.partial(_hgrn_prompt_kernel, layer=layer), 6, 1)
    return pl.pallas_call(
        kern,
        grid=(BATCH, HGRN_HEADS // p),
        in_specs=[pl.BlockSpec((SEQ, dk), lambda b, h: (b, cq + h)),
                  pl.BlockSpec((SEQ, dk), lambda b, h: (b, cf + h)),
                  pl.BlockSpec((SEQ, dv), lambda b, h: (b, ci + h)),
                  pl.BlockSpec((SEQ, dv), lambda b, h: (b, cg + h)),
                  pl.BlockSpec((DEPTH, dk), lambda b, h: (0, h)),
                  pl.BlockSpec((None, 1, HGRN_DV), lambda b, h: (layer, 0, 0)),
                  pl.BlockSpec(memory_space=pl.ANY)],
        out_specs=[pl.BlockSpec((SEQ, dv), lambda b, h: (b, co + h)),
                   pl.BlockSpec((None, p, HGRN_DK, HGRN_DV), lambda b, h: (b, h, 0, 0))],
        out_shape=[jax.ShapeDtypeStruct((N_ROWS, D_MODEL), BF16),
                   jax.ShapeDtypeStruct((BATCH, HGRN_HEADS, HGRN_DK, HGRN_DV), F32)],
        scratch_shapes=_pending_scratch(HGRN_DV),
        input_output_aliases={6: 0},
        compiler_params=_params(("arbitrary", "arbitrary")),
        name="hgrn_prompt",
    )(proj, proj, proj, proj, gamma, hn, merged)


def _sample_setup(q, k, v, g, oin_s, qh_s, kt_s, dt_s):
    cum = [g[0]]
    for t in range(1, DEC_SEQ):
        cum.append(cum[t - 1] + g[t])
    for t in range(DEC_SEQ):
        o = jnp.zeros_like(v[0])
        for s in range(t + 1):
            w = q[t] * k[s] if s == t else q[t] * (k[s] * jnp.exp(cum[t] - cum[s]))
            o = o + jnp.sum(w, axis=-1, keepdims=True) * v[s]
        oin_s[t] = o
    last = cum[DEC_SEQ - 1]
    for t in range(DEC_SEQ):
        qh_s[t] = q[t] * jnp.exp(cum[t])
        kt = jnp.transpose(k[t] * jnp.exp(last - cum[t]))
        for blk in range(NBLK):
            kt_s[blk, :, t * SAMPLE_NB:(t + 1) * SAMPLE_NB] = (
                kt[:, blk * SAMPLE_NB:(blk + 1) * SAMPLE_NB])
    dt = jnp.transpose(jnp.exp(last))
    for blk in range(NBLK):
        dt_s[blk] = dt[:, blk * SAMPLE_NB:(blk + 1) * SAMPLE_NB]


def _sample_step(bb, v_ref, sin_ref, sout_ref, qh_s, kt_s, dt_s, layer):
    if layer == 0:
        for later in range(1, DEPTH):
            sout_ref[later] = jnp.zeros(sout_ref.shape[1:], F32)
        sout_ref = sout_ref.at[0]
    nb = SAMPLE_NB
    nrow = DEC_SEQ * nb
    r0 = pl.multiple_of(bb * nb, nb)
    qh = jnp.concatenate([qh_s[t, pl.ds(r0, nb), :] for t in range(DEC_SEQ)], axis=0)
    vv = jnp.concatenate(
        [v_ref[pl.ds(pl.multiple_of(t * DEC_BATCH + r0, nb), nb), :] for t in range(DEC_SEQ)],
        axis=0)
    rj = jnp.bitwise_and(lax.broadcasted_iota(jnp.int32, (nrow, nb * LANES), 0), nb - 1)
    cb = lax.shift_right_logical(lax.broadcasted_iota(jnp.int32, (nrow, nb * LANES), 1), 7)
    q_bd = jnp.where(rj == cb, jnp.concatenate([qh] * nb, axis=1), 0.0).astype(BF16)
    s_old = sin_ref[...]
    s_stack = s_old.reshape(nb * LANES, s_old.shape[-1])
    o = jnp.dot(q_bd, s_stack.astype(BF16), preferred_element_type=F32)

    kt64 = kt_s[bb]
    lane_j = jnp.bitwise_and(lax.broadcasted_iota(jnp.int32, kt64.shape, 1), nb - 1)
    kt_bd = jnp.concatenate([jnp.where(lane_j == j, kt64, 0.0) for j in range(nb)],
                            axis=0).astype(BF16)
    upd = jnp.dot(kt_bd, vv.astype(BF16), preferred_element_type=F32)
    dcols = dt_s[bb]
    for j in range(nb):
        sout_ref[j] = s_old[j] * dcols[:, j:j + 1] + upd[j * LANES:(j + 1) * LANES, :]
    return o


def _sample_emit(bb, o, oin_s, gate_ref, norm_ref, o_ref):
    nb = SAMPLE_NB
    r0 = pl.multiple_of(bb * nb, nb)
    for t in range(DEC_SEQ):
        rows = pl.ds(pl.multiple_of(t * DEC_BATCH + r0, nb), nb)
        ot = o[t * nb:(t + 1) * nb, :] + oin_s[t, pl.ds(r0, nb), :]
        o_ref[rows, :] = (_rms(ot, norm_ref[...]) * _silu(gate_ref[rows, :])).astype(BF16)


def _tiles(ref):
    return [ref[t * DEC_BATCH:(t + 1) * DEC_BATCH, :] for t in range(DEC_SEQ)]


def _gla_sample_kernel(q_ref, k_ref, v_ref, r_ref, a_ref, wup_ref, bg_ref, gn_ref, sin_ref,
                       o_ref, sout_ref, oin_s, qh_s, kt_s, dt_s, *, layer):
    bb = pl.program_id(1)

    @pl.when(bb == 0)
    def _():
        q = [x * (GLA_DK ** -0.5) for x in _tiles(q_ref)]
        g = [_gla_gate(x, wup_ref, bg_ref) for x in _tiles(a_ref)]
        _sample_setup(q, _tiles(k_ref), _tiles(v_ref), g, oin_s, qh_s, kt_s, dt_s)

    o = _sample_step(bb, v_ref, sin_ref, sout_ref, qh_s, kt_s, dt_s, layer)
    _sample_emit(bb, o, oin_s, r_ref, gn_ref, o_ref)


def _hgrn_sample_kernel(hq_ref, hf_ref, hi_ref, hg_ref, gam_ref, hn_ref, sin_ref,
                        o_ref, sout_ref, oin_s, qh_s, kt_s, dt_s, *, layer):
    bb = pl.program_id(1)

    @pl.when(bb == 0)
    def _():
        lb = _hgrn_lower_bound(gam_ref, layer)
        q = [_silu(x) * (HGRN_DK ** -0.5) for x in _tiles(hq_ref)]
        kg = [_hgrn_gate(x, lb) for x in _tiles(hf_ref)]
        _sample_setup(q, [x[0] for x in kg], _tiles(hi_ref), [x[1] for x in kg],
                      oin_s, qh_s, kt_s, dt_s)

    o = _sample_step(bb, hi_ref, sin_ref, sout_ref, qh_s, kt_s, dt_s, layer)
    _sample_emit(bb, o, oin_s, hg_ref, hn_ref, o_ref)


def _sample_scratch(dv):
    return [pltpu.VMEM((DEC_SEQ, DEC_BATCH, dv), F32),
            pltpu.VMEM((DEC_SEQ, DEC_BATCH, LANES), F32),
            pltpu.VMEM((NBLK, LANES, DEC_SEQ * SAMPLE_NB), F32),
            pltpu.VMEM((NBLK, LANES, SAMPLE_NB), F32)]


def _sample_call(kern, n_in, in_specs, args, heads, dv, col_out, state_in, merged, state_prev,
                 layer, name):
    st_spec = pl.BlockSpec((None, SAMPLE_NB, None, LANES, dv), lambda h, bb: (layer, bb, h, 0, 0))
    st_out_spec = st_spec
    if state_prev is None:
        st_out_spec = pl.BlockSpec((DEPTH, SAMPLE_NB, None, LANES, dv),
                                   lambda h, bb: (0, bb, h, 0, 0))
    carries = [merged] if state_prev is None else [merged, state_prev]
    in_specs = in_specs + [st_spec] + [pl.BlockSpec(memory_space=pl.ANY)] * len(carries)
    args = args + [state_in] + carries
    aliases = {n_in + 1 + i: i for i in range(len(carries))}
    return pl.pallas_call(
        _skip_carries(kern, n_in + 1, len(carries)),
        grid=(heads, NBLK),
        in_specs=in_specs,
        out_specs=[pl.BlockSpec((N_SAMPLE, dv), lambda h, bb: (SAMPLE_ROW_BLOCK, col_out + h)),
                   st_out_spec],
        out_shape=[jax.ShapeDtypeStruct((N_ROWS, D_MODEL), BF16),
                   jax.ShapeDtypeStruct(state_in.shape, F32)],
        scratch_shapes=_sample_scratch(dv),
        input_output_aliases=aliases,
        compiler_params=_params(("arbitrary", "arbitrary")),
        name=name,
    )(*args)


def _sample_tile(width, col):
    return pl.BlockSpec((N_SAMPLE, width), lambda h, bb: (SAMPLE_ROW_BLOCK, col + h))


def _gla_sample(proj, alr, wup, bg, gn, state_in, merged, state_prev, layer):
    in_specs = [_sample_tile(GLA_DK, COL_Q // LANES),
                _sample_tile(GLA_DK, COL_K // LANES),
                _sample_tile(GLA_DV, COL_V // GLA_DV),
                _sample_tile(GLA_DV, COL_R // GLA_DV),
                pl.BlockSpec((N_SAMPLE, LANES), lambda h, bb: (SAMPLE_ROW_BLOCK, 0)),
                pl.BlockSpec((None, LANES, GLA_DK), lambda h, bb: (layer, 0, h)),
                pl.BlockSpec((None, 1, GLA_DK), lambda h, bb: (layer, 0, h)),
                pl.BlockSpec((None, 1, GLA_DV), lambda h, bb: (layer, 0, 0))]
    args = [proj, proj, proj, proj, alr, wup, bg, gn]
    return _sample_call(functools.partial(_gla_sample_kernel, layer=layer), 8, in_specs, args,
                        GLA_HEADS, GLA_DV, 0, state_in, merged, state_prev, layer, "gla_sample")


def _hgrn_sample(proj, gamma, hn, state_in, merged, state_prev, layer):
    in_specs = [_sample_tile(HGRN_DK, COL_HQ // LANES),
                _sample_tile(HGRN_DK, COL_HF // LANES),
                _sample_tile(HGRN_DV, COL_HI // LANES),
                _sample_tile(HGRN_DV, COL_HG // LANES),
                pl.BlockSpec((DEPTH, HGRN_DK), lambda h, bb: (0, h)),
                pl.BlockSpec((None, 1, HGRN_DV), lambda h, bb: (layer, 0, 0))]
    args = [proj, proj, proj, proj, gamma, hn]
    return _sample_call(functools.partial(_hgrn_sample_kernel, layer=layer), 6, in_specs, args,
                        HGRN_HEADS, HGRN_DV, GLA_WIDTH // HGRN_DV, state_in, merged, state_prev,
                        layer, "hgrn_sample")


def kernel(x_prompt, x_sample, state_gla, state_hgrn, norm_gains, ffn1_w_in, ffn1_w_out,
           ffn2_w_in, ffn2_w_out, mix_w_in, gla_w_gate_up, gla_b_gate, gla_norm, hgrn_gamma,
           hgrn_norm, mix_w_out):
    xp = x_prompt.reshape(N_PROMPT, D_MODEL)
    xs = jnp.transpose(x_sample, (1, 0, 2)).reshape(N_SAMPLE, D_MODEL)
    gains = norm_gains.reshape(DEPTH * 6, 1, D_MODEL)
    w_mix_t = jnp.swapaxes(mix_w_in, 1, 2)
    wup = jnp.pad(gla_w_gate_up, ((0, 0), (0, LANES - GLA_RANK), (0, 0)))
    bg = gla_b_gate.reshape(DEPTH, 1, GLA_HEADS * GLA_DK)
    gn = gla_norm.reshape(DEPTH, 1, GLA_DV)
    hn = hgrn_norm.reshape(DEPTH, 1, HGRN_DV)

    h = _rms_cast(xp, xs, gains, 0)
    x = None
    gla_p, hgrn_p = [], []
    st_gla, st_hgrn = None, None
    for l in range(DEPTH):
        base = 6 * l
        act, wo = _ffn_in(h, ffn1_w_in, ffn1_w_out, l)
        if l == 0:
            x, h = _proj_out(act, wo, xp, gains, base + 1, base + 2, 0.5, x_sample=xs)
        else:
            x, h = _proj_out(act, wo, x, gains, base + 1, base + 2, 0.5)
        proj, alr = _mix_in(h, w_mix_t, l)
        merged, wo_mix = _mix_aux(mix_w_out, l)
        merged, sg = _gla_prompt(proj, alr, wup, bg, gn, merged, l)
        merged, sh = _hgrn_prompt(proj, hgrn_gamma, hn, merged, l)
        merged, st_gla = _gla_sample(proj, alr, wup, bg, gn, state_gla, merged, st_gla, l)
        merged, st_hgrn = _hgrn_sample(proj, hgrn_gamma, hn, state_hgrn, merged, st_hgrn, l)
        gla_p.append(sg)
        hgrn_p.append(sh)
        x, h = _proj_out(merged, wo_mix, x, gains, base + 3, base + 4, 1.0)
        act, wo = _ffn_in(h, ffn2_w_in, ffn2_w_out, l)
        if l + 1 < DEPTH:
            x, h = _proj_out(act, wo, x, gains, base + 5, base + 6, 0.5)
        else:
            y_prompt_rows, y_sample_rows = _proj_out(act, wo, x, gains, base + 5, None, 0.5)

    y_prompt = y_prompt_rows.reshape(BATCH, SEQ, D_MODEL)
    y_sample = jnp.transpose(y_sample_rows.reshape(DEC_SEQ, DEC_BATCH, D_MODEL), (1, 0, 2))
    return (y_prompt, y_sample, jnp.stack(gla_p), jnp.stack(hgrn_p), st_gla, st_hgrn)
```

```python
import functools

import jax
import jax.numpy as jnp
from jax import lax
from jax.experimental import pallas as pl
from jax.experimental.pallas import tpu as pltpu

F32 = jnp.float32
BF16 = jnp.bfloat16

D_MODEL = 2048
BATCH = 4
SEQ = 2048
DEPTH = 2
DEC_BATCH = 128
DEC_SEQ = 4
GLA_HEADS = 4
GLA_DK = 128
GLA_DV = 256
GLA_WIDTH = GLA_HEADS * GLA_DV
GLA_RANK = 16
GLA_TAU = 16.0
HGRN_HEADS = 8
HGRN_DK = 128
HGRN_DV = 128
HGRN_WIDTH = HGRN_HEADS * HGRN_DV
D_FF = 5632
EPS = 1e-6

N_PROMPT = BATCH * SEQ
N_SAMPLE = DEC_BATCH * DEC_SEQ
N_ROWS = N_PROMPT + N_SAMPLE
SAMPLE_ROW_BLOCK = N_PROMPT // N_SAMPLE

LANES = 128
SUBLANES = 8
VMEM_LIMIT = 60 * 1024 * 1024

COL_Q = 0
COL_K = 512
COL_V = 1024
COL_R = 2048
COL_HQ = 3072
COL_HF = 4096
COL_HI = 5120
COL_HG = 6144
PROJ_W = 7168
GATE_COL = 2048
MIX_ALIGNED_TILES = GATE_COL // 1024

TM_FFN = 1088
TM_IN = 1088
TN = 512
MIX_TN = 1024
FFN_NC = 256
TM_OUT = 1088
TK_CHOICES = (1408, 1024, 512)
OUT_NC = 512
MIX_WO_ROWS = D_MODEL // (N_ROWS // TM_IN)
OUT_RC = 272
LAST_TILE = N_ROWS // TM_OUT - 1
TAIL_PROMPT_ROWS = N_PROMPT - LAST_TILE * TM_OUT
TM_ROW = 512
CHUNK = 256
SCAN_HEADS = 2
SCAN_UNROLL = 1
SAMPLE_NB = 16
NBLK = DEC_BATCH // SAMPLE_NB


def _params(sem):
    return pltpu.CompilerParams(dimension_semantics=sem, vmem_limit_bytes=VMEM_LIMIT)


def _sigmoid(x):
    return 1.0 / (1.0 + jnp.exp(-x))


def _silu(x):
    return x * _sigmoid(x)


def _rms(y, g):
    return y * lax.rsqrt(jnp.mean(y * y, axis=-1, keepdims=True) + EPS) * g


def _skip_carries(kern, n_in, n_carry):
    def wrapped(*refs):
        return kern(*refs[:n_in], *refs[n_in + n_carry:])
    return wrapped


def _rms_cast_kernel(xp_ref, xs_ref, g_ref, h_ref):
    prompt_tiles = N_PROMPT // TM_ROW

    @pl.when(pl.program_id(0) < prompt_tiles)
    def _():
        h_ref[...] = _rms(xp_ref[...], g_ref[...]).astype(BF16)

    @pl.when(pl.program_id(0) >= prompt_tiles)
    def _():
        h_ref[...] = _rms(xs_ref[...], g_ref[...]).astype(BF16)


def _rms_cast(x_prompt, x_sample, gains, gi):
    prompt_tiles = N_PROMPT // TM_ROW
    return pl.pallas_call(
        _rms_cast_kernel,
        grid=(N_ROWS // TM_ROW,),
        in_specs=[pl.BlockSpec((TM_ROW, D_MODEL), lambda m: (jnp.minimum(m, prompt_tiles - 1), 0)),
                  pl.BlockSpec((TM_ROW, D_MODEL), lambda m: (jnp.maximum(m - prompt_tiles, 0), 0)),
                  pl.BlockSpec((None, 1, D_MODEL), lambda m: (gi, 0, 0))],
        out_specs=pl.BlockSpec((TM_ROW, D_MODEL), lambda m: (m, 0)),
        out_shape=jax.ShapeDtypeStruct((N_ROWS, D_MODEL), BF16),
        compiler_params=_params(("arbitrary",)),
        name="rms_cast",
    )(x_prompt, x_sample, gains)


def _ffn_in_kernel(h_ref, wg_ref, wu_ref, wo_ref, o_ref, wob_ref, wg_s, wu_s):
    @pl.when(pl.program_id(1) == 0)
    def _():
        wg_s[...] = wg_ref[...].astype(BF16)
        wu_s[...] = wu_ref[...].astype(BF16)

    wob_ref[...] = wo_ref[...].astype(BF16)
    h = h_ref[...]
    for c in range(TN // FFN_NC):
        cols = slice(c * FFN_NC, (c + 1) * FFN_NC)
        g = jnp.dot(h, wg_s[:, cols], preferred_element_type=F32)
        u = jnp.dot(h, wu_s[:, cols], preferred_element_type=F32)
        o_ref[:, cols] = (g * _sigmoid(g) * u).astype(BF16)


def _ffn_in(h, w_in, w_out, layer):
    nt = D_FF // TN
    nm = N_ROWS // TM_FFN
    slab = D_FF // (nt * nm)
    return pl.pallas_call(
        _ffn_in_kernel,
        grid=(nt, nm),
        in_specs=[pl.BlockSpec((TM_FFN, D_MODEL), lambda n, m: (m, 0)),
                  pl.BlockSpec((None, D_MODEL, TN), lambda n, m: (layer, 0, n)),
                  pl.BlockSpec((None, D_MODEL, TN), lambda n, m: (layer, 0, n + nt)),
                  pl.BlockSpec((None, slab, D_MODEL), lambda n, m: (layer, n * nm + m, 0))],
        out_specs=[pl.BlockSpec((TM_FFN, TN), lambda n, m: (m, n)),
                   pl.BlockSpec((slab, D_MODEL), lambda n, m: (n * nm + m, 0))],
        out_shape=[jax.ShapeDtypeStruct((N_ROWS, D_FF), BF16),
                   jax.ShapeDtypeStruct((D_FF, D_MODEL), BF16)],
        scratch_shapes=[pltpu.VMEM((D_MODEL, TN), BF16), pltpu.VMEM((D_MODEL, TN), BF16)],
        compiler_params=_params(("arbitrary", "arbitrary")),
        name="ffn_in",
    )(h, w_in, w_in, w_out)


def _mix_in_kernel(h_ref, wa_ref, wb_ref, o_ref, a_ref, w_s, wg_s):
    nt = (((1,), (1,)), ((), ()))
    n = pl.program_id(0)
    first_row_tile = pl.program_id(1) == 0
    gate_tile = n == MIX_ALIGNED_TILES
    kept = MIX_TN - GLA_RANK

    @pl.when(jnp.logical_and(first_row_tile, n < MIX_ALIGNED_TILES))
    def _():
        w_s[...] = wa_ref[...].astype(BF16)

    @pl.when(jnp.logical_and(first_row_tile, n >= MIX_ALIGNED_TILES))
    def _():
        w_s[0:kept, :] = wa_ref[GLA_RANK:MIX_TN, :].astype(BF16)
        w_s[kept:MIX_TN, :] = wb_ref[0:GLA_RANK, :].astype(BF16)

    @pl.when(jnp.logical_and(first_row_tile, gate_tile))
    def _():
        wg_s[...] = jnp.zeros_like(wg_s)
        wg_s[0:GLA_RANK, :] = wa_ref[0:GLA_RANK, :].astype(BF16)

    h = h_ref[...]
    o_ref[...] = lax.dot_general(h, w_s[...], nt, preferred_element_type=F32)

    @pl.when(gate_tile)
    def _():
        a_ref[...] = lax.dot_general(h, wg_s[...], nt, preferred_element_type=F32)


def _mix_in(h, w_in_t, layer):
    last = N_ROWS // TM_IN - 1
    slabs_per_tile = MIX_TN // LANES

    def gate_block(n, m):
        return jnp.where(n == MIX_ALIGNED_TILES, m, jnp.where(n < MIX_ALIGNED_TILES, 0, last))

    return pl.pallas_call(
        _mix_in_kernel,
        grid=(PROJ_W // MIX_TN, N_ROWS // TM_IN),
        in_specs=[pl.BlockSpec((TM_IN, D_MODEL), lambda n, m: (m, 0)),
                  pl.BlockSpec((None, MIX_TN, D_MODEL), lambda n, m: (layer, n, 0)),
                  pl.BlockSpec((None, LANES, D_MODEL),
                               lambda n, m: (layer, (n + 1) * slabs_per_tile, 0))],
        out_specs=[pl.BlockSpec((TM_IN, MIX_TN), lambda n, m: (m, n)),
                   pl.BlockSpec((TM_IN, LANES), lambda n, m: (gate_block(n, m), 0))],
        out_shape=[jax.ShapeDtypeStruct((N_ROWS, PROJ_W), F32),
                   jax.ShapeDtypeStruct((N_ROWS, LANES), F32)],
        scratch_shapes=[pltpu.VMEM((MIX_TN, D_MODEL), BF16), pltpu.VMEM((LANES, D_MODEL), BF16)],
        compiler_params=_params(("arbitrary", "arbitrary")),
        name="mix_in",
    )(h, w_in_t, w_in_t)


def _mix_aux_kernel(wo_ref, z_ref, wob_ref):
    z_ref[...] = jnp.zeros_like(z_ref)
    wob_ref[...] = wo_ref[...].astype(BF16)


def _mix_aux(w_out, layer):
    return pl.pallas_call(
        _mix_aux_kernel,
        grid=(N_ROWS // TM_IN,),
        in_specs=[pl.BlockSpec((None, MIX_WO_ROWS, D_MODEL), lambda m: (layer, m, 0))],
        out_specs=[pl.BlockSpec((TM_IN, D_MODEL), lambda m: (m, 0)),
                   pl.BlockSpec((MIX_WO_ROWS, D_MODEL), lambda m: (m, 0))],
        out_shape=[jax.ShapeDtypeStruct((N_ROWS, D_MODEL), BF16),
                   jax.ShapeDtypeStruct((D_MODEL, D_MODEL), BF16)],
        compiler_params=_params(("arbitrary",)),
        name="mix_aux",
    )(w_out)


def _proj_out_kernel(*refs, alpha, nk, split_x, final):
    refs = list(refs)
    a_ref, w_ref, x_hbm = refs[:3]
    xs_hbm = refs[3] if split_x else None
    gp_ref = refs[3 + split_x]
    gn_ref = None if final else refs[4 + split_x]
    xo_ref, o2_ref, xbuf, sem = refs[-4:]
    m = pl.program_id(0)
    k = pl.program_id(1)

    def tile_copy():
        rows = pl.ds(pl.multiple_of(m * TM_OUT, TM_OUT), TM_OUT)
        return pltpu.make_async_copy(x_hbm.at[rows, :], xbuf, sem.at[0])

    def tail_copies():
        head = pltpu.make_async_copy(x_hbm.at[pl.ds(LAST_TILE * TM_OUT, TAIL_PROMPT_ROWS), :],
                                     xbuf.at[pl.ds(0, TAIL_PROMPT_ROWS), :], sem.at[0])
        tail = pltpu.make_async_copy(xs_hbm, xbuf.at[pl.ds(TAIL_PROMPT_ROWS, N_SAMPLE), :],
                                     sem.at[1])
        return head, tail

    def residual_copies(act):
        if not split_x:
            act(tile_copy())
            return

        @pl.when(m < LAST_TILE)
        def _():
            act(tile_copy())

        @pl.when(m == LAST_TILE)
        def _():
            for c in tail_copies():
                act(c)

    def accumulate(first):
        a = a_ref[...]
        for c in range(D_MODEL // OUT_NC):
            cols = slice(c * OUT_NC, (c + 1) * OUT_NC)
            p = jnp.dot(a, w_ref[:, cols], preferred_element_type=F32)
            xo_ref[:, cols] = p if first else xo_ref[:, cols] + p

    @pl.when(k == 0)
    def _():
        residual_copies(lambda c: c.start())
        accumulate(True)

    @pl.when(k > 0)
    def _():
        accumulate(False)

    @pl.when(k == nk - 1)
    def _():
        residual_copies(lambda c: c.wait())

        def rows_body(i, carry):
            rows = pl.ds(pl.multiple_of(i * OUT_RC, OUT_RC), OUT_RC)
            xn = xbuf[rows, :] + alpha * _rms(xo_ref[rows, :], gp_ref[...])
            xo_ref[rows, :] = xn
            if not final:
                o2_ref[rows, :] = _rms(xn, gn_ref[...]).astype(BF16)
            return carry

        lax.fori_loop(0, TM_OUT // OUT_RC, rows_body, 0)
        if final:
            @pl.when(m == LAST_TILE)
            def _():
                o2_ref[...] = xo_ref[pl.ds(TAIL_PROMPT_ROWS, N_SAMPLE), :]


def _proj_out(a, w, x, gains, gi_post, gi_next, alpha, x_sample=None):
    kdim = a.shape[1]
    tk = next(t for t in TK_CHOICES if kdim % t == 0)
    nk = kdim // tk
    split_x = x_sample is not None
    final = gi_next is None
    kern = functools.partial(_proj_out_kernel, alpha=alpha, nk=nk, split_x=split_x, final=final)
    gain = lambda gi: pl.BlockSpec((None, 1, D_MODEL), lambda m, k: (gi, 0, 0))
    in_specs = [pl.BlockSpec((TM_OUT, tk), lambda m, k: (m, k)),
                pl.BlockSpec((tk, D_MODEL), lambda m, k: (k, 0)),
                pl.BlockSpec(memory_space=pl.ANY)]
    args = [a, w, x]
    if split_x:
        in_specs.append(pl.BlockSpec(memory_space=pl.ANY))
        args.append(x_sample)
    in_specs.append(gain(gi_post))
    args.append(gains)
    row_tile = pl.BlockSpec((TM_OUT, D_MODEL), lambda m, k: (m, 0))
    if final:
        out_specs = [row_tile, pl.BlockSpec((N_SAMPLE, D_MODEL), lambda m, k: (0, 0))]
        out_shape = [jax.ShapeDtypeStruct((N_PROMPT, D_MODEL), F32),
                     jax.ShapeDtypeStruct((N_SAMPLE, D_MODEL), F32)]
    else:
        in_specs.append(gain(gi_next))
        args.append(gains)
        out_specs = [row_tile, row_tile]
        out_shape = [jax.ShapeDtypeStruct((N_ROWS, D_MODEL), F32),
                     jax.ShapeDtypeStruct((N_ROWS, D_MODEL), BF16)]
    return pl.pallas_call(
        kern,
        grid=(N_ROWS // TM_OUT, nk),
        in_specs=in_specs,
        out_specs=out_specs,
        out_shape=out_shape,
        scratch_shapes=[pltpu.VMEM((TM_OUT, D_MODEL), F32), pltpu.SemaphoreType.DMA((2,))],
        compiler_params=_params(("arbitrary", "arbitrary")),
        name="proj_out",
    )(*args)


def _log_sigmoid(x):
    return jnp.minimum(x, 0.0) - jnp.log(1.0 + jnp.exp(-jnp.abs(x)))


def _gla_gate(a, wup_ref, bg_ref):
    pre = jnp.dot(a.astype(BF16), wup_ref[...].astype(BF16), preferred_element_type=F32)
    return _log_sigmoid(pre + bg_ref[...]) * (1.0 / GLA_TAU)


def _hgrn_lower_bound(gam_ref, layer):
    gam = gam_ref[...]
    mx = jnp.max(gam, axis=0, keepdims=True)
    e = jnp.exp(gam - mx)
    probs = e / jnp.sum(e, axis=0, keepdims=True)
    acc = probs[0:1, :]
    for j in range(1, layer + 1):
        acc = acc + probs[j:j + 1, :]
    return acc - probs[0:1, :]


def _hgrn_gate(z, lb):
    e = jnp.exp(-jnp.abs(z))
    r = 1.0 / (1.0 + e)
    er = e * r
    pos = z >= 0.0
    sig = jnp.where(pos, r, er)
    sig_neg = jnp.where(pos, er, r)
    return (1.0 - lb) * sig_neg, jnp.log(lb + (1.0 - lb) * sig)


def _chunk_levels(c):
    levels, m = [], 2
    while m <= c:
        levels.append(m)
        m *= 2
    return levels


def _midpoint_distance(cum, g, m):
    c = cum.shape[0]
    hm = m // 2
    if m == 2:
        odd = jnp.bitwise_and(lax.broadcasted_iota(jnp.int32, cum.shape, 0), 1) == 1
        return jnp.where(odd, jnp.abs(g), 0.0)
    if m < 2 * SUBLANES:
        cum3 = cum.reshape(c // SUBLANES, SUBLANES, LANES)
        sub = lax.broadcasted_iota(jnp.int32, cum3.shape, 1)
        mid = None
        for b0 in range(0, SUBLANES, m):
            cand = jnp.broadcast_to(cum3[:, b0 + hm - 1:b0 + hm, :], cum3.shape)
            mid = cand if mid is None else jnp.where(sub >= b0, cand, mid)
        return jnp.abs(cum3 - mid).reshape(c, LANES)
    mid = cum.reshape(c // m, m, LANES)[:, hm - 1:hm, :]
    mid = jnp.broadcast_to(mid, (c // m, m, LANES)).reshape(c, LANES)
    return jnp.abs(cum - mid)


def _pair_levels(c):
    row = lax.broadcasted_iota(jnp.int32, (c, c), 0)
    col = lax.broadcasted_iota(jnp.int32, (c, c), 1)
    diff = jnp.bitwise_xor(row, col)
    lvl = jnp.zeros((c, c), jnp.int32)
    for j in range(c.bit_length() - 1):
        lvl = lvl + (diff >= (1 << j)).astype(jnp.int32)
    return jnp.where(col > row, -1, lvl)


def _chunk_scan(qs, ks, vs, gs, s_refs, lvl):
    heads = range(len(qs))
    c = qs[0].shape[0]
    nt = (((1,), (1,)), ((), ()))
    eye = lvl == 0
    tri = (lvl >= 0).astype(BF16)
    g_hi = [g.astype(BF16) for g in gs]
    g_lo = [(gs[i] - g_hi[i].astype(F32)).astype(BF16) for i in heads]
    cums = [jnp.dot(tri, g_hi[i], preferred_element_type=F32)
            + jnp.dot(tri, g_lo[i], preferred_element_type=F32) for i in heads]

    s_old = [s_refs[i][...] for i in heads]
    vb = [v.astype(BF16) for v in vs]
    o = [jnp.dot((qs[i] * jnp.exp(cums[i])).astype(BF16), s_old[i].astype(BF16),
                 preferred_element_type=F32) for i in heads]

    for i in heads:
        last = cums[i][c - 1:c, :]
        kt_t = jnp.transpose(ks[i] * jnp.exp(last - cums[i])).astype(BF16)
        dec = jnp.broadcast_to(jnp.exp(last), (LANES, LANES))
        dec_col = jnp.sum(jnp.where(eye[:LANES, :LANES], dec, 0.0), axis=1, keepdims=True)
        s_refs[i][...] = s_old[i] * dec_col + jnp.dot(kt_t, vb[i], preferred_element_type=F32)

    rowl = lax.broadcasted_iota(jnp.int32, (c, LANES), 0)
    a = [jnp.where(eye, lax.dot_general(qs[i].astype(BF16), ks[i].astype(BF16), nt,
                                        preferred_element_type=F32), 0.0) for i in heads]
    for m in _chunk_levels(c):
        second = jnp.bitwise_and(rowl, m // 2) != 0
        mask = lvl == m.bit_length() - 1
        x = [(jnp.where(second, qs[i], ks[i])
              * jnp.exp(-_midpoint_distance(cums[i], gs[i], m))).astype(BF16) for i in heads]
        am = [lax.dot_general(x[i], x[i], nt, preferred_element_type=F32) for i in heads]
        a = [jnp.where(mask, am[i], a[i]) for i in heads]
    return o, [x.astype(BF16) for x in a], vb


def _head_cols(ref, rows, i, width):
    return ref[rows, i * width:(i + 1) * width]


def _skewed_chunk_loop(front, finish, pend_refs):
    oi_s, pa_s, vb_s = pend_refs
    heads = range(SCAN_HEADS)

    def stash(parts):
        ois, pas, vbs = parts
        for i in heads:
            oi_s[i] = ois[i]
            pa_s[i] = pas[i]
            vb_s[i] = vbs[i]

    def issue():
        return [oi_s[i] + jnp.dot(pa_s[i], vb_s[i], preferred_element_type=F32) for i in heads]

    stash(front(0))

    def body(ci, carry):
        os = issue()
        stash(front(ci))
        finish(ci - 1, os)
        return carry

    n = SEQ // CHUNK
    lax.fori_loop(1, n, body, 0, unroll=SCAN_UNROLL)
    finish(n - 1, issue())


def _chunk_rows(ci):
    return pl.ds(pl.multiple_of(ci * CHUNK, CHUNK), CHUNK)


def _gla_prompt_kernel(q_ref, k_ref, v_ref, r_ref, a_ref, wup_ref, bg_ref, gn_ref, o_ref, s_ref,
                       oi_s, pa_s, vb_s):
    heads = range(SCAN_HEADS)
    s_ref[...] = jnp.zeros_like(s_ref)
    lvl = _pair_levels(CHUNK)

    def front(ci):
        rows = _chunk_rows(ci)
        a = a_ref[rows, :].astype(BF16)
        pre = jnp.dot(a, wup_ref[...].astype(BF16), preferred_element_type=F32) + bg_ref[...]
        gs = [_log_sigmoid(pre[:, i * GLA_DK:(i + 1) * GLA_DK]) * (1.0 / GLA_TAU) for i in heads]
        qs = [_head_cols(q_ref, rows, i, GLA_DK) * (GLA_DK ** -0.5) for i in heads]
        ks = [_head_cols(k_ref, rows, i, GLA_DK) for i in heads]
        vs = [_head_cols(v_ref, rows, i, GLA_DV) for i in heads]
        return _chunk_scan(qs, ks, vs, gs, [s_ref.at[i] for i in heads], lvl)

    def finish(ci, os):
        rows = _chunk_rows(ci)
        for i in heads:
            gate = _silu(_head_cols(r_ref, rows, i, GLA_DV))
            o_ref[rows, i * GLA_DV:(i + 1) * GLA_DV] = (_rms(os[i], gn_ref[...]) * gate).astype(BF16)

    _skewed_chunk_loop(front, finish, (oi_s, pa_s, vb_s))


def _pending_scratch(dv):
    return [pltpu.VMEM((SCAN_HEADS, CHUNK, dv), F32),
            pltpu.VMEM((SCAN_HEADS, CHUNK, CHUNK), BF16),
            pltpu.VMEM((SCAN_HEADS, CHUNK, dv), BF16)]


def _gla_prompt(proj, alr, wup, bg, gn, merged, layer):
    p = SCAN_HEADS
    dk, dv = p * GLA_DK, p * GLA_DV
    cq, ck, cv, cr = (COL_Q // dk, COL_K // dk, COL_V // dv, COL_R // dv)
    return pl.pallas_call(
        _skip_carries(_gla_prompt_kernel, 8, 1),
        grid=(BATCH, GLA_HEADS // p),
        in_specs=[pl.BlockSpec((SEQ, dk), lambda b, h: (b, cq + h)),
                  pl.BlockSpec((SEQ, dk), lambda b, h: (b, ck + h)),
                  pl.BlockSpec((SEQ, dv), lambda b, h: (b, cv + h)),
                  pl.BlockSpec((SEQ, dv), lambda b, h: (b, cr + h)),
                  pl.BlockSpec((SEQ, LANES), lambda b, h: (b, 0)),
                  pl.BlockSpec((None, LANES, dk), lambda b, h: (layer, 0, h)),
                  pl.BlockSpec((None, 1, dk), lambda b, h: (layer, 0, h)),
                  pl.BlockSpec((None, 1, GLA_DV), lambda b, h: (layer, 0, 0)),
                  pl.BlockSpec(memory_space=pl.ANY)],
        out_specs=[pl.BlockSpec((SEQ, dv), lambda b, h: (b, h)),
                   pl.BlockSpec((None, p, GLA_DK, GLA_DV), lambda b, h: (b, h, 0, 0))],
        out_shape=[jax.ShapeDtypeStruct((N_ROWS, D_MODEL), BF16),
                   jax.ShapeDtypeStruct((BATCH, GLA_HEADS, GLA_DK, GLA_DV), F32)],
        scratch_shapes=_pending_scratch(GLA_DV),
        input_output_aliases={8: 0},
        compiler_params=_params(("arbitrary", "arbitrary")),
        name="gla_prompt",
    )(proj, proj, proj, proj, alr, wup, bg, gn, merged)


def _hgrn_prompt_kernel(hq_ref, hf_ref, hi_ref, hg_ref, gam_ref, hn_ref, o_ref, s_ref,
                        oi_s, pa_s, vb_s, *, layer):
    heads = range(SCAN_HEADS)
    s_ref[...] = jnp.zeros_like(s_ref)
    lvl = _pair_levels(CHUNK)
    lb = _hgrn_lower_bound(gam_ref, layer)

    def front(ci):
        rows = _chunk_rows(ci)
        qs = [_silu(_head_cols(hq_ref, rows, i, HGRN_DK)) * (HGRN_DK ** -0.5) for i in heads]
        kg = [_hgrn_gate(_head_cols(hf_ref, rows, i, HGRN_DK),
                         lb[:, i * HGRN_DK:(i + 1) * HGRN_DK]) for i in heads]
        vs = [_head_cols(hi_ref, rows, i, HGRN_DV) for i in heads]
        return _chunk_scan(qs, [x[0] for x in kg], vs, [x[1] for x in kg],
                           [s_ref.at[i] for i in heads], lvl)

    def finish(ci, os):
        rows = _chunk_rows(ci)
        for i in heads:
            gate = _silu(_head_cols(hg_ref, rows, i, HGRN_DV))
            o_ref[rows, i * HGRN_DV:(i + 1) * HGRN_DV] = (
                _rms(os[i], hn_ref[...]) * gate).astype(BF16)

    _skewed_chunk_loop(front, finish, (oi_s, pa_s, vb_s))


def _hgrn_prompt(proj, gamma, hn, merged, layer):
    p = SCAN_HEADS
    dk, dv = p * HGRN_DK, p * HGRN_DV
    cq, cf, ci, cg = (COL_HQ // dk, COL_HF // dk, COL_HI // dv, COL_HG // dv)
    co = GLA_WIDTH // dv
    kern = _skip_carries(functools.partial(_hgrn_prompt_kernel, layer=layer), 6, 1)
    return pl.pallas_call(
        kern,
        grid=(BATCH, HGRN_HEADS // p),
        in_specs=[pl.BlockSpec((SEQ, dk), lambda b, h: (b, cq + h)),
                  pl.BlockSpec((SEQ, dk), lambda b, h: (b, cf + h)),
                  pl.BlockSpec((SEQ, dv), lambda b, h: (b, ci + h)),
                  pl.BlockSpec((SEQ, dv), lambda b, h: (b, cg + h)),
                  pl.BlockSpec((DEPTH, dk), lambda b, h: (0, h)),
                  pl.BlockSpec((None, 1, HGRN_DV), lambda b, h: (layer, 0, 0)),
                  pl.BlockSpec(memory_space=pl.ANY)],
        out_specs=[pl.BlockSpec((SEQ, dv), lambda b, h: (b, co + h)),
                   pl.BlockSpec((None, p, HGRN_DK, HGRN_DV), lambda b, h: (b, h, 0, 0))],
        out_shape=[jax.ShapeDtypeStruct((N_ROWS, D_MODEL), BF16),
                   jax.ShapeDtypeStruct((BATCH, HGRN_HEADS, HGRN_DK, HGRN_DV), F32)],
        scratch_shapes=_pending_scratch(HGRN_DV),
        input_output_aliases={6: 0},
        compiler_params=_params(("arbitrary", "arbitrary")),
        name="hgrn_prompt",
    )(proj, proj, proj, proj, gamma, hn, merged)


def _sample_setup(q, k, v, g, oin_s, qh_s, kt_s, dt_s):
    cum = [g[0]]
    for t in range(1, DEC_SEQ):
        cum.append(cum[t - 1] + g[t])
    for t in range(DEC_SEQ):
        o = jnp.zeros_like(v[0])
        for s in range(t + 1):
            w = q[t] * k[s] if s == t else q[t] * (k[s] * jnp.exp(cum[t] - cum[s]))
            o = o + jnp.sum(w, axis=-1, keepdims=True) * v[s]
        oin_s[t] = o
    last = cum[DEC_SEQ - 1]
    for t in range(DEC_SEQ):
        qh_s[t] = q[t] * jnp.exp(cum[t])
        kt = jnp.transpose(k[t] * jnp.exp(last - cum[t]))
        for blk in range(NBLK):
            kt_s[blk, :, t * SAMPLE_NB:(t + 1) * SAMPLE_NB] = (
                kt[:, blk * SAMPLE_NB:(blk + 1) * SAMPLE_NB])
    dt = jnp.transpose(jnp.exp(last))
    for blk in range(NBLK):
        dt_s[blk] = dt[:, blk * SAMPLE_NB:(blk + 1) * SAMPLE_NB]


def _sample_step(bb, v_ref, sin_ref, sout_ref, qh_s, kt_s, dt_s, layer):
    if layer == 0:
        for later in range(1, DEPTH):
            sout_ref[later] = jnp.zeros(sout_ref.shape[1:], F32)
        sout_ref = sout_ref.at[0]
    nb = SAMPLE_NB
    nrow = DEC_SEQ * nb
    r0 = pl.multiple_of(bb * nb, nb)
    qh = jnp.concatenate([qh_s[t, pl.ds(r0, nb), :] for t in range(DEC_SEQ)], axis=0)
    vv = jnp.concatenate(
        [v_ref[pl.ds(pl.multiple_of(t * DEC_BATCH + r0, nb), nb), :] for t in range(DEC_SEQ)],
        axis=0)
    rj = jnp.bitwise_and(lax.broadcasted_iota(jnp.int32, (nrow, nb * LANES), 0), nb - 1)
    cb = lax.shift_right_logical(lax.broadcasted_iota(jnp.int32, (nrow, nb * LANES), 1), 7)
    q_bd = jnp.where(rj == cb, jnp.concatenate([qh] * nb, axis=1), 0.0).astype(BF16)
    s_old = sin_ref[...]
    s_stack = s_old.reshape(nb * LANES, s_old.shape[-1])
    o = jnp.dot(q_bd, s_stack.astype(BF16), preferred_element_type=F32)

    kt64 = kt_s[bb]
    lane_j = jnp.bitwise_and(lax.broadcasted_iota(jnp.int32, kt64.shape, 1), nb - 1)
    kt_bd = jnp.concatenate([jnp.where(lane_j == j, kt64, 0.0) for j in range(nb)],
                            axis=0).astype(BF16)
    upd = jnp.dot(kt_bd, vv.astype(BF16), preferred_element_type=F32)
    dcols = dt_s[bb]
    for j in range(nb):
        sout_ref[j] = s_old[j] * dcols[:, j:j + 1] + upd[j * LANES:(j + 1) * LANES, :]
    return o


def _sample_emit(bb, o, oin_s, gate_ref, norm_ref, o_ref):
    nb = SAMPLE_NB
    r0 = pl.multiple_of(bb * nb, nb)
    for t in range(DEC_SEQ):
        rows = pl.ds(pl.multiple_of(t * DEC_BATCH + r0, nb), nb)
        ot = o[t * nb:(t + 1) * nb, :] + oin_s[t, pl.ds(r0, nb), :]
        o_ref[rows, :] = (_rms(ot, norm_ref[...]) * _silu(gate_ref[rows, :])).astype(BF16)


def _tiles(ref):
    return [ref[t * DEC_BATCH:(t + 1) * DEC_BATCH, :] for t in range(DEC_SEQ)]


def _gla_sample_kernel(q_ref, k_ref, v_ref, r_ref, a_ref, wup_ref, bg_ref, gn_ref, sin_ref,
                       o_ref, sout_ref, oin_s, qh_s, kt_s, dt_s, *, layer):
    bb = pl.program_id(1)

    @pl.when(bb == 0)
    def _():
        q = [x * (GLA_DK ** -0.5) for x in _tiles(q_ref)]
        g = [_gla_gate(x, wup_ref, bg_ref) for x in _tiles(a_ref)]
        _sample_setup(q, _tiles(k_ref), _tiles(v_ref), g, oin_s, qh_s, kt_s, dt_s)

    o = _sample_step(bb, v_ref, sin_ref, sout_ref, qh_s, kt_s, dt_s, layer)
    _sample_emit(bb, o, oin_s, r_ref, gn_ref, o_ref)


def _hgrn_sample_kernel(hq_ref, hf_ref, hi_ref, hg_ref, gam_ref, hn_ref, sin_ref,
                        o_ref, sout_ref, oin_s, qh_s, kt_s, dt_s, *, layer):
    bb = pl.program_id(1)

    @pl.when(bb == 0)
    def _():
        lb = _hgrn_lower_bound(gam_ref, layer)
        q = [_silu(x) * (HGRN_DK ** -0.5) for x in _tiles(hq_ref)]
        kg = [_hgrn_gate(x, lb) for x in _tiles(hf_ref)]
        _sample_setup(q, [x[0] for x in kg], _tiles(hi_ref), [x[1] for x in kg],
                      oin_s, qh_s, kt_s, dt_s)

    o = _sample_step(bb, hi_ref, sin_ref, sout_ref, qh_s, kt_s, dt_s, layer)
    _sample_emit(bb, o, oin_s, hg_ref, hn_ref, o_ref)


def _sample_scratch(dv):
    return [pltpu.VMEM((DEC_SEQ, DEC_BATCH, dv), F32),
            pltpu.VMEM((DEC_SEQ, DEC_BATCH, LANES), F32),
            pltpu.VMEM((NBLK, LANES, DEC_SEQ * SAMPLE_NB), F32),
            pltpu.VMEM((NBLK, LANES, SAMPLE_NB), F32)]


def _sample_call(kern, n_in, in_specs, args, heads, dv, col_out, state_in, merged, state_prev,
                 layer, name):
    st_spec = pl.BlockSpec((None, SAMPLE_NB, None, LANES, dv), lambda h, bb: (layer, bb, h, 0, 0))
    st_out_spec = st_spec
    if state_prev is None:
        st_out_spec = pl.BlockSpec((DEPTH, SAMPLE_NB, None, LANES, dv),
                                   lambda h, bb: (0, bb, h, 0, 0))
    carries = [merged] if state_prev is None else [merged, state_prev]
    in_specs = in_specs + [st_spec] + [pl.BlockSpec(memory_space=pl.ANY)] * len(carries)
    args = args + [state_in] + carries
    aliases = {n_in + 1 + i: i for i in range(len(carries))}
    return pl.pallas_call(
        _skip_carries(kern, n_in + 1, len(carries)),
        grid=(heads, NBLK),
        in_specs=in_specs,
        out_specs=[pl.BlockSpec((N_SAMPLE, dv), lambda h, bb: (SAMPLE_ROW_BLOCK, col_out + h)),
                   st_out_spec],
        out_shape=[jax.ShapeDtypeStruct((N_ROWS, D_MODEL), BF16),
                   jax.ShapeDtypeStruct(state_in.shape, F32)],
        scratch_shapes=_sample_scratch(dv),
        input_output_aliases=aliases,
        compiler_params=_params(("arbitrary", "arbitrary")),
        name=name,
    )(*args)


def _sample_tile(width, col):
    return pl.BlockSpec((N_SAMPLE, width), lambda h, bb: (SAMPLE_ROW_BLOCK, col + h))


def _gla_sample(proj, alr, wup, bg, gn, state_in, merged, state_prev, layer):
    in_specs = [_sample_tile(GLA_DK, COL_Q // LANES),
                _sample_tile(GLA_DK, COL_K // LANES),
                _sample_tile(GLA_DV, COL_V // GLA_DV),
                _sample_tile(GLA_DV, COL_R // GLA_DV),
                pl.BlockSpec((N_SAMPLE, LANES), lambda h, bb: (SAMPLE_ROW_BLOCK, 0)),
                pl.BlockSpec((None, LANES, GLA_DK), lambda h, bb: (layer, 0, h)),
                pl.BlockSpec((None, 1, GLA_DK), lambda h, bb: (layer, 0, h)),
                pl.BlockSpec((None, 1, GLA_DV), lambda h, bb: (layer, 0, 0))]
    args = [proj, proj, proj, proj, alr, wup, bg, gn]
    return _sample_call(functools.partial(_gla_sample_kernel, layer=layer), 8, in_specs, args,
                        GLA_HEADS, GLA_DV, 0, state_in, merged, state_prev, layer, "gla_sample")


def _hgrn_sample(proj, gamma, hn, state_in, merged, state_prev, layer):
    in_specs = [_sample_tile(HGRN_DK, COL_HQ // LANES),
                _sample_tile(HGRN_DK, COL_HF // LANES),
                _sample_tile(HGRN_DV, COL_HI // LANES),
                _sample_tile(HGRN_DV, COL_HG // LANES),
                pl.BlockSpec((DEPTH, HGRN_DK), lambda h, bb: (0, h)),
                pl.BlockSpec((None, 1, HGRN_DV), lambda h, bb: (layer, 0, 0))]
    args = [proj, proj, proj, proj, gamma, hn]
    return _sample_call(functools.partial(_hgrn_sample_kernel, layer=layer), 6, in_specs, args,
                        HGRN_HEADS, HGRN_DV, GLA_WIDTH // HGRN_DV, state_in, merged, state_prev,
                        layer, "hgrn_sample")


def kernel(x_prompt, x_sample, state_gla, state_hgrn, norm_gains, ffn1_w_in, ffn1_w_out,
           ffn2_w_in, ffn2_w_out, mix_w_in, gla_w_gate_up, gla_b_gate, gla_norm, hgrn_gamma,
           hgrn_norm, mix_w_out):
    xp = x_prompt.reshape(N_PROMPT, D_MODEL)
    xs = jnp.transpose(x_sample, (1, 0, 2)).reshape(N_SAMPLE, D_MODEL)
    gains = norm_gains.reshape(DEPTH * 6, 1, D_MODEL)
    w_mix_t = jnp.swapaxes(mix_w_in, 1, 2)
    wup = jnp.pad(gla_w_gate_up, ((0, 0), (0, LANES - GLA_RANK), (0, 0)))
    bg = gla_b_gate.reshape(DEPTH, 1, GLA_HEADS * GLA_DK)
    gn = gla_norm.reshape(DEPTH, 1, GLA_DV)
    hn = hgrn_norm.reshape(DEPTH, 1, HGRN_DV)

    h = _rms_cast(xp, xs, gains, 0)
    x = None
    gla_p, hgrn_p = [], []
    st_gla, st_hgrn = None, None
    for l in range(DEPTH):
        base = 6 * l
        act, wo = _ffn_in(h, ffn1_w_in, ffn1_w_out, l)
        if l == 0:
            x, h = _proj_out(act, wo, xp, gains, base + 1, base + 2, 0.5, x_sample=xs)
        else:
            x, h = _proj_out(act, wo, x, gains, base + 1, base + 2, 0.5)
        proj, alr = _mix_in(h, w_mix_t, l)
        merged, wo_mix = _mix_aux(mix_w_out, l)
        merged, sg = _gla_prompt(proj, alr, wup, bg, gn, merged, l)
        merged, sh = _hgrn_prompt(proj, hgrn_gamma, hn, merged, l)
        merged, st_gla = _gla_sample(proj, alr, wup, bg, gn, state_gla, merged, st_gla, l)
        merged, st_hgrn = _hgrn_sample(proj, hgrn_gamma, hn, state_hgrn, merged, st_hgrn, l)
        gla_p.append(sg)
        hgrn_p.append(sh)
        x, h = _proj_out(merged, wo_mix, x, gains, base + 3, base + 4, 1.0)
        act, wo = _ffn_in(h, ffn2_w_in, ffn2_w_out, l)
        if l + 1 < DEPTH:
            x, h = _proj_out(act, wo, x, gains, base + 5, base + 6, 0.5)
        else:
            y_prompt_rows, y_sample_rows = _proj_out(act, wo, x, gains, base + 5, None, 0.5)

    y_prompt = y_prompt_rows.reshape(BATCH, SEQ, D_MODEL)
    y_sample = jnp.transpose(y_sample_rows.reshape(DEC_SEQ, DEC_BATCH, D_MODEL), (1, 0, 2))
    return (y_prompt, y_sample, jnp.stack(gla_p), jnp.stack(hgrn_p), st_gla, st_hgrn)
```

```python
import functools

import jax
import jax.numpy as jnp
from jax import lax
from jax.experimental import pallas as pl
from jax.experimental.pallas import tpu as pltpu

F32 = jnp.float32
BF16 = jnp.bfloat16

D_MODEL = 2048
BATCH = 4
SEQ = 2048
DEPTH = 2
DEC_BATCH = 128
DEC_SEQ = 4
GLA_HEADS = 4
GLA_DK = 128
GLA_DV = 256
GLA_WIDTH = GLA_HEADS * GLA_DV
GLA_RANK = 16
GLA_TAU = 16.0
HGRN_HEADS = 8
HGRN_DK = 128
HGRN_DV = 128
HGRN_WIDTH = HGRN_HEADS * HGRN_DV
D_FF = 5632
EPS = 1e-6

N_PROMPT = BATCH * SEQ
N_SAMPLE = DEC_BATCH * DEC_SEQ
N_ROWS = N_PROMPT + N_SAMPLE
SAMPLE_ROW_BLOCK = N_PROMPT // N_SAMPLE

LANES = 128
SUBLANES = 8
VMEM_LIMIT = 60 * 1024 * 1024

COL_Q = 0
COL_K = 512
COL_V = 1024
COL_R = 2048
COL_HQ = 3072
COL_HF = 4096
COL_HI = 5120
COL_HG = 6144
PROJ_W = 7168
GATE_COL = 2048
MIX_ALIGNED_TILES = GATE_COL // 1024

TM_FFN = 1088
TM_IN = 1088
TN = 512
MIX_TN = 1024
FFN_NC = 256
TM_OUT = 1088
TK_CHOICES = (1408, 1024, 512)
OUT_NC = 512
MIX_WO_ROWS = D_MODEL // (N_ROWS // TM_IN)
OUT_RC = 272
LAST_TILE = N_ROWS // TM_OUT - 1
TAIL_PROMPT_ROWS = N_PROMPT - LAST_TILE * TM_OUT
TM_ROW = 512
CHUNK = 256
SCAN_HEADS = 2
HGRN_SCAN_HEADS = 4
SCAN_UNROLL = 1
SAMPLE_NB = 16
GLA_SAMPLE_HEADS = 1
HGRN_SAMPLE_HEADS = 2
NBLK = DEC_BATCH // SAMPLE_NB


def _params(sem):
    return pltpu.CompilerParams(dimension_semantics=sem, vmem_limit_bytes=VMEM_LIMIT)


def _sigmoid(x):
    return 1.0 / (1.0 + jnp.exp(-x))


def _silu(x):
    return x * _sigmoid(x)


def _rms(y, g):
    return y * lax.rsqrt(jnp.mean(y * y, axis=-1, keepdims=True) + EPS) * g


def _skip_carries(kern, n_in, n_carry):
    def wrapped(*refs):
        return kern(*refs[:n_in], *refs[n_in + n_carry:])
    return wrapped


def _rms_cast_kernel(xp_ref, xs_ref, g_ref, h_ref):
    prompt_tiles = N_PROMPT // TM_ROW

    @pl.when(pl.program_id(0) < prompt_tiles)
    def _():
        h_ref[...] = _rms(xp_ref[...], g_ref[...]).astype(BF16)

    @pl.when(pl.program_id(0) >= prompt_tiles)
    def _():
        h_ref[...] = _rms(xs_ref[...], g_ref[...]).astype(BF16)


def _rms_cast(x_prompt, x_sample, gains, gi):
    prompt_tiles = N_PROMPT // TM_ROW
    return pl.pallas_call(
        _rms_cast_kernel,
        grid=(N_ROWS // TM_ROW,),
        in_specs=[pl.BlockSpec((TM_ROW, D_MODEL), lambda m: (jnp.minimum(m, prompt_tiles - 1), 0)),
                  pl.BlockSpec((TM_ROW, D_MODEL), lambda m: (jnp.maximum(m - prompt_tiles, 0), 0)),
                  pl.BlockSpec((None, 1, D_MODEL), lambda m: (gi, 0, 0))],
        out_specs=pl.BlockSpec((TM_ROW, D_MODEL), lambda m: (m, 0)),
        out_shape=jax.ShapeDtypeStruct((N_ROWS, D_MODEL), BF16),
        compiler_params=_params(("arbitrary",)),
        name="rms_cast",
    )(x_prompt, x_sample, gains)


def _ffn_in_kernel(h_ref, wg_ref, wu_ref, wo_ref, o_ref, wob_ref, wg_s, wu_s):
    @pl.when(pl.program_id(1) == 0)
    def _():
        wg_s[...] = wg_ref[...].astype(BF16)
        wu_s[...] = wu_ref[...].astype(BF16)

    wob_ref[...] = wo_ref[...].astype(BF16)
    h = h_ref[...]
    for c in range(TN // FFN_NC):
        cols = slice(c * FFN_NC, (c + 1) * FFN_NC)
        g = jnp.dot(h, wg_s[:, cols], preferred_element_type=F32)
        u = jnp.dot(h, wu_s[:, cols], preferred_element_type=F32)
        o_ref[:, cols] = (g * _sigmoid(g) * u).astype(BF16)


def _ffn_in(h, w_in, w_out, layer):
    nt = D_FF // TN
    nm = N_ROWS // TM_FFN
    slab = D_FF // (nt * nm)
    return pl.pallas_call(
        _ffn_in_kernel,
        grid=(nt, nm),
        in_specs=[pl.BlockSpec((TM_FFN, D_MODEL), lambda n, m: (m, 0)),
                  pl.BlockSpec((None, D_MODEL, TN), lambda n, m: (layer, 0, n)),
                  pl.BlockSpec((None, D_MODEL, TN), lambda n, m: (layer, 0, n + nt)),
                  pl.BlockSpec((None, slab, D_MODEL), lambda n, m: (layer, n * nm + m, 0))],
        out_specs=[pl.BlockSpec((TM_FFN, TN), lambda n, m: (m, n)),
                   pl.BlockSpec((slab, D_MODEL), lambda n, m: (n * nm + m, 0))],
        out_shape=[jax.ShapeDtypeStruct((N_ROWS, D_FF), BF16),
                   jax.ShapeDtypeStruct((D_FF, D_MODEL), BF16)],
        scratch_shapes=[pltpu.VMEM((D_MODEL, TN), BF16), pltpu.VMEM((D_MODEL, TN), BF16)],
        compiler_params=_params(("arbitrary", "arbitrary")),
        name="ffn_in",
    )(h, w_in, w_in, w_out)


def _mix_in_kernel(h_ref, wa_ref, wb_ref, o_ref, a_ref, w_s, wg_s):
    nt = (((1,), (1,)), ((), ()))
    n = pl.program_id(0)
    first_row_tile = pl.program_id(1) == 0
    gate_tile = n == MIX_ALIGNED_TILES
    kept = MIX_TN - GLA_RANK

    @pl.when(jnp.logical_and(first_row_tile, n < MIX_ALIGNED_TILES))
    def _():
        w_s[...] = wa_ref[...].astype(BF16)

    @pl.when(jnp.logical_and(first_row_tile, n >= MIX_ALIGNED_TILES))
    def _():
        w_s[0:kept, :] = wa_ref[GLA_RANK:MIX_TN, :].astype(BF16)
        w_s[kept:MIX_TN, :] = wb_ref[0:GLA_RANK, :].astype(BF16)

    @pl.when(jnp.logical_and(first_row_tile, gate_tile))
    def _():
        wg_s[...] = jnp.zeros_like(wg_s)
        wg_s[0:GLA_RANK, :] = wa_ref[0:GLA_RANK, :].astype(BF16)

    h = h_ref[...]
    o_ref[...] = lax.dot_general(h, w_s[...], nt, preferred_element_type=F32)

    @pl.when(gate_tile)
    def _():
        a_ref[...] = lax.dot_general(h, wg_s[...], nt, preferred_element_type=F32)


def _mix_in(h, w_in_t, layer):
    last = N_ROWS // TM_IN - 1
    slabs_per_tile = MIX_TN // LANES

    def gate_block(n, m):
        return jnp.where(n == MIX_ALIGNED_TILES, m, jnp.where(n < MIX_ALIGNED_TILES, 0, last))

    return pl.pallas_call(
        _mix_in_kernel,
        grid=(PROJ_W // MIX_TN, N_ROWS // TM_IN),
        in_specs=[pl.BlockSpec((TM_IN, D_MODEL), lambda n, m: (m, 0)),
                  pl.BlockSpec((None, MIX_TN, D_MODEL), lambda n, m: (layer, n, 0)),
                  pl.BlockSpec((None, LANES, D_MODEL),
                               lambda n, m: (layer, (n + 1) * slabs_per_tile, 0))],
        out_specs=[pl.BlockSpec((TM_IN, MIX_TN), lambda n, m: (m, n)),
                   pl.BlockSpec((TM_IN, LANES), lambda n, m: (gate_block(n, m), 0))],
        out_shape=[jax.ShapeDtypeStruct((N_ROWS, PROJ_W), F32),
                   jax.ShapeDtypeStruct((N_ROWS, LANES), F32)],
        scratch_shapes=[pltpu.VMEM((MIX_TN, D_MODEL), BF16), pltpu.VMEM((LANES, D_MODEL), BF16)],
        compiler_params=_params(("arbitrary", "arbitrary")),
        name="mix_in",
    )(h, w_in_t, w_in_t)


def _mix_aux_kernel(wo_ref, z_ref, wob_ref):
    z_ref[...] = jnp.zeros_like(z_ref)
    wob_ref[...] = wo_ref[...].astype(BF16)


def _mix_aux(w_out, layer):
    return pl.pallas_call(
        _mix_aux_kernel,
        grid=(N_ROWS // TM_IN,),
        in_specs=[pl.BlockSpec((None, MIX_WO_ROWS, D_MODEL), lambda m: (layer, m, 0))],
        out_specs=[pl.BlockSpec((TM_IN, D_MODEL), lambda m: (m, 0)),
                   pl.BlockSpec((MIX_WO_ROWS, D_MODEL), lambda m: (m, 0))],
        out_shape=[jax.ShapeDtypeStruct((N_ROWS, D_MODEL), BF16),
                   jax.ShapeDtypeStruct((D_MODEL, D_MODEL), BF16)],
        compiler_params=_params(("arbitrary",)),
        name="mix_aux",
    )(w_out)


def _proj_out_kernel(*refs, alpha, nk, split_x, final):
    refs = list(refs)
    a_ref, w_ref, x_hbm = refs[:3]
    xs_hbm = refs[3] if split_x else None
    gp_ref = refs[3 + split_x]
    gn_ref = None if final else refs[4 + split_x]
    xo_ref, o2_ref, xbuf, sem = refs[-4:]
    m = pl.program_id(0)
    k = pl.program_id(1)

    def tile_copy():
        rows = pl.ds(pl.multiple_of(m * TM_OUT, TM_OUT), TM_OUT)
        return pltpu.make_async_copy(x_hbm.at[rows, :], xbuf, sem.at[0])

    def tail_copies():
        head = pltpu.make_async_copy(x_hbm.at[pl.ds(LAST_TILE * TM_OUT, TAIL_PROMPT_ROWS), :],
                                     xbuf.at[pl.ds(0, TAIL_PROMPT_ROWS), :], sem.at[0])
        tail = pltpu.make_async_copy(xs_hbm, xbuf.at[pl.ds(TAIL_PROMPT_ROWS, N_SAMPLE), :],
                                     sem.at[1])
        return head, tail

    def residual_copies(act):
        if not split_x:
            act(tile_copy())
            return

        @pl.when(m < LAST_TILE)
        def _():
            act(tile_copy())

        @pl.when(m == LAST_TILE)
        def _():
            for c in tail_copies():
                act(c)

    def accumulate(first):
        a = a_ref[...]
        for c in range(D_MODEL // OUT_NC):
            cols = slice(c * OUT_NC, (c + 1) * OUT_NC)
            p = jnp.dot(a, w_ref[:, cols], preferred_element_type=F32)
            xo_ref[:, cols] = p if first else xo_ref[:, cols] + p

    @pl.when(k == 0)
    def _():
        residual_copies(lambda c: c.start())
        accumulate(True)

    @pl.when(k > 0)
    def _():
        accumulate(False)

    @pl.when(k == nk - 1)
    def _():
        residual_copies(lambda c: c.wait())

        def rows_body(i, carry):
            rows = pl.ds(pl.multiple_of(i * OUT_RC, OUT_RC), OUT_RC)
            xn = xbuf[rows, :] + alpha * _rms(xo_ref[rows, :], gp_ref[...])
            xo_ref[rows, :] = xn
            if not final:
                o2_ref[rows, :] = _rms(xn, gn_ref[...]).astype(BF16)
            return carry

        lax.fori_loop(0, TM_OUT // OUT_RC, rows_body, 0)
        if final:
            @pl.when(m == LAST_TILE)
            def _():
                o2_ref[...] = xo_ref[pl.ds(TAIL_PROMPT_ROWS, N_SAMPLE), :]


def _proj_out(a, w, x, gains, gi_post, gi_next, alpha, x_sample=None):
    kdim = a.shape[1]
    tk = next(t for t in TK_CHOICES if kdim % t == 0)
    nk = kdim // tk
    split_x = x_sample is not None
    final = gi_next is None
    kern = functools.partial(_proj_out_kernel, alpha=alpha, nk=nk, split_x=split_x, final=final)
    gain = lambda gi: pl.BlockSpec((None, 1, D_MODEL), lambda m, k: (gi, 0, 0))
    in_specs = [pl.BlockSpec((TM_OUT, tk), lambda m, k: (m, k)),
                pl.BlockSpec((tk, D_MODEL), lambda m, k: (k, 0)),
                pl.BlockSpec(memory_space=pl.ANY)]
    args = [a, w, x]
    if split_x:
        in_specs.append(pl.BlockSpec(memory_space=pl.ANY))
        args.append(x_sample)
    in_specs.append(gain(gi_post))
    args.append(gains)
    row_tile = pl.BlockSpec((TM_OUT, D_MODEL), lambda m, k: (m, 0))
    if final:
        out_specs = [row_tile, pl.BlockSpec((N_SAMPLE, D_MODEL), lambda m, k: (0, 0))]
        out_shape = [jax.ShapeDtypeStruct((N_PROMPT, D_MODEL), F32),
                     jax.ShapeDtypeStruct((N_SAMPLE, D_MODEL), F32)]
    else:
        in_specs.append(gain(gi_next))
        args.append(gains)
        out_specs = [row_tile, row_tile]
        out_shape = [jax.ShapeDtypeStruct((N_ROWS, D_MODEL), F32),
                     jax.ShapeDtypeStruct((N_ROWS, D_MODEL), BF16)]
    return pl.pallas_call(
        kern,
        grid=(N_ROWS // TM_OUT, nk),
        in_specs=in_specs,
        out_specs=out_specs,
        out_shape=out_shape,
        scratch_shapes=[pltpu.VMEM((TM_OUT, D_MODEL), F32), pltpu.SemaphoreType.DMA((2,))],
        compiler_params=_params(("arbitrary", "arbitrary")),
        name="proj_out",
    )(*args)


def _log_sigmoid(x):
    return jnp.minimum(x, 0.0) - jnp.log(1.0 + jnp.exp(-jnp.abs(x)))


def _hgrn_lower_bound(gam_ref, layer):
    gam = gam_ref[...]
    mx = jnp.max(gam, axis=0, keepdims=True)
    e = jnp.exp(gam - mx)
    probs = e / jnp.sum(e, axis=0, keepdims=True)
    acc = probs[0:1, :]
    for j in range(1, layer + 1):
        acc = acc + probs[j:j + 1, :]
    return acc - probs[0:1, :]


def _hgrn_gate(z, lb):
    e = jnp.exp(-jnp.abs(z))
    r = 1.0 / (1.0 + e)
    er = e * r
    pos = z >= 0.0
    sig = jnp.where(pos, r, er)
    sig_neg = jnp.where(pos, er, r)
    return (1.0 - lb) * sig_neg, jnp.log(lb + (1.0 - lb) * sig)


def _chunk_levels(c):
    levels, m = [], 2
    while m <= c:
        levels.append(m)
        m *= 2
    return levels


def _midpoint_distance(cum, g, m):
    c = cum.shape[0]
    hm = m // 2
    if m == 2:
        odd = jnp.bitwise_and(lax.broadcasted_iota(jnp.int32, cum.shape, 0), 1) == 1
        return jnp.where(odd, jnp.abs(g), 0.0)
    if m < 2 * SUBLANES:
        cum3 = cum.reshape(c // SUBLANES, SUBLANES, LANES)
        sub = lax.broadcasted_iota(jnp.int32, cum3.shape, 1)
        mid = None
        for b0 in range(0, SUBLANES, m):
            cand = jnp.broadcast_to(cum3[:, b0 + hm - 1:b0 + hm, :], cum3.shape)
            mid = cand if mid is None else jnp.where(sub >= b0, cand, mid)
        return jnp.abs(cum3 - mid).reshape(c, LANES)
    mid = cum.reshape(c // m, m, LANES)[:, hm - 1:hm, :]
    mid = jnp.broadcast_to(mid, (c // m, m, LANES)).reshape(c, LANES)
    return jnp.abs(cum - mid)


def _pair_levels(c):
    row = lax.broadcasted_iota(jnp.int32, (c, c), 0)
    col = lax.broadcasted_iota(jnp.int32, (c, c), 1)
    diff = jnp.bitwise_xor(row, col)
    lvl = jnp.zeros((c, c), jnp.int32)
    for j in range(c.bit_length() - 1):
        lvl = lvl + (diff >= (1 << j)).astype(jnp.int32)
    return jnp.where(col > row, -1, lvl)


def _chunk_scan(qs, ks, vs, gs, s_refs, lvl):
    heads = range(len(qs))
    c = qs[0].shape[0]
    nt = (((1,), (1,)), ((), ()))
    eye = lvl == 0
    tri = (lvl >= 0).astype(BF16)
    g_hi = [g.astype(BF16) for g in gs]
    g_lo = [(gs[i] - g_hi[i].astype(F32)).astype(BF16) for i in heads]
    cums = [jnp.dot(tri, g_hi[i], preferred_element_type=F32)
            + jnp.dot(tri, g_lo[i], preferred_element_type=F32) for i in heads]

    s_old = [s_refs[i][...] for i in heads]
    vb = [v.astype(BF16) for v in vs]
    o = [jnp.dot((qs[i] * jnp.exp(cums[i])).astype(BF16), s_old[i].astype(BF16),
                 preferred_element_type=F32) for i in heads]

    for i in heads:
        last = cums[i][c - 1:c, :]
        kt_t = jnp.transpose(ks[i] * jnp.exp(last - cums[i])).astype(BF16)
        dec = jnp.broadcast_to(jnp.exp(last), (LANES, LANES))
        dec_col = jnp.sum(jnp.where(eye[:LANES, :LANES], dec, 0.0), axis=1, keepdims=True)
        s_refs[i][...] = s_old[i] * dec_col + jnp.dot(kt_t, vb[i], preferred_element_type=F32)

    rowl = lax.broadcasted_iota(jnp.int32, (c, LANES), 0)
    a = [jnp.where(eye, lax.dot_general(qs[i].astype(BF16), ks[i].astype(BF16), nt,
                                        preferred_element_type=F32), 0.0) for i in heads]
    for m in _chunk_levels(c):
        second = jnp.bitwise_and(rowl, m // 2) != 0
        mask = lvl == m.bit_length() - 1
        x = [(jnp.where(second, qs[i], ks[i])
              * jnp.exp(-_midpoint_distance(cums[i], gs[i], m))).astype(BF16) for i in heads]
        am = [lax.dot_general(x[i], x[i], nt, preferred_element_type=F32) for i in heads]
        a = [jnp.where(mask, am[i], a[i]) for i in heads]
    return o, [x.astype(BF16) for x in a], vb


def _head_cols(ref, rows, i, width):
    return ref[rows, i * width:(i + 1) * width]


def _skewed_chunk_loop(front, finish, pend_refs):
    oi_s, pa_s, vb_s = pend_refs
    heads = range(oi_s.shape[0])

    def stash(parts):
        ois, pas, vbs = parts
        for i in heads:
            oi_s[i] = ois[i]
            pa_s[i] = pas[i]
            vb_s[i] = vbs[i]

    def issue():
        return [oi_s[i] + jnp.dot(pa_s[i], vb_s[i], preferred_element_type=F32) for i in heads]

    stash(front(0))

    def body(ci, carry):
        os = issue()
        stash(front(ci))
        finish(ci - 1, os)
        return carry

    n = SEQ // CHUNK
    lax.fori_loop(1, n, body, 0, unroll=SCAN_UNROLL)
    finish(n - 1, issue())


def _chunk_rows(ci):
    return pl.ds(pl.multiple_of(ci * CHUNK, CHUNK), CHUNK)


def _gla_prompt_kernel(q_ref, k_ref, v_ref, r_ref, a_ref, wup_ref, bg_ref, gn_ref, o_ref, s_ref,
                       oi_s, pa_s, vb_s):
    heads = range(SCAN_HEADS)
    s_ref[...] = jnp.zeros_like(s_ref)
    lvl = _pair_levels(CHUNK)

    def front(ci):
        rows = _chunk_rows(ci)
        a = a_ref[rows, :].astype(BF16)
        pre = jnp.dot(a, wup_ref[...].astype(BF16), preferred_element_type=F32) + bg_ref[...]
        gs = [_log_sigmoid(pre[:, i * GLA_DK:(i + 1) * GLA_DK]) * (1.0 / GLA_TAU) for i in heads]
        qs = [_head_cols(q_ref, rows, i, GLA_DK) * (GLA_DK ** -0.5) for i in heads]
        ks = [_head_cols(k_ref, rows, i, GLA_DK) for i in heads]
        vs = [_head_cols(v_ref, rows, i, GLA_DV) for i in heads]
        return _chunk_scan(qs, ks, vs, gs, [s_ref.at[i] for i in heads], lvl)

    def finish(ci, os):
        rows = _chunk_rows(ci)
        for i in heads:
            gate = _silu(_head_cols(r_ref, rows, i, GLA_DV))
            o_ref[rows, i * GLA_DV:(i + 1) * GLA_DV] = (_rms(os[i], gn_ref[...]) * gate).astype(BF16)

    _skewed_chunk_loop(front, finish, (oi_s, pa_s, vb_s))


def _pending_scratch(heads, dv):
    return [pltpu.VMEM((heads, CHUNK, dv), F32),
            pltpu.VMEM((heads, CHUNK, CHUNK), BF16),
            pltpu.VMEM((heads, CHUNK, dv), BF16)]


def _gla_prompt(proj, alr, wup, bg, gn, merged, layer):
    p = SCAN_HEADS
    dk, dv = p * GLA_DK, p * GLA_DV
    cq, ck, cv, cr = (COL_Q // dk, COL_K // dk, COL_V // dv, COL_R // dv)
    return pl.pallas_call(
        _skip_carries(_gla_prompt_kernel, 8, 1),
        grid=(BATCH, GLA_HEADS // p),
        in_specs=[pl.BlockSpec((SEQ, dk), lambda b, h: (b, cq + h)),
                  pl.BlockSpec((SEQ, dk), lambda b, h: (b, ck + h)),
                  pl.BlockSpec((SEQ, dv), lambda b, h: (b, cv + h)),
                  pl.BlockSpec((SEQ, dv), lambda b, h: (b, cr + h)),
                  pl.BlockSpec((SEQ, LANES), lambda b, h: (b, 0)),
                  pl.BlockSpec((None, LANES, dk), lambda b, h: (layer, 0, h)),
                  pl.BlockSpec((None, 1, dk), lambda b, h: (layer, 0, h)),
                  pl.BlockSpec((None, 1, GLA_DV), lambda b, h: (layer, 0, 0)),
                  pl.BlockSpec(memory_space=pl.ANY)],
        out_specs=[pl.BlockSpec((SEQ, dv), lambda b, h: (b, h)),
                   pl.BlockSpec((None, p, GLA_DK, GLA_DV), lambda b, h: (b, h, 0, 0))],
        out_shape=[jax.ShapeDtypeStruct((N_ROWS, D_MODEL), BF16),
                   jax.ShapeDtypeStruct((BATCH, GLA_HEADS, GLA_DK, GLA_DV), F32)],
        scratch_shapes=_pending_scratch(SCAN_HEADS, GLA_DV),
        input_output_aliases={8: 0},
        compiler_params=_params(("arbitrary", "arbitrary")),
        name="gla_prompt",
    )(proj, proj, proj, proj, alr, wup, bg, gn, merged)


def _hgrn_prompt_kernel(hq_ref, hf_ref, hi_ref, hg_ref, gam_ref, hn_ref, o_ref, s_ref,
                        oi_s, pa_s, vb_s, *, layer):
    heads = range(HGRN_SCAN_HEADS)
    s_ref[...] = jnp.zeros_like(s_ref)
    lvl = _pair_levels(CHUNK)
    lb = _hgrn_lower_bound(gam_ref, layer)

    def front(ci):
        rows = _chunk_rows(ci)
        qs = [_silu(_head_cols(hq_ref, rows, i, HGRN_DK)) * (HGRN_DK ** -0.5) for i in heads]
        kg = [_hgrn_gate(_head_cols(hf_ref, rows, i, HGRN_DK),
                         lb[:, i * HGRN_DK:(i + 1) * HGRN_DK]) for i in heads]
        vs = [_head_cols(hi_ref, rows, i, HGRN_DV) for i in heads]
        return _chunk_scan(qs, [x[0] for x in kg], vs, [x[1] for x in kg],
                           [s_ref.at[i] for i in heads], lvl)

    def finish(ci, os):
        rows = _chunk_rows(ci)
        for i in heads:
            gate = _silu(_head_cols(hg_ref, rows, i, HGRN_DV))
            o_ref[rows, i * HGRN_DV:(i + 1) * HGRN_DV] = (
                _rms(os[i], hn_ref[...]) * gate).astype(BF16)

    _skewed_chunk_loop(front, finish, (oi_s, pa_s, vb_s))


def _hgrn_prompt(proj, gamma, hn, merged, layer):
    p = HGRN_SCAN_HEADS
    dk, dv = p * HGRN_DK, p * HGRN_DV
    cq, cf, ci, cg = (COL_HQ // dk, COL_HF // dk, COL_HI // dv, COL_HG // dv)
    co = GLA_WIDTH // dv
    kern = _skip_carries(functools.partial(_hgrn_prompt_kernel, layer=layer), 6, 1)
    return pl.pallas_call(
        kern,
        grid=(BATCH, HGRN_HEADS // p),
        in_specs=[pl.BlockSpec((SEQ, dk), lambda b, h: (b, cq + h)),
                  pl.BlockSpec((SEQ, dk), lambda b, h: (b, cf + h)),
                  pl.BlockSpec((SEQ, dv), lambda b, h: (b, ci + h)),
                  pl.BlockSpec((SEQ, dv), lambda b, h: (b, cg + h)),
                  pl.BlockSpec((DEPTH, dk), lambda b, h: (0, h)),
                  pl.BlockSpec((None, 1, HGRN_DV), lambda b, h: (layer, 0, 0)),
                  pl.BlockSpec(memory_space=pl.ANY)],
        out_specs=[pl.BlockSpec((SEQ, dv), lambda b, h: (b, co + h)),
                   pl.BlockSpec((None, p, HGRN_DK, HGRN_DV), lambda b, h: (b, h, 0, 0))],
        out_shape=[jax.ShapeDtypeStruct((N_ROWS, D_MODEL), BF16),
                   jax.ShapeDtypeStruct((BATCH, HGRN_HEADS, HGRN_DK, HGRN_DV), F32)],
        scratch_shapes=_pending_scratch(HGRN_SCAN_HEADS, HGRN_DV),
        input_output_aliases={6: 0},
        compiler_params=_params(("arbitrary", "arbitrary")),
        name="hgrn_prompt",
    )(proj, proj, proj, proj, gamma, hn, merged)


def _sample_setup(q, k, v, g, oin_s, qh_s, kt_s, dt_s):
    cum = [g[0]]
    for t in range(1, DEC_SEQ):
        cum.append(cum[t - 1] + g[t])
    for t in range(DEC_SEQ):
        o = jnp.zeros_like(v[0])
        for s in range(t + 1):
            w = q[t] * k[s] if s == t else q[t] * (k[s] * jnp.exp(cum[t] - cum[s]))
            o = o + jnp.sum(w, axis=-1, keepdims=True) * v[s]
        oin_s[t] = o
    last = cum[DEC_SEQ - 1]
    for t in range(DEC_SEQ):
        qh_s[t] = q[t] * jnp.exp(cum[t])
        kt = jnp.transpose(k[t] * jnp.exp(last - cum[t]))
        for blk in range(NBLK):
            kt_s[blk, :, t * SAMPLE_NB:(t + 1) * SAMPLE_NB] = (
                kt[:, blk * SAMPLE_NB:(blk + 1) * SAMPLE_NB])
    dt = jnp.transpose(jnp.exp(last))
    for blk in range(NBLK):
        dt_s[blk] = dt[:, blk * SAMPLE_NB:(blk + 1) * SAMPLE_NB]


def _sample_rows(bb, t):
    return pl.ds(pl.multiple_of(t * DEC_BATCH + bb * SAMPLE_NB, SAMPLE_NB), SAMPLE_NB)


def _sample_step(bb, i, v_ref, sin_ref, sout_ref, qh_s, kt_s, dt_s):
    nb = SAMPLE_NB
    nrow = DEC_SEQ * nb
    dv = sin_ref.shape[-1]
    r0 = pl.multiple_of(bb * nb, nb)
    qh = jnp.concatenate([qh_s[t, pl.ds(r0, nb), :] for t in range(DEC_SEQ)], axis=0)
    vv = jnp.concatenate([v_ref[_sample_rows(bb, t), i * dv:(i + 1) * dv]
                          for t in range(DEC_SEQ)], axis=0)
    rj = jnp.bitwise_and(lax.broadcasted_iota(jnp.int32, (nrow, nb * LANES), 0), nb - 1)
    cb = lax.shift_right_logical(lax.broadcasted_iota(jnp.int32, (nrow, nb * LANES), 1), 7)
    q_bd = jnp.where(rj == cb, jnp.concatenate([qh] * nb, axis=1), 0.0).astype(BF16)
    s_old = sin_ref[:, i]
    s_stack = s_old.reshape(nb * LANES, dv)
    o = jnp.dot(q_bd, s_stack.astype(BF16), preferred_element_type=F32)

    kt64 = kt_s[bb]
    lane_j = jnp.bitwise_and(lax.broadcasted_iota(jnp.int32, kt64.shape, 1), nb - 1)
    kt_bd = jnp.concatenate([jnp.where(lane_j == j, kt64, 0.0) for j in range(nb)],
                            axis=0).astype(BF16)
    upd = jnp.dot(kt_bd, vv.astype(BF16), preferred_element_type=F32)
    dcols = dt_s[bb]
    for j in range(nb):
        sout_ref[j, i] = s_old[j] * dcols[:, j:j + 1] + upd[j * LANES:(j + 1) * LANES, :]
    return o


def _sample_emit(bb, i, o, oin_s, gate_ref, norm_ref, o_ref):
    nb = SAMPLE_NB
    dv = o.shape[-1]
    cols = slice(i * dv, (i + 1) * dv)
    for t in range(DEC_SEQ):
        rows = _sample_rows(bb, t)
        ot = o[t * nb:(t + 1) * nb, :] + oin_s[t, pl.ds(pl.multiple_of(bb * nb, nb), nb), :]
        o_ref[rows, cols] = (_rms(ot, norm_ref[...]) * _silu(gate_ref[rows, cols])).astype(BF16)


def _tiles(ref, i, width):
    return [ref[t * DEC_BATCH:(t + 1) * DEC_BATCH, i * width:(i + 1) * width]
            for t in range(DEC_SEQ)]


def _first_layer_state_out(sout_ref, layer):
    if layer != 0:
        return sout_ref
    for later in range(1, DEPTH):
        sout_ref[later] = jnp.zeros(sout_ref.shape[1:], F32)
    return sout_ref.at[0]


def _gla_sample_kernel(q_ref, k_ref, v_ref, r_ref, a_ref, wup_ref, bg_ref, gn_ref, sin_ref,
                       o_ref, sout_ref, oin_s, qh_s, kt_s, dt_s, *, layer):
    bb = pl.program_id(1)
    heads = range(GLA_SAMPLE_HEADS)

    @pl.when(bb == 0)
    def _():
        for i in heads:
            q = [x * (GLA_DK ** -0.5) for x in _tiles(q_ref, i, GLA_DK)]
            pre = [jnp.dot(x.astype(BF16),
                           wup_ref[:, i * GLA_DK:(i + 1) * GLA_DK].astype(BF16),
                           preferred_element_type=F32) + bg_ref[:, i * GLA_DK:(i + 1) * GLA_DK]
                   for x in _tiles(a_ref, 0, LANES)]
            g = [_log_sigmoid(x) * (1.0 / GLA_TAU) for x in pre]
            _sample_setup(q, _tiles(k_ref, i, GLA_DK), _tiles(v_ref, i, GLA_DV), g,
                          oin_s.at[i], qh_s.at[i], kt_s.at[i], dt_s.at[i])

    sout = _first_layer_state_out(sout_ref, layer)
    for i in heads:
        o = _sample_step(bb, i, v_ref, sin_ref, sout, qh_s.at[i], kt_s.at[i], dt_s.at[i])
        _sample_emit(bb, i, o, oin_s.at[i], r_ref, gn_ref, o_ref)


def _hgrn_sample_kernel(hq_ref, hf_ref, hi_ref, hg_ref, gam_ref, hn_ref, sin_ref,
                        o_ref, sout_ref, oin_s, qh_s, kt_s, dt_s, *, layer):
    bb = pl.program_id(1)
    heads = range(HGRN_SAMPLE_HEADS)

    @pl.when(bb == 0)
    def _():
        lb = _hgrn_lower_bound(gam_ref, layer)
        for i in heads:
            q = [_silu(x) * (HGRN_DK ** -0.5) for x in _tiles(hq_ref, i, HGRN_DK)]
            kg = [_hgrn_gate(x, lb[:, i * HGRN_DK:(i + 1) * HGRN_DK])
                  for x in _tiles(hf_ref, i, HGRN_DK)]
            _sample_setup(q, [x[0] for x in kg], _tiles(hi_ref, i, HGRN_DV), [x[1] for x in kg],
                          oin_s.at[i], qh_s.at[i], kt_s.at[i], dt_s.at[i])

    sout = _first_layer_state_out(sout_ref, layer)
    for i in heads:
        o = _sample_step(bb, i, hi_ref, sin_ref, sout, qh_s.at[i], kt_s.at[i], dt_s.at[i])
        _sample_emit(bb, i, o, oin_s.at[i], hg_ref, hn_ref, o_ref)


def _sample_scratch(nh, dv):
    return [pltpu.VMEM((nh, DEC_SEQ, DEC_BATCH, dv), F32),
            pltpu.VMEM((nh, DEC_SEQ, DEC_BATCH, LANES), F32),
            pltpu.VMEM((nh, NBLK, LANES, DEC_SEQ * SAMPLE_NB), F32),
            pltpu.VMEM((nh, NBLK, LANES, SAMPLE_NB), F32)]


def _sample_call(kern, n_in, in_specs, args, heads, nh, dv, col_out, state_in, merged,
                 state_prev, layer, name):
    st_spec = pl.BlockSpec((None, SAMPLE_NB, nh, LANES, dv), lambda h, bb: (layer, bb, h, 0, 0))
    st_out_spec = st_spec
    if state_prev is None:
        st_out_spec = pl.BlockSpec((DEPTH, SAMPLE_NB, nh, LANES, dv),
                                   lambda h, bb: (0, bb, h, 0, 0))
    carries = [merged] if state_prev is None else [merged, state_prev]
    in_specs = in_specs + [st_spec] + [pl.BlockSpec(memory_space=pl.ANY)] * len(carries)
    args = args + [state_in] + carries
    aliases = {n_in + 1 + i: i for i in range(len(carries))}
    return pl.pallas_call(
        _skip_carries(kern, n_in + 1, len(carries)),
        grid=(heads // nh, NBLK),
        in_specs=in_specs,
        out_specs=[pl.BlockSpec((N_SAMPLE, nh * dv),
                                lambda h, bb: (SAMPLE_ROW_BLOCK, col_out + h)),
                   st_out_spec],
        out_shape=[jax.ShapeDtypeStruct((N_ROWS, D_MODEL), BF16),
                   jax.ShapeDtypeStruct(state_in.shape, F32)],
        scratch_shapes=_sample_scratch(nh, dv),
        input_output_aliases=aliases,
        compiler_params=_params(("arbitrary", "arbitrary")),
        name=name,
    )(*args)


def _sample_tile(width, col):
    return pl.BlockSpec((N_SAMPLE, width), lambda h, bb: (SAMPLE_ROW_BLOCK, col // width + h))


def _gla_sample(proj, alr, wup, bg, gn, state_in, merged, state_prev, layer):
    nh = GLA_SAMPLE_HEADS
    dk, dv = nh * GLA_DK, nh * GLA_DV
    in_specs = [_sample_tile(dk, COL_Q), _sample_tile(dk, COL_K),
                _sample_tile(dv, COL_V), _sample_tile(dv, COL_R),
                pl.BlockSpec((N_SAMPLE, LANES), lambda h, bb: (SAMPLE_ROW_BLOCK, 0)),
                pl.BlockSpec((None, LANES, dk), lambda h, bb: (layer, 0, h)),
                pl.BlockSpec((None, 1, dk), lambda h, bb: (layer, 0, h)),
                pl.BlockSpec((None, 1, GLA_DV), lambda h, bb: (layer, 0, 0))]
    args = [proj, proj, proj, proj, alr, wup, bg, gn]
    return _sample_call(functools.partial(_gla_sample_kernel, layer=layer), 8, in_specs, args,
                        GLA_HEADS, nh, GLA_DV, 0, state_in, merged, state_prev, layer,
                        "gla_sample")


def _hgrn_sample(proj, gamma, hn, state_in, merged, state_prev, layer):
    nh = HGRN_SAMPLE_HEADS
    dk, dv = nh * HGRN_DK, nh * HGRN_DV
    in_specs = [_sample_tile(dk, COL_HQ), _sample_tile(dk, COL_HF),
                _sample_tile(dv, COL_HI), _sample_tile(dv, COL_HG),
                pl.BlockSpec((DEPTH, dk), lambda h, bb: (0, h)),
                pl.BlockSpec((None, 1, HGRN_DV), lambda h, bb: (layer, 0, 0))]
    args = [proj, proj, proj, proj, gamma, hn]
    return _sample_call(functools.partial(_hgrn_sample_kernel, layer=layer), 6, in_specs, args,
                        HGRN_HEADS, nh, HGRN_DV, GLA_WIDTH // dv, state_in, merged, state_prev,
                        layer, "hgrn_sample")


def kernel(x_prompt, x_sample, state_gla, state_hgrn, norm_gains, ffn1_w_in, ffn1_w_out,
           ffn2_w_in, ffn2_w_out, mix_w_in, gla_w_gate_up, gla_b_gate, gla_norm, hgrn_gamma,
           hgrn_norm, mix_w_out):
    xp = x_prompt.reshape(N_PROMPT, D_MODEL)
    xs = jnp.transpose(x_sample, (1, 0, 2)).reshape(N_SAMPLE, D_MODEL)
    gains = norm_gains.reshape(DEPTH * 6, 1, D_MODEL)
    w_mix_t = jnp.swapaxes(mix_w_in, 1, 2)
    wup = jnp.pad(gla_w_gate_up, ((0, 0), (0, LANES - GLA_RANK), (0, 0)))
    bg = gla_b_gate.reshape(DEPTH, 1, GLA_HEADS * GLA_DK)
    gn = gla_norm.reshape(DEPTH, 1, GLA_DV)
    hn = hgrn_norm.reshape(DEPTH, 1, HGRN_DV)

    h = _rms_cast(xp, xs, gains, 0)
    x = None
    gla_p, hgrn_p = [], []
    st_gla, st_hgrn = None, None
    for l in range(DEPTH):
        base = 6 * l
        act, wo = _ffn_in(h, ffn1_w_in, ffn1_w_out, l)
        if l == 0:
            x, h = _proj_out(act, wo, xp, gains, base + 1, base + 2, 0.5, x_sample=xs)
        else:
            x, h = _proj_out(act, wo, x, gains, base + 1, base + 2, 0.5)
        proj, alr = _mix_in(h, w_mix_t, l)
        merged, wo_mix = _mix_aux(mix_w_out, l)
        merged, sg = _gla_prompt(proj, alr, wup, bg, gn, merged, l)
        merged, sh = _hgrn_prompt(proj, hgrn_gamma, hn, merged, l)
        merged, st_gla = _gla_sample(proj, alr, wup, bg, gn, state_gla, merged, st_gla, l)
        merged, st_hgrn = _hgrn_sample(proj, hgrn_gamma, hn, state_hgrn, merged, st_hgrn, l)
        gla_p.append(sg)
        hgrn_p.append(sh)
        x, h = _proj_out(merged, wo_mix, x, gains, base + 3, base + 4, 1.0)
        act, wo = _ffn_in(h, ffn2_w_in, ffn2_w_out, l)
        if l + 1 < DEPTH:
            x, h = _proj_out(act, wo, x, gains, base + 5, base + 6, 0.5)
        else:
            y_prompt_rows, y_sample_rows = _proj_out(act, wo, x, gains, base + 5, None, 0.5)

    y_prompt = y_prompt_rows.reshape(BATCH, SEQ, D_MODEL)
    y_sample = jnp.transpose(y_sample_rows.reshape(DEC_SEQ, DEC_BATCH, D_MODEL), (1, 0, 2))
    return (y_prompt, y_sample, jnp.stack(gla_p), jnp.stack(hgrn_p), st_gla, st_hgrn)
```

```python
import functools

import jax
import jax.numpy as jnp
from jax import lax
from jax.experimental import pallas as pl
from jax.experimental.pallas import tpu as pltpu

F32 = jnp.float32
BF16 = jnp.bfloat16

D_MODEL = 2048
BATCH = 4
SEQ = 2048
DEPTH = 2
DEC_BATCH = 128
DEC_SEQ = 4
GLA_HEADS = 4
GLA_DK = 128
GLA_DV = 256
GLA_WIDTH = GLA_HEADS * GLA_DV
GLA_RANK = 16
GLA_TAU = 16.0
HGRN_HEADS = 8
HGRN_DK = 128
HGRN_DV = 128
HGRN_WIDTH = HGRN_HEADS * HGRN_DV
D_FF = 5632
EPS = 1e-6

N_PROMPT = BATCH * SEQ
N_SAMPLE = DEC_BATCH * DEC_SEQ
N_ROWS = N_PROMPT + N_SAMPLE
SAMPLE_ROW_BLOCK = N_PROMPT // N_SAMPLE

LANES = 128
SUBLANES = 8
VMEM_LIMIT = 60 * 1024 * 1024

COL_Q = 0
COL_K = 512
COL_V = 1024
COL_R = 2048
COL_HQ = 3072
COL_HF = 4096
COL_HI = 5120
COL_HG = 6144
PROJ_W = 7168
GATE_COL = 2048
MIX_ALIGNED_TILES = GATE_COL // 1024

TM_FFN = 1088
TM_IN = 1088
TN = 512
MIX_TN = 1024
FFN_NC = 256
TM_OUT = 1088
TK_CHOICES = (1408, 1024, 512)
OUT_NC = 512
MIX_WO_ROWS = D_MODEL // (N_ROWS // TM_IN)
OUT_RC = 272
LAST_TILE = N_ROWS // TM_OUT - 1
TAIL_PROMPT_ROWS = N_PROMPT - LAST_TILE * TM_OUT
TM_ROW = 512
CHUNK = 256
SCAN_HEADS = 2
HGRN_SCAN_HEADS = 4
SCAN_UNROLL = 1
SAMPLE_NB = 16
GLA_SAMPLE_HEADS = 2
HGRN_SAMPLE_HEADS = 4
NBLK = DEC_BATCH // SAMPLE_NB


def _params(sem):
    return pltpu.CompilerParams(dimension_semantics=sem, vmem_limit_bytes=VMEM_LIMIT)


def _sigmoid(x):
    return 1.0 / (1.0 + jnp.exp(-x))


def _silu(x):
    return x * _sigmoid(x)


def _rms(y, g):
    return y * lax.rsqrt(jnp.mean(y * y, axis=-1, keepdims=True) + EPS) * g


def _skip_carries(kern, n_in, n_carry):
    def wrapped(*refs):
        return kern(*refs[:n_in], *refs[n_in + n_carry:])
    return wrapped


def _rms_cast_kernel(xp_ref, xs_ref, g_ref, h_ref):
    prompt_tiles = N_PROMPT // TM_ROW

    @pl.when(pl.program_id(0) < prompt_tiles)
    def _():
        h_ref[...] = _rms(xp_ref[...], g_ref[...]).astype(BF16)

    @pl.when(pl.program_id(0) >= prompt_tiles)
    def _():
        h_ref[...] = _rms(xs_ref[...], g_ref[...]).astype(BF16)


def _rms_cast(x_prompt, x_sample, gains, gi):
    prompt_tiles = N_PROMPT // TM_ROW
    return pl.pallas_call(
        _rms_cast_kernel,
        grid=(N_ROWS // TM_ROW,),
        in_specs=[pl.BlockSpec((TM_ROW, D_MODEL), lambda m: (jnp.minimum(m, prompt_tiles - 1), 0)),
                  pl.BlockSpec((TM_ROW, D_MODEL), lambda m: (jnp.maximum(m - prompt_tiles, 0), 0)),
                  pl.BlockSpec((None, 1, D_MODEL), lambda m: (gi, 0, 0))],
        out_specs=pl.BlockSpec((TM_ROW, D_MODEL), lambda m: (m, 0)),
        out_shape=jax.ShapeDtypeStruct((N_ROWS, D_MODEL), BF16),
        compiler_params=_params(("arbitrary",)),
        name="rms_cast",
    )(x_prompt, x_sample, gains)


def _ffn_in_kernel(h_ref, wg_ref, wu_ref, wo_ref, o_ref, wob_ref, wg_s, wu_s):
    @pl.when(pl.program_id(1) == 0)
    def _():
        wg_s[...] = wg_ref[...].astype(BF16)
        wu_s[...] = wu_ref[...].astype(BF16)

    wob_ref[...] = wo_ref[...].astype(BF16)
    h = h_ref[...]
    for c in range(TN // FFN_NC):
        cols = slice(c * FFN_NC, (c + 1) * FFN_NC)
        g = jnp.dot(h, wg_s[:, cols], preferred_element_type=F32)
        u = jnp.dot(h, wu_s[:, cols], preferred_element_type=F32)
        o_ref[:, cols] = (g * _sigmoid(g) * u).astype(BF16)


def _ffn_in(h, w_in, w_out, layer):
    nt = D_FF // TN
    nm = N_ROWS // TM_FFN
    slab = D_FF // (nt * nm)
    return pl.pallas_call(
        _ffn_in_kernel,
        grid=(nt, nm),
        in_specs=[pl.BlockSpec((TM_FFN, D_MODEL), lambda n, m: (m, 0)),
                  pl.BlockSpec((None, D_MODEL, TN), lambda n, m: (layer, 0, n)),
                  pl.BlockSpec((None, D_MODEL, TN), lambda n, m: (layer, 0, n + nt)),
                  pl.BlockSpec((None, slab, D_MODEL), lambda n, m: (layer, n * nm + m, 0))],
        out_specs=[pl.BlockSpec((TM_FFN, TN), lambda n, m: (m, n)),
                   pl.BlockSpec((slab, D_MODEL), lambda n, m: (n * nm + m, 0))],
        out_shape=[jax.ShapeDtypeStruct((N_ROWS, D_FF), BF16),
                   jax.ShapeDtypeStruct((D_FF, D_MODEL), BF16)],
        scratch_shapes=[pltpu.VMEM((D_MODEL, TN), BF16), pltpu.VMEM((D_MODEL, TN), BF16)],
        compiler_params=_params(("arbitrary", "arbitrary")),
        name="ffn_in",
    )(h, w_in, w_in, w_out)


def _mix_in_kernel(h_ref, wa_ref, wb_ref, o_ref, a_ref, w_s, wg_s):
    nt = (((1,), (1,)), ((), ()))
    n = pl.program_id(0)
    first_row_tile = pl.program_id(1) == 0
    gate_tile = n == MIX_ALIGNED_TILES
    kept = MIX_TN - GLA_RANK

    @pl.when(jnp.logical_and(first_row_tile, n < MIX_ALIGNED_TILES))
    def _():
        w_s[...] = wa_ref[...].astype(BF16)

    @pl.when(jnp.logical_and(first_row_tile, n >= MIX_ALIGNED_TILES))
    def _():
        w_s[0:kept, :] = wa_ref[GLA_RANK:MIX_TN, :].astype(BF16)
        w_s[kept:MIX_TN, :] = wb_ref[0:GLA_RANK, :].astype(BF16)

    @pl.when(jnp.logical_and(first_row_tile, gate_tile))
    def _():
        wg_s[...] = jnp.zeros_like(wg_s)
        wg_s[0:GLA_RANK, :] = wa_ref[0:GLA_RANK, :].astype(BF16)

    h = h_ref[...]
    o_ref[...] = lax.dot_general(h, w_s[...], nt, preferred_element_type=F32)

    @pl.when(gate_tile)
    def _():
        a_ref[...] = lax.dot_general(h, wg_s[...], nt, preferred_element_type=F32)


def _mix_in(h, w_in_t, layer):
    last = N_ROWS // TM_IN - 1
    slabs_per_tile = MIX_TN // LANES

    def gate_block(n, m):
        return jnp.where(n == MIX_ALIGNED_TILES, m, jnp.where(n < MIX_ALIGNED_TILES, 0, last))

    return pl.pallas_call(
        _mix_in_kernel,
        grid=(PROJ_W // MIX_TN, N_ROWS // TM_IN),
        in_specs=[pl.BlockSpec((TM_IN, D_MODEL), lambda n, m: (m, 0)),
                  pl.BlockSpec((None, MIX_TN, D_MODEL), lambda n, m: (layer, n, 0)),
                  pl.BlockSpec((None, LANES, D_MODEL),
                               lambda n, m: (layer, (n + 1) * slabs_per_tile, 0))],
        out_specs=[pl.BlockSpec((TM_IN, MIX_TN), lambda n, m: (m, n)),
                   pl.BlockSpec((TM_IN, LANES), lambda n, m: (gate_block(n, m), 0))],
        out_shape=[jax.ShapeDtypeStruct((N_ROWS, PROJ_W), F32),
                   jax.ShapeDtypeStruct((N_ROWS, LANES), F32)],
        scratch_shapes=[pltpu.VMEM((MIX_TN, D_MODEL), BF16), pltpu.VMEM((LANES, D_MODEL), BF16)],
        compiler_params=_params(("arbitrary", "arbitrary")),
        name="mix_in",
    )(h, w_in_t, w_in_t)


def _mix_aux_kernel(wo_ref, z_ref, wob_ref):
    z_ref[...] = jnp.zeros_like(z_ref)
    wob_ref[...] = wo_ref[...].astype(BF16)


def _mix_aux(w_out, layer):
    return pl.pallas_call(
        _mix_aux_kernel,
        grid=(N_ROWS // TM_IN,),
        in_specs=[pl.BlockSpec((None, MIX_WO_ROWS, D_MODEL), lambda m: (layer, m, 0))],
        out_specs=[pl.BlockSpec((TM_IN, D_MODEL), lambda m: (m, 0)),
                   pl.BlockSpec((MIX_WO_ROWS, D_MODEL), lambda m: (m, 0))],
        out_shape=[jax.ShapeDtypeStruct((N_ROWS, D_MODEL), BF16),
                   jax.ShapeDtypeStruct((D_MODEL, D_MODEL), BF16)],
        compiler_params=_params(("arbitrary",)),
        name="mix_aux",
    )(w_out)


def _proj_out_kernel(*refs, alpha, nk, split_x, final):
    refs = list(refs)
    a_ref, w_ref, x_hbm = refs[:3]
    xs_hbm = refs[3] if split_x else None
    gp_ref = refs[3 + split_x]
    gn_ref = None if final else refs[4 + split_x]
    xo_ref, o2_ref, xbuf, sem = refs[-4:]
    m = pl.program_id(0)
    k = pl.program_id(1)

    def tile_copy():
        rows = pl.ds(pl.multiple_of(m * TM_OUT, TM_OUT), TM_OUT)
        return pltpu.make_async_copy(x_hbm.at[rows, :], xbuf, sem.at[0])

    def tail_copies():
        head = pltpu.make_async_copy(x_hbm.at[pl.ds(LAST_TILE * TM_OUT, TAIL_PROMPT_ROWS), :],
                                     xbuf.at[pl.ds(0, TAIL_PROMPT_ROWS), :], sem.at[0])
        tail = pltpu.make_async_copy(xs_hbm, xbuf.at[pl.ds(TAIL_PROMPT_ROWS, N_SAMPLE), :],
                                     sem.at[1])
        return head, tail

    def residual_copies(act):
        if not split_x:
            act(tile_copy())
            return

        @pl.when(m < LAST_TILE)
        def _():
            act(tile_copy())

        @pl.when(m == LAST_TILE)
        def _():
            for c in tail_copies():
                act(c)

    def accumulate(first):
        a = a_ref[...]
        for c in range(D_MODEL // OUT_NC):
            cols = slice(c * OUT_NC, (c + 1) * OUT_NC)
            p = jnp.dot(a, w_ref[:, cols], preferred_element_type=F32)
            xo_ref[:, cols] = p if first else xo_ref[:, cols] + p

    @pl.when(k == 0)
    def _():
        residual_copies(lambda c: c.start())
        accumulate(True)

    @pl.when(k > 0)
    def _():
        accumulate(False)

    @pl.when(k == nk - 1)
    def _():
        residual_copies(lambda c: c.wait())

        def rows_body(i, carry):
            rows = pl.ds(pl.multiple_of(i * OUT_RC, OUT_RC), OUT_RC)
            xn = xbuf[rows, :] + alpha * _rms(xo_ref[rows, :], gp_ref[...])
            xo_ref[rows, :] = xn
            if not final:
                o2_ref[rows, :] = _rms(xn, gn_ref[...]).astype(BF16)
            return carry

        lax.fori_loop(0, TM_OUT // OUT_RC, rows_body, 0)
        if final:
            @pl.when(m == LAST_TILE)
            def _():
                o2_ref[...] = xo_ref[pl.ds(TAIL_PROMPT_ROWS, N_SAMPLE), :]


def _proj_out(a, w, x, gains, gi_post, gi_next, alpha, x_sample=None):
    kdim = a.shape[1]
    tk = next(t for t in TK_CHOICES if kdim % t == 0)
    nk = kdim // tk
    split_x = x_sample is not None
    final = gi_next is None
    kern = functools.partial(_proj_out_kernel, alpha=alpha, nk=nk, split_x=split_x, final=final)
    gain = lambda gi: pl.BlockSpec((None, 1, D_MODEL), lambda m, k: (gi, 0, 0))
    in_specs = [pl.BlockSpec((TM_OUT, tk), lambda m, k: (m, k)),
                pl.BlockSpec((tk, D_MODEL), lambda m, k: (k, 0)),
                pl.BlockSpec(memory_space=pl.ANY)]
    args = [a, w, x]
    if split_x:
        in_specs.append(pl.BlockSpec(memory_space=pl.ANY))
        args.append(x_sample)
    in_specs.append(gain(gi_post))
    args.append(gains)
    row_tile = pl.BlockSpec((TM_OUT, D_MODEL), lambda m, k: (m, 0))
    if final:
        out_specs = [row_tile, pl.BlockSpec((N_SAMPLE, D_MODEL), lambda m, k: (0, 0))]
        out_shape = [jax.ShapeDtypeStruct((N_PROMPT, D_MODEL), F32),
                     jax.ShapeDtypeStruct((N_SAMPLE, D_MODEL), F32)]
    else:
        in_specs.append(gain(gi_next))
        args.append(gains)
        out_specs = [row_tile, row_tile]
        out_shape = [jax.ShapeDtypeStruct((N_ROWS, D_MODEL), F32),
                     jax.ShapeDtypeStruct((N_ROWS, D_MODEL), BF16)]
    return pl.pallas_call(
        kern,
        grid=(N_ROWS // TM_OUT, nk),
        in_specs=in_specs,
        out_specs=out_specs,
        out_shape=out_shape,
        scratch_shapes=[pltpu.VMEM((TM_OUT, D_MODEL), F32), pltpu.SemaphoreType.DMA((2,))],
        compiler_params=_params(("arbitrary", "arbitrary")),
        name="proj_out",
    )(*args)


def _log_sigmoid(x):
    return jnp.minimum(x, 0.0) - jnp.log(1.0 + jnp.exp(-jnp.abs(x)))


def _hgrn_lower_bound(gam_ref, layer):
    gam = gam_ref[...]
    mx = jnp.max(gam, axis=0, keepdims=True)
    e = jnp.exp(gam - mx)
    probs = e / jnp.sum(e, axis=0, keepdims=True)
    acc = probs[0:1, :]
    for j in range(1, layer + 1):
        acc = acc + probs[j:j + 1, :]
    return acc - probs[0:1, :]


def _hgrn_gate(z, lb):
    e = jnp.exp(-jnp.abs(z))
    r = 1.0 / (1.0 + e)
    er = e * r
    pos = z >= 0.0
    sig = jnp.where(pos, r, er)
    sig_neg = jnp.where(pos, er, r)
    return (1.0 - lb) * sig_neg, jnp.log(lb + (1.0 - lb) * sig)


def _chunk_levels(c):
    levels, m = [], 2
    while m <= c:
        levels.append(m)
        m *= 2
    return levels


def _midpoint_distance(cum, g, m):
    c = cum.shape[0]
    hm = m // 2
    if m == 2:
        odd = jnp.bitwise_and(lax.broadcasted_iota(jnp.int32, cum.shape, 0), 1) == 1
        return jnp.where(odd, jnp.abs(g), 0.0)
    if m < 2 * SUBLANES:
        cum3 = cum.reshape(c // SUBLANES, SUBLANES, LANES)
        sub = lax.broadcasted_iota(jnp.int32, cum3.shape, 1)
        mid = None
        for b0 in range(0, SUBLANES, m):
            cand = jnp.broadcast_to(cum3[:, b0 + hm - 1:b0 + hm, :], cum3.shape)
            mid = cand if mid is None else jnp.where(sub >= b0, cand, mid)
        return jnp.abs(cum3 - mid).reshape(c, LANES)
    mid = cum.reshape(c // m, m, LANES)[:, hm - 1:hm, :]
    mid = jnp.broadcast_to(mid, (c // m, m, LANES)).reshape(c, LANES)
    return jnp.abs(cum - mid)


def _pair_levels(c):
    row = lax.broadcasted_iota(jnp.int32, (c, c), 0)
    col = lax.broadcasted_iota(jnp.int32, (c, c), 1)
    diff = jnp.bitwise_xor(row, col)
    lvl = jnp.zeros((c, c), jnp.int32)
    for j in range(c.bit_length() - 1):
        lvl = lvl + (diff >= (1 << j)).astype(jnp.int32)
    return jnp.where(col > row, -1, lvl)


def _chunk_scan(qs, ks, vs, gs, s_refs, lvl):
    heads = range(len(qs))
    c = qs[0].shape[0]
    nt = (((1,), (1,)), ((), ()))
    eye = lvl == 0
    tri = (lvl >= 0).astype(BF16)
    g_hi = [g.astype(BF16) for g in gs]
    g_lo = [(gs[i] - g_hi[i].astype(F32)).astype(BF16) for i in heads]
    cums = [jnp.dot(tri, g_hi[i], preferred_element_type=F32)
            + jnp.dot(tri, g_lo[i], preferred_element_type=F32) for i in heads]

    s_old = [s_refs[i][...] for i in heads]
    vb = [v.astype(BF16) for v in vs]
    o = [jnp.dot((qs[i] * jnp.exp(cums[i])).astype(BF16), s_old[i].astype(BF16),
                 preferred_element_type=F32) for i in heads]

    for i in heads:
        last = cums[i][c - 1:c, :]
        kt_t = jnp.transpose(ks[i] * jnp.exp(last - cums[i])).astype(BF16)
        dec = jnp.broadcast_to(jnp.exp(last), (LANES, LANES))
        dec_col = jnp.sum(jnp.where(eye[:LANES, :LANES], dec, 0.0), axis=1, keepdims=True)
        s_refs[i][...] = s_old[i] * dec_col + jnp.dot(kt_t, vb[i], preferred_element_type=F32)

    rowl = lax.broadcasted_iota(jnp.int32, (c, LANES), 0)
    a = [jnp.where(eye, lax.dot_general(qs[i].astype(BF16), ks[i].astype(BF16), nt,
                                        preferred_element_type=F32), 0.0) for i in heads]
    for m in _chunk_levels(c):
        second = jnp.bitwise_and(rowl, m // 2) != 0
        mask = lvl == m.bit_length() - 1
        x = [(jnp.where(second, qs[i], ks[i])
              * jnp.exp(-_midpoint_distance(cums[i], gs[i], m))).astype(BF16) for i in heads]
        am = [lax.dot_general(x[i], x[i], nt, preferred_element_type=F32) for i in heads]
        a = [jnp.where(mask, am[i], a[i]) for i in heads]
    return o, [x.astype(BF16) for x in a], vb


def _head_cols(ref, rows, i, width):
    return ref[rows, i * width:(i + 1) * width]


def _skewed_chunk_loop(front, finish, pend_refs):
    oi_s, pa_s, vb_s = pend_refs
    heads = range(oi_s.shape[0])

    def stash(parts):
        ois, pas, vbs = parts
        for i in heads:
            oi_s[i] = ois[i]
            pa_s[i] = pas[i]
            vb_s[i] = vbs[i]

    def issue():
        return [oi_s[i] + jnp.dot(pa_s[i], vb_s[i], preferred_element_type=F32) for i in heads]

    stash(front(0))

    def body(ci, carry):
        os = issue()
        stash(front(ci))
        finish(ci - 1, os)
        return carry

    n = SEQ // CHUNK
    lax.fori_loop(1, n, body, 0, unroll=SCAN_UNROLL)
    finish(n - 1, issue())


def _chunk_rows(ci):
    return pl.ds(pl.multiple_of(ci * CHUNK, CHUNK), CHUNK)


def _gla_prompt_kernel(q_ref, k_ref, v_ref, r_ref, a_ref, wup_ref, bg_ref, gn_ref, o_ref, s_ref,
                       oi_s, pa_s, vb_s):
    heads = range(SCAN_HEADS)
    s_ref[...] = jnp.zeros_like(s_ref)
    lvl = _pair_levels(CHUNK)

    def front(ci):
        rows = _chunk_rows(ci)
        a = a_ref[rows, :].astype(BF16)
        pre = jnp.dot(a, wup_ref[...].astype(BF16), preferred_element_type=F32) + bg_ref[...]
        gs = [_log_sigmoid(pre[:, i * GLA_DK:(i + 1) * GLA_DK]) * (1.0 / GLA_TAU) for i in heads]
        qs = [_head_cols(q_ref, rows, i, GLA_DK) * (GLA_DK ** -0.5) for i in heads]
        ks = [_head_cols(k_ref, rows, i, GLA_DK) for i in heads]
        vs = [_head_cols(v_ref, rows, i, GLA_DV) for i in heads]
        return _chunk_scan(qs, ks, vs, gs, [s_ref.at[i] for i in heads], lvl)

    def finish(ci, os):
        rows = _chunk_rows(ci)
        for i in heads:
            gate = _silu(_head_cols(r_ref, rows, i, GLA_DV))
            o_ref[rows, i * GLA_DV:(i + 1) * GLA_DV] = (_rms(os[i], gn_ref[...]) * gate).astype(BF16)

    _skewed_chunk_loop(front, finish, (oi_s, pa_s, vb_s))


def _pending_scratch(heads, dv):
    return [pltpu.VMEM((heads, CHUNK, dv), F32),
            pltpu.VMEM((heads, CHUNK, CHUNK), BF16),
            pltpu.VMEM((heads, CHUNK, dv), BF16)]


def _gla_prompt(proj, alr, wup, bg, gn, merged, layer):
    p = SCAN_HEADS
    dk, dv = p * GLA_DK, p * GLA_DV
    cq, ck, cv, cr = (COL_Q // dk, COL_K // dk, COL_V // dv, COL_R // dv)
    return pl.pallas_call(
        _skip_carries(_gla_prompt_kernel, 8, 1),
        grid=(BATCH, GLA_HEADS // p),
        in_specs=[pl.BlockSpec((SEQ, dk), lambda b, h: (b, cq + h)),
                  pl.BlockSpec((SEQ, dk), lambda b, h: (b, ck + h)),
                  pl.BlockSpec((SEQ, dv), lambda b, h: (b, cv + h)),
                  pl.BlockSpec((SEQ, dv), lambda b, h: (b, cr + h)),
                  pl.BlockSpec((SEQ, LANES), lambda b, h: (b, 0)),
                  pl.BlockSpec((None, LANES, dk), lambda b, h: (layer, 0, h)),
                  pl.BlockSpec((None, 1, dk), lambda b, h: (layer, 0, h)),
                  pl.BlockSpec((None, 1, GLA_DV), lambda b, h: (layer, 0, 0)),
                  pl.BlockSpec(memory_space=pl.ANY)],
        out_specs=[pl.BlockSpec((SEQ, dv), lambda b, h: (b, h)),
                   pl.BlockSpec((None, p, GLA_DK, GLA_DV), lambda b, h: (b, h, 0, 0))],
        out_shape=[jax.ShapeDtypeStruct((N_ROWS, D_MODEL), BF16),
                   jax.ShapeDtypeStruct((BATCH, GLA_HEADS, GLA_DK, GLA_DV), F32)],
        scratch_shapes=_pending_scratch(SCAN_HEADS, GLA_DV),
        input_output_aliases={8: 0},
        compiler_params=_params(("arbitrary", "arbitrary")),
        name="gla_prompt",
    )(proj, proj, proj, proj, alr, wup, bg, gn, merged)


def _hgrn_prompt_kernel(hq_ref, hf_ref, hi_ref, hg_ref, gam_ref, hn_ref, o_ref, s_ref,
                        oi_s, pa_s, vb_s, *, layer):
    heads = range(HGRN_SCAN_HEADS)
    s_ref[...] = jnp.zeros_like(s_ref)
    lvl = _pair_levels(CHUNK)
    lb = _hgrn_lower_bound(gam_ref, layer)

    def front(ci):
        rows = _chunk_rows(ci)
        qs = [_silu(_head_cols(hq_ref, rows, i, HGRN_DK)) * (HGRN_DK ** -0.5) for i in heads]
        kg = [_hgrn_gate(_head_cols(hf_ref, rows, i, HGRN_DK),
                         lb[:, i * HGRN_DK:(i + 1) * HGRN_DK]) for i in heads]
        vs = [_head_cols(hi_ref, rows, i, HGRN_DV) for i in heads]
        return _chunk_scan(qs, [x[0] for x in kg], vs, [x[1] for x in kg],
                           [s_ref.at[i] for i in heads], lvl)

    def finish(ci, os):
        rows = _chunk_rows(ci)
        for i in heads:
            gate = _silu(_head_cols(hg_ref, rows, i, HGRN_DV))
            o_ref[rows, i * HGRN_DV:(i + 1) * HGRN_DV] = (
                _rms(os[i], hn_ref[...]) * gate).astype(BF16)

    _skewed_chunk_loop(front, finish, (oi_s, pa_s, vb_s))


def _hgrn_prompt(proj, gamma, hn, merged, layer):
    p = HGRN_SCAN_HEADS
    dk, dv = p * HGRN_DK, p * HGRN_DV
    cq, cf, ci, cg = (COL_HQ // dk, COL_HF // dk, COL_HI // dv, COL_HG // dv)
    co = GLA_WIDTH // dv
    kern = _skip_carries(functools.partial(_hgrn_prompt_kernel, layer=layer), 6, 1)
    return pl.pallas_call(
        kern,
        grid=(BATCH, HGRN_HEADS // p),
        in_specs=[pl.BlockSpec((SEQ, dk), lambda b, h: (b, cq + h)),
                  pl.BlockSpec((SEQ, dk), lambda b, h: (b, cf + h)),
                  pl.BlockSpec((SEQ, dv), lambda b, h: (b, ci + h)),
                  pl.BlockSpec((SEQ, dv), lambda b, h: (b, cg + h)),
                  pl.BlockSpec((DEPTH, dk), lambda b, h: (0, h)),
                  pl.BlockSpec((None, 1, HGRN_DV), lambda b, h: (layer, 0, 0)),
                  pl.BlockSpec(memory_space=pl.ANY)],
        out_specs=[pl.BlockSpec((SEQ, dv), lambda b, h: (b, co + h)),
                   pl.BlockSpec((None, p, HGRN_DK, HGRN_DV), lambda b, h: (b, h, 0, 0))],
        out_shape=[jax.ShapeDtypeStruct((N_ROWS, D_MODEL), BF16),
                   jax.ShapeDtypeStruct((BATCH, HGRN_HEADS, HGRN_DK, HGRN_DV), F32)],
        scratch_shapes=_pending_scratch(HGRN_SCAN_HEADS, HGRN_DV),
        input_output_aliases={6: 0},
        compiler_params=_params(("arbitrary", "arbitrary")),
        name="hgrn_prompt",
    )(proj, proj, proj, proj, gamma, hn, merged)


def _sample_setup(q, k, v, g, oin_s, qh_s, kt_s, dt_s):
    cum = [g[0]]
    for t in range(1, DEC_SEQ):
        cum.append(cum[t - 1] + g[t])
    for t in range(DEC_SEQ):
        o = jnp.zeros_like(v[0])
        for s in range(t + 1):
            w = q[t] * k[s] if s == t else q[t] * (k[s] * jnp.exp(cum[t] - cum[s]))
            o = o + jnp.sum(w, axis=-1, keepdims=True) * v[s]
        oin_s[t] = o
    last = cum[DEC_SEQ - 1]
    for t in range(DEC_SEQ):
        qh_s[t] = q[t] * jnp.exp(cum[t])
        kt = jnp.transpose(k[t] * jnp.exp(last - cum[t]))
        for blk in range(NBLK):
            kt_s[blk, :, t * SAMPLE_NB:(t + 1) * SAMPLE_NB] = (
                kt[:, blk * SAMPLE_NB:(blk + 1) * SAMPLE_NB])
    dt = jnp.transpose(jnp.exp(last))
    for blk in range(NBLK):
        dt_s[blk] = dt[:, blk * SAMPLE_NB:(blk + 1) * SAMPLE_NB]


def _sample_rows(bb, t):
    return pl.ds(pl.multiple_of(t * DEC_BATCH + bb * SAMPLE_NB, SAMPLE_NB), SAMPLE_NB)


def _sample_step(bb, i, v_ref, sin_ref, sout_ref, qh_s, kt_s, dt_s):
    nb = SAMPLE_NB
    nrow = DEC_SEQ * nb
    dv = sin_ref.shape[-1]
    r0 = pl.multiple_of(bb * nb, nb)
    qh = jnp.concatenate([qh_s[t, pl.ds(r0, nb), :] for t in range(DEC_SEQ)], axis=0)
    vv = jnp.concatenate([v_ref[_sample_rows(bb, t), i * dv:(i + 1) * dv]
                          for t in range(DEC_SEQ)], axis=0)
    rj = jnp.bitwise_and(lax.broadcasted_iota(jnp.int32, (nrow, nb * LANES), 0), nb - 1)
    cb = lax.shift_right_logical(lax.broadcasted_iota(jnp.int32, (nrow, nb * LANES), 1), 7)
    q_bd = jnp.where(rj == cb, jnp.concatenate([qh] * nb, axis=1), 0.0).astype(BF16)
    s_old = sin_ref[:, i]
    s_stack = s_old.reshape(nb * LANES, dv)
    o = jnp.dot(q_bd, s_stack.astype(BF16), preferred_element_type=F32)

    kt64 = kt_s[bb]
    lane_j = jnp.bitwise_and(lax.broadcasted_iota(jnp.int32, kt64.shape, 1), nb - 1)
    kt_bd = jnp.concatenate([jnp.where(lane_j == j, kt64, 0.0) for j in range(nb)],
                            axis=0).astype(BF16)
    upd = jnp.dot(kt_bd, vv.astype(BF16), preferred_element_type=F32)
    dcols = dt_s[bb]
    for j in range(nb):
        sout_ref[j, i] = s_old[j] * dcols[:, j:j + 1] + upd[j * LANES:(j + 1) * LANES, :]
    return o


def _sample_emit(bb, i, o, oin_s, gate_ref, norm_ref, o_ref):
    nb = SAMPLE_NB
    dv = o.shape[-1]
    cols = slice(i * dv, (i + 1) * dv)
    for t in range(DEC_SEQ):
        rows = _sample_rows(bb, t)
        ot = o[t * nb:(t + 1) * nb, :] + oin_s[t, pl.ds(pl.multiple_of(bb * nb, nb), nb), :]
        o_ref[rows, cols] = (_rms(ot, norm_ref[...]) * _silu(gate_ref[rows, cols])).astype(BF16)


def _tiles(ref, i, width):
    return [ref[t * DEC_BATCH:(t + 1) * DEC_BATCH, i * width:(i + 1) * width]
            for t in range(DEC_SEQ)]


def _first_layer_state_out(sout_ref, layer):
    if layer != 0:
        return sout_ref
    for later in range(1, DEPTH):
        sout_ref[later] = jnp.zeros(sout_ref.shape[1:], F32)
    return sout_ref.at[0]


def _gla_sample_kernel(q_ref, k_ref, v_ref, r_ref, a_ref, wup_ref, bg_ref, gn_ref, sin_ref,
                       o_ref, sout_ref, oin_s, qh_s, kt_s, dt_s, *, layer):
    bb = pl.program_id(1)
    heads = range(GLA_SAMPLE_HEADS)

    @pl.when(bb == 0)
    def _():
        for i in heads:
            q = [x * (GLA_DK ** -0.5) for x in _tiles(q_ref, i, GLA_DK)]
            pre = [jnp.dot(x.astype(BF16),
                           wup_ref[:, i * GLA_DK:(i + 1) * GLA_DK].astype(BF16),
                           preferred_element_type=F32) + bg_ref[:, i * GLA_DK:(i + 1) * GLA_DK]
                   for x in _tiles(a_ref, 0, LANES)]
            g = [_log_sigmoid(x) * (1.0 / GLA_TAU) for x in pre]
            _sample_setup(q, _tiles(k_ref, i, GLA_DK), _tiles(v_ref, i, GLA_DV), g,
                          oin_s.at[i], qh_s.at[i], kt_s.at[i], dt_s.at[i])

    sout = _first_layer_state_out(sout_ref, layer)
    for i in heads:
        o = _sample_step(bb, i, v_ref, sin_ref, sout, qh_s.at[i], kt_s.at[i], dt_s.at[i])
        _sample_emit(bb, i, o, oin_s.at[i], r_ref, gn_ref, o_ref)


def _hgrn_sample_kernel(hq_ref, hf_ref, hi_ref, hg_ref, gam_ref, hn_ref, sin_ref,
                        o_ref, sout_ref, oin_s, qh_s, kt_s, dt_s, *, layer):
    bb = pl.program_id(1)
    heads = range(HGRN_SAMPLE_HEADS)

    @pl.when(bb == 0)
    def _():
        lb = _hgrn_lower_bound(gam_ref, layer)
        for i in heads:
            q = [_silu(x) * (HGRN_DK ** -0.5) for x in _tiles(hq_ref, i, HGRN_DK)]
            kg = [_hgrn_gate(x, lb[:, i * HGRN_DK:(i + 1) * HGRN_DK])
                  for x in _tiles(hf_ref, i, HGRN_DK)]
            _sample_setup(q, [x[0] for x in kg], _tiles(hi_ref, i, HGRN_DV), [x[1] for x in kg],
                          oin_s.at[i], qh_s.at[i], kt_s.at[i], dt_s.at[i])

    sout = _first_layer_state_out(sout_ref, layer)
    for i in heads:
        o = _sample_step(bb, i, hi_ref, sin_ref, sout, qh_s.at[i], kt_s.at[i], dt_s.at[i])
        _sample_emit(bb, i, o, oin_s.at[i], hg_ref, hn_ref, o_ref)


def _sample_scratch(nh, dv):
    return [pltpu.VMEM((nh, DEC_SEQ, DEC_BATCH, dv), F32),
            pltpu.VMEM((nh, DEC_SEQ, DEC_BATCH, LANES), F32),
            pltpu.VMEM((nh, NBLK, LANES, DEC_SEQ * SAMPLE_NB), F32),
            pltpu.VMEM((nh, NBLK, LANES, SAMPLE_NB), F32)]


def _sample_call(kern, n_in, in_specs, args, heads, nh, dv, col_out, state_in, merged,
                 state_prev, layer, name):
    st_spec = pl.BlockSpec((None, SAMPLE_NB, nh, LANES, dv), lambda h, bb: (layer, bb, h, 0, 0))
    st_out_spec = st_spec
    if state_prev is None:
        st_out_spec = pl.BlockSpec((DEPTH, SAMPLE_NB, nh, LANES, dv),
                                   lambda h, bb: (0, bb, h, 0, 0))
    carries = [merged] if state_prev is None else [merged, state_prev]
    in_specs = in_specs + [st_spec] + [pl.BlockSpec(memory_space=pl.ANY)] * len(carries)
    args = args + [state_in] + carries
    aliases = {n_in + 1 + i: i for i in range(len(carries))}
    return pl.pallas_call(
        _skip_carries(kern, n_in + 1, len(carries)),
        grid=(heads // nh, NBLK),
        in_specs=in_specs,
        out_specs=[pl.BlockSpec((N_SAMPLE, nh * dv),
                                lambda h, bb: (SAMPLE_ROW_BLOCK, col_out + h)),
                   st_out_spec],
        out_shape=[jax.ShapeDtypeStruct((N_ROWS, D_MODEL), BF16),
                   jax.ShapeDtypeStruct(state_in.shape, F32)],
        scratch_shapes=_sample_scratch(nh, dv),
        input_output_aliases=aliases,
        compiler_params=_params(("arbitrary", "arbitrary")),
        name=name,
    )(*args)


def _sample_tile(width, col):
    return pl.BlockSpec((N_SAMPLE, width), lambda h, bb: (SAMPLE_ROW_BLOCK, col // width + h))


def _gla_sample(proj, alr, wup, bg, gn, state_in, merged, state_prev, layer):
    nh = GLA_SAMPLE_HEADS
    dk, dv = nh * GLA_DK, nh * GLA_DV
    in_specs = [_sample_tile(dk, COL_Q), _sample_tile(dk, COL_K),
                _sample_tile(dv, COL_V), _sample_tile(dv, COL_R),
                pl.BlockSpec((N_SAMPLE, LANES), lambda h, bb: (SAMPLE_ROW_BLOCK, 0)),
                pl.BlockSpec((None, LANES, dk), lambda h, bb: (layer, 0, h)),
                pl.BlockSpec((None, 1, dk), lambda h, bb: (layer, 0, h)),
                pl.BlockSpec((None, 1, GLA_DV), lambda h, bb: (layer, 0, 0))]
    args = [proj, proj, proj, proj, alr, wup, bg, gn]
    return _sample_call(functools.partial(_gla_sample_kernel, layer=layer), 8, in_specs, args,
                        GLA_HEADS, nh, GLA_DV, 0, state_in, merged, state_prev, layer,
                        "gla_sample")


def _hgrn_sample(proj, gamma, hn, state_in, merged, state_prev, layer):
    nh = HGRN_SAMPLE_HEADS
    dk, dv = nh * HGRN_DK, nh * HGRN_DV
    in_specs = [_sample_tile(dk, COL_HQ), _sample_tile(dk, COL_HF),
                _sample_tile(dv, COL_HI), _sample_tile(dv, COL_HG),
                pl.BlockSpec((DEPTH, dk), lambda h, bb: (0, h)),
                pl.BlockSpec((None, 1, HGRN_DV), lambda h, bb: (layer, 0, 0))]
    args = [proj, proj, proj, proj, gamma, hn]
    return _sample_call(functools.partial(_hgrn_sample_kernel, layer=layer), 6, in_specs, args,
                        HGRN_HEADS, nh, HGRN_DV, GLA_WIDTH // dv, state_in, merged, state_prev,
                        layer, "hgrn_sample")


def kernel(x_prompt, x_sample, state_gla, state_hgrn, norm_gains, ffn1_w_in, ffn1_w_out,
           ffn2_w_in, ffn2_w_out, mix_w_in, gla_w_gate_up, gla_b_gate, gla_norm, hgrn_gamma,
           hgrn_norm, mix_w_out):
    xp = x_prompt.reshape(N_PROMPT, D_MODEL)
    xs = jnp.transpose(x_sample, (1, 0, 2)).reshape(N_SAMPLE, D_MODEL)
    gains = norm_gains.reshape(DEPTH * 6, 1, D_MODEL)
    w_mix_t = jnp.swapaxes(mix_w_in, 1, 2)
    wup = jnp.pad(gla_w_gate_up, ((0, 0), (0, LANES - GLA_RANK), (0, 0)))
    bg = gla_b_gate.reshape(DEPTH, 1, GLA_HEADS * GLA_DK)
    gn = gla_norm.reshape(DEPTH, 1, GLA_DV)
    hn = hgrn_norm.reshape(DEPTH, 1, HGRN_DV)

    h = _rms_cast(xp, xs, gains, 0)
    x = None
    gla_p, hgrn_p = [], []
    st_gla, st_hgrn = None, None
    for l in range(DEPTH):
        base = 6 * l
        act, wo = _ffn_in(h, ffn1_w_in, ffn1_w_out, l)
        if l == 0:
            x, h = _proj_out(act, wo, xp, gains, base + 1, base + 2, 0.5, x_sample=xs)
        else:
            x, h = _proj_out(act, wo, x, gains, base + 1, base + 2, 0.5)
        proj, alr = _mix_in(h, w_mix_t, l)
        merged, wo_mix = _mix_aux(mix_w_out, l)
        merged, sg = _gla_prompt(proj, alr, wup, bg, gn, merged, l)
        merged, sh = _hgrn_prompt(proj, hgrn_gamma, hn, merged, l)
        merged, st_gla = _gla_sample(proj, alr, wup, bg, gn, state_gla, merged, st_gla, l)
        merged, st_hgrn = _hgrn_sample(proj, hgrn_gamma, hn, state_hgrn, merged, st_hgrn, l)
        gla_p.append(sg)
        hgrn_p.append(sh)
        x, h = _proj_out(merged, wo_mix, x, gains, base + 3, base + 4, 1.0)
        act, wo = _ffn_in(h, ffn2_w_in, ffn2_w_out, l)
        if l + 1 < DEPTH:
            x, h = _proj_out(act, wo, x, gains, base + 5, base + 6, 0.5)
        else:
            y_prompt_rows, y_sample_rows = _proj_out(act, wo, x, gains, base + 5, None, 0.5)

    y_prompt = y_prompt_rows.reshape(BATCH, SEQ, D_MODEL)
    y_sample = jnp.transpose(y_sample_rows.reshape(DEC_SEQ, DEC_BATCH, D_MODEL), (1, 0, 2))
    return (y_prompt, y_sample, jnp.stack(gla_p), jnp.stack(hgrn_p), st_gla, st_hgrn)
```

```python
import functools

import jax
import jax.numpy as jnp
from jax import lax
from jax.experimental import pallas as pl
from jax.experimental.pallas import tpu as pltpu

F32 = jnp.float32
BF16 = jnp.bfloat16

D_MODEL = 2048
BATCH = 4
SEQ = 2048
DEPTH = 2
DEC_BATCH = 128
DEC_SEQ = 4
GLA_HEADS = 4
GLA_DK = 128
GLA_DV = 256
GLA_WIDTH = GLA_HEADS * GLA_DV
GLA_RANK = 16
GLA_TAU = 16.0
HGRN_HEADS = 8
HGRN_DK = 128
HGRN_DV = 128
HGRN_WIDTH = HGRN_HEADS * HGRN_DV
D_FF = 5632
EPS = 1e-6

N_PROMPT = BATCH * SEQ
N_SAMPLE = DEC_BATCH * DEC_SEQ
N_ROWS = N_PROMPT + N_SAMPLE
SAMPLE_ROW_BLOCK = N_PROMPT // N_SAMPLE

LANES = 128
SUBLANES = 8
VMEM_LIMIT = 60 * 1024 * 1024

COL_Q = 0
COL_K = 512
COL_V = 1024
COL_R = 2048
COL_HQ = 3072
COL_HF = 4096
COL_HI = 5120
COL_HG = 6144
PROJ_W = 7168
GATE_COL = 2048
MIX_ALIGNED_TILES = GATE_COL // 1024

TM_FFN = 1088
TM_IN = 1088
TN = 512
MIX_TN = 1024
FFN_NC = 256
TM_OUT = 1088
TK_CHOICES = (1408, 1024, 512)
OUT_NC = 512
MIX_WO_ROWS = D_MODEL // (N_ROWS // TM_IN)
OUT_RC = 272
LAST_TILE = N_ROWS // TM_OUT - 1
TAIL_PROMPT_ROWS = N_PROMPT - LAST_TILE * TM_OUT
TM_ROW = 512
CHUNK = 256
SCAN_HEADS = 2
HGRN_SCAN_HEADS = 4
SCAN_UNROLL = 1
SAMPLE_NB = 16
GLA_SAMPLE_HEADS = 2
HGRN_SAMPLE_HEADS = 4
NBLK = DEC_BATCH // SAMPLE_NB


def _params(sem):
    return pltpu.CompilerParams(dimension_semantics=sem, vmem_limit_bytes=VMEM_LIMIT)


def _sigmoid(x):
    return 1.0 / (1.0 + jnp.exp(-x))


def _silu(x):
    return x * _sigmoid(x)


def _rms(y, g):
    return y * lax.rsqrt(jnp.mean(y * y, axis=-1, keepdims=True) + EPS) * g


def _skip_carries(kern, n_in, n_carry):
    def wrapped(*refs):
        return kern(*refs[:n_in], *refs[n_in + n_carry:])
    return wrapped


def _rms_cast_kernel(xp_ref, xs_ref, g_ref, h_ref):
    prompt_tiles = N_PROMPT // TM_ROW

    @pl.when(pl.program_id(0) < prompt_tiles)
    def _():
        h_ref[...] = _rms(xp_ref[...], g_ref[...]).astype(BF16)

    @pl.when(pl.program_id(0) >= prompt_tiles)
    def _():
        h_ref[...] = _rms(xs_ref[...], g_ref[...]).astype(BF16)


def _rms_cast(x_prompt, x_sample, gains, gi):
    prompt_tiles = N_PROMPT // TM_ROW
    return pl.pallas_call(
        _rms_cast_kernel,
        grid=(N_ROWS // TM_ROW,),
        in_specs=[pl.BlockSpec((TM_ROW, D_MODEL), lambda m: (jnp.minimum(m, prompt_tiles - 1), 0)),
                  pl.BlockSpec((TM_ROW, D_MODEL), lambda m: (jnp.maximum(m - prompt_tiles, 0), 0)),
                  pl.BlockSpec((None, 1, D_MODEL), lambda m: (gi, 0, 0))],
        out_specs=pl.BlockSpec((TM_ROW, D_MODEL), lambda m: (m, 0)),
        out_shape=jax.ShapeDtypeStruct((N_ROWS, D_MODEL), BF16),
        compiler_params=_params(("arbitrary",)),
        name="rms_cast",
    )(x_prompt, x_sample, gains)


def _ffn_in_kernel(h_ref, wg_ref, wu_ref, wo_ref, o_ref, wob_ref, wg_s, wu_s):
    @pl.when(pl.program_id(1) == 0)
    def _():
        wg_s[...] = wg_ref[...].astype(BF16)
        wu_s[...] = wu_ref[...].astype(BF16)

    wob_ref[...] = wo_ref[...].astype(BF16)
    h = h_ref[...]
    for c in range(TN // FFN_NC):
        cols = slice(c * FFN_NC, (c + 1) * FFN_NC)
        g = jnp.dot(h, wg_s[:, cols], preferred_element_type=F32)
        u = jnp.dot(h, wu_s[:, cols], preferred_element_type=F32)
        o_ref[:, cols] = (g * _sigmoid(g) * u).astype(BF16)


def _ffn_in(h, w_in, w_out, layer):
    nt = D_FF // TN
    nm = N_ROWS // TM_FFN
    slab = D_FF // (nt * nm)
    return pl.pallas_call(
        _ffn_in_kernel,
        grid=(nt, nm),
        in_specs=[pl.BlockSpec((TM_FFN, D_MODEL), lambda n, m: (m, 0)),
                  pl.BlockSpec((None, D_MODEL, TN), lambda n, m: (layer, 0, n)),
                  pl.BlockSpec((None, D_MODEL, TN), lambda n, m: (layer, 0, n + nt)),
                  pl.BlockSpec((None, slab, D_MODEL), lambda n, m: (layer, n * nm + m, 0))],
        out_specs=[pl.BlockSpec((TM_FFN, TN), lambda n, m: (m, n)),
                   pl.BlockSpec((slab, D_MODEL), lambda n, m: (n * nm + m, 0))],
        out_shape=[jax.ShapeDtypeStruct((N_ROWS, D_FF), BF16),
                   jax.ShapeDtypeStruct((D_FF, D_MODEL), BF16)],
        scratch_shapes=[pltpu.VMEM((D_MODEL, TN), BF16), pltpu.VMEM((D_MODEL, TN), BF16)],
        compiler_params=_params(("arbitrary", "arbitrary")),
        name="ffn_in",
    )(h, w_in, w_in, w_out)


def _mix_in_kernel(h_ref, wa_ref, wb_ref, o_ref, a_ref, w_s, wg_s):
    nt = (((1,), (1,)), ((), ()))
    n = pl.program_id(0)
    first_row_tile = pl.program_id(1) == 0
    gate_tile = n == MIX_ALIGNED_TILES
    kept = MIX_TN - GLA_RANK

    @pl.when(jnp.logical_and(first_row_tile, n < MIX_ALIGNED_TILES))
    def _():
        w_s[...] = wa_ref[...].astype(BF16)

    @pl.when(jnp.logical_and(first_row_tile, n >= MIX_ALIGNED_TILES))
    def _():
        w_s[0:kept, :] = wa_ref[GLA_RANK:MIX_TN, :].astype(BF16)
        w_s[kept:MIX_TN, :] = wb_ref[0:GLA_RANK, :].astype(BF16)

    @pl.when(jnp.logical_and(first_row_tile, gate_tile))
    def _():
        wg_s[...] = jnp.zeros_like(wg_s)
        wg_s[0:GLA_RANK, :] = wa_ref[0:GLA_RANK, :].astype(BF16)

    h = h_ref[...]
    o_ref[...] = lax.dot_general(h, w_s[...], nt, preferred_element_type=F32)

    @pl.when(gate_tile)
    def _():
        a_ref[...] = lax.dot_general(h, wg_s[...], nt, preferred_element_type=F32)


def _mix_in(h, w_in_t, layer):
    last = N_ROWS // TM_IN - 1
    slabs_per_tile = MIX_TN // LANES

    def gate_block(n, m):
        return jnp.where(n == MIX_ALIGNED_TILES, m, jnp.where(n < MIX_ALIGNED_TILES, 0, last))

    return pl.pallas_call(
        _mix_in_kernel,
        grid=(PROJ_W // MIX_TN, N_ROWS // TM_IN),
        in_specs=[pl.BlockSpec((TM_IN, D_MODEL), lambda n, m: (m, 0)),
                  pl.BlockSpec((None, MIX_TN, D_MODEL), lambda n, m: (layer, n, 0)),
                  pl.BlockSpec((None, LANES, D_MODEL),
                               lambda n, m: (layer, (n + 1) * slabs_per_tile, 0))],
        out_specs=[pl.BlockSpec((TM_IN, MIX_TN), lambda n, m: (m, n)),
                   pl.BlockSpec((TM_IN, LANES), lambda n, m: (gate_block(n, m), 0))],
        out_shape=[jax.ShapeDtypeStruct((N_ROWS, PROJ_W), F32),
                   jax.ShapeDtypeStruct((N_ROWS, LANES), F32)],
        scratch_shapes=[pltpu.VMEM((MIX_TN, D_MODEL), BF16), pltpu.VMEM((LANES, D_MODEL), BF16)],
        compiler_params=_params(("arbitrary", "arbitrary")),
        name="mix_in",
    )(h, w_in_t, w_in_t)


def _mix_aux_kernel(wo_ref, z_ref, wob_ref):
    z_ref[...] = jnp.zeros_like(z_ref)
    wob_ref[...] = wo_ref[...].astype(BF16)


def _mix_aux(w_out, layer):
    return pl.pallas_call(
        _mix_aux_kernel,
        grid=(N_ROWS // TM_IN,),
        in_specs=[pl.BlockSpec((None, MIX_WO_ROWS, D_MODEL), lambda m: (layer, m, 0))],
        out_specs=[pl.BlockSpec((TM_IN, D_MODEL), lambda m: (m, 0)),
                   pl.BlockSpec((MIX_WO_ROWS, D_MODEL), lambda m: (m, 0))],
        out_shape=[jax.ShapeDtypeStruct((N_ROWS, D_MODEL), BF16),
                   jax.ShapeDtypeStruct((D_MODEL, D_MODEL), BF16)],
        compiler_params=_params(("arbitrary",)),
        name="mix_aux",
    )(w_out)


def _proj_out_kernel(*refs, alpha, nk, split_x, final):
    refs = list(refs)
    a_ref, w_ref, x_hbm = refs[:3]
    xs_hbm = refs[3] if split_x else None
    gp_ref = refs[3 + split_x]
    gn_ref = None if final else refs[4 + split_x]
    xo_ref, o2_ref, xbuf, sem = refs[-4:]
    m = pl.program_id(0)
    k = pl.program_id(1)

    def tile_copy():
        rows = pl.ds(pl.multiple_of(m * TM_OUT, TM_OUT), TM_OUT)
        return pltpu.make_async_copy(x_hbm.at[rows, :], xbuf, sem.at[0])

    def tail_copies():
        head = pltpu.make_async_copy(x_hbm.at[pl.ds(LAST_TILE * TM_OUT, TAIL_PROMPT_ROWS), :],
                                     xbuf.at[pl.ds(0, TAIL_PROMPT_ROWS), :], sem.at[0])
        tail = pltpu.make_async_copy(xs_hbm, xbuf.at[pl.ds(TAIL_PROMPT_ROWS, N_SAMPLE), :],
                                     sem.at[1])
        return head, tail

    def residual_copies(act):
        if not split_x:
            act(tile_copy())
            return

        @pl.when(m < LAST_TILE)
        def _():
            act(tile_copy())

        @pl.when(m == LAST_TILE)
        def _():
            for c in tail_copies():
                act(c)

    def accumulate(rows, first):
        a = a_ref[rows, :]
        for c in range(D_MODEL // OUT_NC):
            cols = slice(c * OUT_NC, (c + 1) * OUT_NC)
            p = jnp.dot(a, w_ref[:, cols], preferred_element_type=F32)
            xo_ref[rows, cols] = p if first else xo_ref[rows, cols] + p

    whole = slice(0, TM_OUT)

    @pl.when(k == 0)
    def _():
        residual_copies(lambda c: c.start())
        accumulate(whole, True)

    @pl.when(jnp.logical_and(k > 0, k < nk - 1))
    def _():
        accumulate(whole, False)

    @pl.when(k == nk - 1)
    def _():
        residual_copies(lambda c: c.wait())
        for i in range(TM_OUT // OUT_RC):
            rows = slice(i * OUT_RC, (i + 1) * OUT_RC)
            accumulate(rows, False)
            xn = xbuf[rows, :] + alpha * _rms(xo_ref[rows, :], gp_ref[...])
            xo_ref[rows, :] = xn
            if not final:
                o2_ref[rows, :] = _rms(xn, gn_ref[...]).astype(BF16)
        if final:
            @pl.when(m == LAST_TILE)
            def _():
                o2_ref[...] = xo_ref[pl.ds(TAIL_PROMPT_ROWS, N_SAMPLE), :]


def _proj_out(a, w, x, gains, gi_post, gi_next, alpha, x_sample=None):
    kdim = a.shape[1]
    tk = next(t for t in TK_CHOICES if kdim % t == 0)
    nk = kdim // tk
    split_x = x_sample is not None
    final = gi_next is None
    kern = functools.partial(_proj_out_kernel, alpha=alpha, nk=nk, split_x=split_x, final=final)
    gain = lambda gi: pl.BlockSpec((None, 1, D_MODEL), lambda m, k: (gi, 0, 0))
    in_specs = [pl.BlockSpec((TM_OUT, tk), lambda m, k: (m, k)),
                pl.BlockSpec((tk, D_MODEL), lambda m, k: (k, 0)),
                pl.BlockSpec(memory_space=pl.ANY)]
    args = [a, w, x]
    if split_x:
        in_specs.append(pl.BlockSpec(memory_space=pl.ANY))
        args.append(x_sample)
    in_specs.append(gain(gi_post))
    args.append(gains)
    row_tile = pl.BlockSpec((TM_OUT, D_MODEL), lambda m, k: (m, 0))
    if final:
        out_specs = [row_tile, pl.BlockSpec((N_SAMPLE, D_MODEL), lambda m, k: (0, 0))]
        out_shape = [jax.ShapeDtypeStruct((N_PROMPT, D_MODEL), F32),
                     jax.ShapeDtypeStruct((N_SAMPLE, D_MODEL), F32)]
    else:
        in_specs.append(gain(gi_next))
        args.append(gains)
        out_specs = [row_tile, row_tile]
        out_shape = [jax.ShapeDtypeStruct((N_ROWS, D_MODEL), F32),
                     jax.ShapeDtypeStruct((N_ROWS, D_MODEL), BF16)]
    return pl.pallas_call(
        kern,
        grid=(N_ROWS // TM_OUT, nk),
        in_specs=in_specs,
        out_specs=out_specs,
        out_shape=out_shape,
        scratch_shapes=[pltpu.VMEM((TM_OUT, D_MODEL), F32), pltpu.SemaphoreType.DMA((2,))],
        compiler_params=_params(("arbitrary", "arbitrary")),
        name="proj_out",
    )(*args)


def _log_sigmoid(x):
    return jnp.minimum(x, 0.0) - jnp.log(1.0 + jnp.exp(-jnp.abs(x)))


def _hgrn_lower_bound(gam_ref, layer):
    gam = gam_ref[...]
    mx = jnp.max(gam, axis=0, keepdims=True)
    e = jnp.exp(gam - mx)
    probs = e / jnp.sum(e, axis=0, keepdims=True)
    acc = probs[0:1, :]
    for j in range(1, layer + 1):
        acc = acc + probs[j:j + 1, :]
    return acc - probs[0:1, :]


def _hgrn_gate(z, lb):
    e = jnp.exp(-jnp.abs(z))
    r = 1.0 / (1.0 + e)
    er = e * r
    pos = z >= 0.0
    sig = jnp.where(pos, r, er)
    sig_neg = jnp.where(pos, er, r)
    return (1.0 - lb) * sig_neg, jnp.log(lb + (1.0 - lb) * sig)


def _chunk_levels(c):
    levels, m = [], 2
    while m <= c:
        levels.append(m)
        m *= 2
    return levels


def _midpoint_distance(cum, g, m):
    c = cum.shape[0]
    hm = m // 2
    if m == 2:
        odd = jnp.bitwise_and(lax.broadcasted_iota(jnp.int32, cum.shape, 0), 1) == 1
        return jnp.where(odd, jnp.abs(g), 0.0)
    if m < 2 * SUBLANES:
        cum3 = cum.reshape(c // SUBLANES, SUBLANES, LANES)
        sub = lax.broadcasted_iota(jnp.int32, cum3.shape, 1)
        mid = None
        for b0 in range(0, SUBLANES, m):
            cand = jnp.broadcast_to(cum3[:, b0 + hm - 1:b0 + hm, :], cum3.shape)
            mid = cand if mid is None else jnp.where(sub >= b0, cand, mid)
        return jnp.abs(cum3 - mid).reshape(c, LANES)
    mid = cum.reshape(c // m, m, LANES)[:, hm - 1:hm, :]
    mid = jnp.broadcast_to(mid, (c // m, m, LANES)).reshape(c, LANES)
    return jnp.abs(cum - mid)


def _pair_levels(c):
    row = lax.broadcasted_iota(jnp.int32, (c, c), 0)
    col = lax.broadcasted_iota(jnp.int32, (c, c), 1)
    diff = jnp.bitwise_xor(row, col)
    lvl = jnp.zeros((c, c), jnp.int32)
    for j in range(c.bit_length() - 1):
        lvl = lvl + (diff >= (1 << j)).astype(jnp.int32)
    return jnp.where(col > row, -1, lvl)


def _chunk_scan(qs, ks, vs, gs, s_refs, lvl):
    heads = range(len(qs))
    c = qs[0].shape[0]
    nt = (((1,), (1,)), ((), ()))
    eye = lvl == 0
    tri = (lvl >= 0).astype(BF16)
    g_hi = [g.astype(BF16) for g in gs]
    g_lo = [(gs[i] - g_hi[i].astype(F32)).astype(BF16) for i in heads]
    cums = [jnp.dot(tri, g_hi[i], preferred_element_type=F32)
            + jnp.dot(tri, g_lo[i], preferred_element_type=F32) for i in heads]

    s_old = [s_refs[i][...] for i in heads]
    vb = [v.astype(BF16) for v in vs]
    o = [jnp.dot((qs[i] * jnp.exp(cums[i])).astype(BF16), s_old[i].astype(BF16),
                 preferred_element_type=F32) for i in heads]

    for i in heads:
        last = cums[i][c - 1:c, :]
        kt_t = jnp.transpose(ks[i] * jnp.exp(last - cums[i])).astype(BF16)
        dec = jnp.broadcast_to(jnp.exp(last), (LANES, LANES))
        dec_col = jnp.sum(jnp.where(eye[:LANES, :LANES], dec, 0.0), axis=1, keepdims=True)
        s_refs[i][...] = s_old[i] * dec_col + jnp.dot(kt_t, vb[i], preferred_element_type=F32)

    rowl = lax.broadcasted_iota(jnp.int32, (c, LANES), 0)
    a = [jnp.where(eye, lax.dot_general(qs[i].astype(BF16), ks[i].astype(BF16), nt,
                                        preferred_element_type=F32), 0.0) for i in heads]
    for m in _chunk_levels(c):
        second = jnp.bitwise_and(rowl, m // 2) != 0
        mask = lvl == m.bit_length() - 1
        x = [(jnp.where(second, qs[i], ks[i])
              * jnp.exp(-_midpoint_distance(cums[i], gs[i], m))).astype(BF16) for i in heads]
        am = [lax.dot_general(x[i], x[i], nt, preferred_element_type=F32) for i in heads]
        a = [jnp.where(mask, am[i], a[i]) for i in heads]
    return o, [x.astype(BF16) for x in a], vb


def _head_cols(ref, rows, i, width):
    return ref[rows, i * width:(i + 1) * width]


def _skewed_chunk_loop(front, finish, pend_refs):
    oi_s, pa_s, vb_s = pend_refs
    heads = range(oi_s.shape[0])

    def stash(parts):
        ois, pas, vbs = parts
        for i in heads:
            oi_s[i] = ois[i]
            pa_s[i] = pas[i]
            vb_s[i] = vbs[i]

    def issue():
        return [oi_s[i] + jnp.dot(pa_s[i], vb_s[i], preferred_element_type=F32) for i in heads]

    stash(front(0))

    def body(ci, carry):
        os = issue()
        stash(front(ci))
        finish(ci - 1, os)
        return carry

    n = SEQ // CHUNK
    lax.fori_loop(1, n, body, 0, unroll=SCAN_UNROLL)
    finish(n - 1, issue())


def _chunk_rows(ci):
    return pl.ds(pl.multiple_of(ci * CHUNK, CHUNK), CHUNK)


def _gla_prompt_kernel(q_ref, k_ref, v_ref, r_ref, a_ref, wup_ref, bg_ref, gn_ref, o_ref, s_ref,
                       oi_s, pa_s, vb_s):
    heads = range(SCAN_HEADS)
    s_ref[...] = jnp.zeros_like(s_ref)
    lvl = _pair_levels(CHUNK)

    def front(ci):
        rows = _chunk_rows(ci)
        a = a_ref[rows, :].astype(BF16)
        pre = jnp.dot(a, wup_ref[...].astype(BF16), preferred_element_type=F32) + bg_ref[...]
        gs = [_log_sigmoid(pre[:, i * GLA_DK:(i + 1) * GLA_DK]) * (1.0 / GLA_TAU) for i in heads]
        qs = [_head_cols(q_ref, rows, i, GLA_DK) * (GLA_DK ** -0.5) for i in heads]
        ks = [_head_cols(k_ref, rows, i, GLA_DK) for i in heads]
        vs = [_head_cols(v_ref, rows, i, GLA_DV) for i in heads]
        return _chunk_scan(qs, ks, vs, gs, [s_ref.at[i] for i in heads], lvl)

    def finish(ci, os):
        rows = _chunk_rows(ci)
        for i in heads:
            gate = _silu(_head_cols(r_ref, rows, i, GLA_DV))
            o_ref[rows, i * GLA_DV:(i + 1) * GLA_DV] = (_rms(os[i], gn_ref[...]) * gate).astype(BF16)

    _skewed_chunk_loop(front, finish, (oi_s, pa_s, vb_s))


def _pending_scratch(heads, dv):
    return [pltpu.VMEM((heads, CHUNK, dv), F32),
            pltpu.VMEM((heads, CHUNK, CHUNK), BF16),
            pltpu.VMEM((heads, CHUNK, dv), BF16)]


def _gla_prompt(proj, alr, wup, bg, gn, merged, layer):
    p = SCAN_HEADS
    dk, dv = p * GLA_DK, p * GLA_DV
    cq, ck, cv, cr = (COL_Q // dk, COL_K // dk, COL_V // dv, COL_R // dv)
    return pl.pallas_call(
        _skip_carries(_gla_prompt_kernel, 8, 1),
        grid=(BATCH, GLA_HEADS // p),
        in_specs=[pl.BlockSpec((SEQ, dk), lambda b, h: (b, cq + h)),
                  pl.BlockSpec((SEQ, dk), lambda b, h: (b, ck + h)),
                  pl.BlockSpec((SEQ, dv), lambda b, h: (b, cv + h)),
                  pl.BlockSpec((SEQ, dv), lambda b, h: (b, cr + h)),
                  pl.BlockSpec((SEQ, LANES), lambda b, h: (b, 0)),
                  pl.BlockSpec((None, LANES, dk), lambda b, h: (layer, 0, h)),
                  pl.BlockSpec((None, 1, dk), lambda b, h: (layer, 0, h)),
                  pl.BlockSpec((None, 1, GLA_DV), lambda b, h: (layer, 0, 0)),
                  pl.BlockSpec(memory_space=pl.ANY)],
        out_specs=[pl.BlockSpec((SEQ, dv), lambda b, h: (b, h)),
                   pl.BlockSpec((None, p, GLA_DK, GLA_DV), lambda b, h: (b, h, 0, 0))],
        out_shape=[jax.ShapeDtypeStruct((N_ROWS, D_MODEL), BF16),
                   jax.ShapeDtypeStruct((BATCH, GLA_HEADS, GLA_DK, GLA_DV), F32)],
        scratch_shapes=_pending_scratch(SCAN_HEADS, GLA_DV),
        input_output_aliases={8: 0},
        compiler_params=_params(("arbitrary", "arbitrary")),
        name="gla_prompt",
    )(proj, proj, proj, proj, alr, wup, bg, gn, merged)


def _hgrn_prompt_kernel(hq_ref, hf_ref, hi_ref, hg_ref, gam_ref, hn_ref, o_ref, s_ref,
                        oi_s, pa_s, vb_s, *, layer):
    heads = range(HGRN_SCAN_HEADS)
    s_ref[...] = jnp.zeros_like(s_ref)
    lvl = _pair_levels(CHUNK)
    lb = _hgrn_lower_bound(gam_ref, layer)

    def front(ci):
        rows = _chunk_rows(ci)
        qs = [_silu(_head_cols(hq_ref, rows, i, HGRN_DK)) * (HGRN_DK ** -0.5) for i in heads]
        kg = [_hgrn_gate(_head_cols(hf_ref, rows, i, HGRN_DK),
                         lb[:, i * HGRN_DK:(i + 1) * HGRN_DK]) for i in heads]
        vs = [_head_cols(hi_ref, rows, i, HGRN_DV) for i in heads]
        return _chunk_scan(qs, [x[0] for x in kg], vs, [x[1] for x in kg],
                           [s_ref.at[i] for i in heads], lvl)

    def finish(ci, os):
        rows = _chunk_rows(ci)
        for i in heads:
            gate = _silu(_head_cols(hg_ref, rows, i, HGRN_DV))
            o_ref[rows, i * HGRN_DV:(i + 1) * HGRN_DV] = (
                _rms(os[i], hn_ref[...]) * gate).astype(BF16)

    _skewed_chunk_loop(front, finish, (oi_s, pa_s, vb_s))


def _hgrn_prompt(proj, gamma, hn, merged, layer):
    p = HGRN_SCAN_HEADS
    dk, dv = p * HGRN_DK, p * HGRN_DV
    cq, cf, ci, cg = (COL_HQ // dk, COL_HF // dk, COL_HI // dv, COL_HG // dv)
    co = GLA_WIDTH // dv
    kern = _skip_carries(functools.partial(_hgrn_prompt_kernel, layer=layer), 6, 1)
    return pl.pallas_call(
        kern,
        grid=(BATCH, HGRN_HEADS // p),
        in_specs=[pl.BlockSpec((SEQ, dk), lambda b, h: (b, cq + h)),
                  pl.BlockSpec((SEQ, dk), lambda b, h: (b, cf + h)),
                  pl.BlockSpec((SEQ, dv), lambda b, h: (b, ci + h)),
                  pl.BlockSpec((SEQ, dv), lambda b, h: (b, cg + h)),
                  pl.BlockSpec((DEPTH, dk), lambda b, h: (0, h)),
                  pl.BlockSpec((None, 1, HGRN_DV), lambda b, h: (layer, 0, 0)),
                  pl.BlockSpec(memory_space=pl.ANY)],
        out_specs=[pl.BlockSpec((SEQ, dv), lambda b, h: (b, co + h)),
                   pl.BlockSpec((None, p, HGRN_DK, HGRN_DV), lambda b, h: (b, h, 0, 0))],
        out_shape=[jax.ShapeDtypeStruct((N_ROWS, D_MODEL), BF16),
                   jax.ShapeDtypeStruct((BATCH, HGRN_HEADS, HGRN_DK, HGRN_DV), F32)],
        scratch_shapes=_pending_scratch(HGRN_SCAN_HEADS, HGRN_DV),
        input_output_aliases={6: 0},
        compiler_params=_params(("arbitrary", "arbitrary")),
        name="hgrn_prompt",
    )(proj, proj, proj, proj, gamma, hn, merged)


def _sample_setup(q, k, v, g, oin_s, qh_s, kt_s, dt_s):
    cum = [g[0]]
    for t in range(1, DEC_SEQ):
        cum.append(cum[t - 1] + g[t])
    for t in range(DEC_SEQ):
        o = jnp.zeros_like(v[0])
        for s in range(t + 1):
            w = q[t] * k[s] if s == t else q[t] * (k[s] * jnp.exp(cum[t] - cum[s]))
            o = o + jnp.sum(w, axis=-1, keepdims=True) * v[s]
        oin_s[t] = o
    last = cum[DEC_SEQ - 1]
    for t in range(DEC_SEQ):
        qh_s[t] = q[t] * jnp.exp(cum[t])
        kt = jnp.transpose(k[t] * jnp.exp(last - cum[t]))
        for blk in range(NBLK):
            kt_s[blk, :, t * SAMPLE_NB:(t + 1) * SAMPLE_NB] = (
                kt[:, blk * SAMPLE_NB:(blk + 1) * SAMPLE_NB])
    dt = jnp.transpose(jnp.exp(last))
    for blk in range(NBLK):
        dt_s[blk] = dt[:, blk * SAMPLE_NB:(blk + 1) * SAMPLE_NB]


def _sample_rows(bb, t):
    return pl.ds(pl.multiple_of(t * DEC_BATCH + bb * SAMPLE_NB, SAMPLE_NB), SAMPLE_NB)


def _sample_step(bb, i, v_ref, sin_ref, sout_ref, qh_s, kt_s, dt_s):
    nb = SAMPLE_NB
    nrow = DEC_SEQ * nb
    dv = sin_ref.shape[-1]
    r0 = pl.multiple_of(bb * nb, nb)
    qh = jnp.concatenate([qh_s[t, pl.ds(r0, nb), :] for t in range(DEC_SEQ)], axis=0)
    vv = jnp.concatenate([v_ref[_sample_rows(bb, t), i * dv:(i + 1) * dv]
                          for t in range(DEC_SEQ)], axis=0)
    rj = jnp.bitwise_and(lax.broadcasted_iota(jnp.int32, (nrow, nb * LANES), 0), nb - 1)
    cb = lax.shift_right_logical(lax.broadcasted_iota(jnp.int32, (nrow, nb * LANES), 1), 7)
    q_bd = jnp.where(rj == cb, jnp.concatenate([qh] * nb, axis=1), 0.0).astype(BF16)
    s_old = sin_ref[:, i]
    s_stack = s_old.reshape(nb * LANES, dv)
    o = jnp.dot(q_bd, s_stack.astype(BF16), preferred_element_type=F32)

    kt64 = kt_s[bb]
    lane_j = jnp.bitwise_and(lax.broadcasted_iota(jnp.int32, kt64.shape, 1), nb - 1)
    kt_bd = jnp.concatenate([jnp.where(lane_j == j, kt64, 0.0) for j in range(nb)],
                            axis=0).astype(BF16)
    upd = jnp.dot(kt_bd, vv.astype(BF16), preferred_element_type=F32)
    dcols = dt_s[bb]
    for j in range(nb):
        sout_ref[j, i] = s_old[j] * dcols[:, j:j + 1] + upd[j * LANES:(j + 1) * LANES, :]
    return o


def _sample_emit(bb, i, o, oin_s, gate_ref, norm_ref, o_ref):
    nb = SAMPLE_NB
    dv = o.shape[-1]
    cols = slice(i * dv, (i + 1) * dv)
    for t in range(DEC_SEQ):
        rows = _sample_rows(bb, t)
        ot = o[t * nb:(t + 1) * nb, :] + oin_s[t, pl.ds(pl.multiple_of(bb * nb, nb), nb), :]
        o_ref[rows, cols] = (_rms(ot, norm_ref[...]) * _silu(gate_ref[rows, cols])).astype(BF16)


def _tiles(ref, i, width):
    return [ref[t * DEC_BATCH:(t + 1) * DEC_BATCH, i * width:(i + 1) * width]
            for t in range(DEC_SEQ)]


def _first_layer_state_out(sout_ref, layer):
    if layer != 0:
        return sout_ref
    for later in range(1, DEPTH):
        sout_ref[later] = jnp.zeros(sout_ref.shape[1:], F32)
    return sout_ref.at[0]


def _gla_sample_kernel(q_ref, k_ref, v_ref, r_ref, a_ref, wup_ref, bg_ref, gn_ref, sin_ref,
                       o_ref, sout_ref, oin_s, qh_s, kt_s, dt_s, *, layer):
    bb = pl.program_id(1)
    heads = range(GLA_SAMPLE_HEADS)

    @pl.when(bb == 0)
    def _():
        for i in heads:
            q = [x * (GLA_DK ** -0.5) for x in _tiles(q_ref, i, GLA_DK)]
            pre = [jnp.dot(x.astype(BF16),
                           wup_ref[:, i * GLA_DK:(i + 1) * GLA_DK].astype(BF16),
                           preferred_element_type=F32) + bg_ref[:, i * GLA_DK:(i + 1) * GLA_DK]
                   for x in _tiles(a_ref, 0, LANES)]
            g = [_log_sigmoid(x) * (1.0 / GLA_TAU) for x in pre]
            _sample_setup(q, _tiles(k_ref, i, GLA_DK), _tiles(v_ref, i, GLA_DV), g,
                          oin_s.at[i], qh_s.at[i], kt_s.at[i], dt_s.at[i])

    sout = _first_layer_state_out(sout_ref, layer)
    for i in heads:
        o = _sample_step(bb, i, v_ref, sin_ref, sout, qh_s.at[i], kt_s.at[i], dt_s.at[i])
        _sample_emit(bb, i, o, oin_s.at[i], r_ref, gn_ref, o_ref)


def _hgrn_sample_kernel(hq_ref, hf_ref, hi_ref, hg_ref, gam_ref, hn_ref, sin_ref,
                        o_ref, sout_ref, oin_s, qh_s, kt_s, dt_s, *, layer):
    bb = pl.program_id(1)
    heads = range(HGRN_SAMPLE_HEADS)

    @pl.when(bb == 0)
    def _():
        lb = _hgrn_lower_bound(gam_ref, layer)
        for i in heads:
            q = [_silu(x) * (HGRN_DK ** -0.5) for x in _tiles(hq_ref, i, HGRN_DK)]
            kg = [_hgrn_gate(x, lb[:, i * HGRN_DK:(i + 1) * HGRN_DK])
                  for x in _tiles(hf_ref, i, HGRN_DK)]
            _sample_setup(q, [x[0] for x in kg], _tiles(hi_ref, i, HGRN_DV), [x[1] for x in kg],
                          oin_s.at[i], qh_s.at[i], kt_s.at[i], dt_s.at[i])

    sout = _first_layer_state_out(sout_ref, layer)
    for i in heads:
        o = _sample_step(bb, i, hi_ref, sin_ref, sout, qh_s.at[i], kt_s.at[i], dt_s.at[i])
        _sample_emit(bb, i, o, oin_s.at[i], hg_ref, hn_ref, o_ref)


def _sample_scratch(nh, dv):
    return [pltpu.VMEM((nh, DEC_SEQ, DEC_BATCH, dv), F32),
            pltpu.VMEM((nh, DEC_SEQ, DEC_BATCH, LANES), F32),
            pltpu.VMEM((nh, NBLK, LANES, DEC_SEQ * SAMPLE_NB), F32),
            pltpu.VMEM((nh, NBLK, LANES, SAMPLE_NB), F32)]


def _sample_call(kern, n_in, in_specs, args, heads, nh, dv, col_out, state_in, merged,
                 state_prev, layer, name):
    st_spec = pl.BlockSpec((None, SAMPLE_NB, nh, LANES, dv), lambda h, bb: (layer, bb, h, 0, 0))
    st_out_spec = st_spec
    if state_prev is None:
        st_out_spec = pl.BlockSpec((DEPTH, SAMPLE_NB, nh, LANES, dv),
                                   lambda h, bb: (0, bb, h, 0, 0))
    carries = [merged] if state_prev is None else [merged, state_prev]
    in_specs = in_specs + [st_spec] + [pl.BlockSpec(memory_space=pl.ANY)] * len(carries)
    args = args + [state_in] + carries
    aliases = {n_in + 1 + i: i for i in range(len(carries))}
    return pl.pallas_call(
        _skip_carries(kern, n_in + 1, len(carries)),
        grid=(heads // nh, NBLK),
        in_specs=in_specs,
        out_specs=[pl.BlockSpec((N_SAMPLE, nh * dv),
                                lambda h, bb: (SAMPLE_ROW_BLOCK, col_out + h)),
                   st_out_spec],
        out_shape=[jax.ShapeDtypeStruct((N_ROWS, D_MODEL), BF16),
                   jax.ShapeDtypeStruct(state_in.shape, F32)],
        scratch_shapes=_sample_scratch(nh, dv),
        input_output_aliases=aliases,
        compiler_params=_params(("arbitrary", "arbitrary")),
        name=name,
    )(*args)


def _sample_tile(width, col):
    return pl.BlockSpec((N_SAMPLE, width), lambda h, bb: (SAMPLE_ROW_BLOCK, col // width + h))


def _gla_sample(proj, alr, wup, bg, gn, state_in, merged, state_prev, layer):
    nh = GLA_SAMPLE_HEADS
    dk, dv = nh * GLA_DK, nh * GLA_DV
    in_specs = [_sample_tile(dk, COL_Q), _sample_tile(dk, COL_K),
                _sample_tile(dv, COL_V), _sample_tile(dv, COL_R),
                pl.BlockSpec((N_SAMPLE, LANES), lambda h, bb: (SAMPLE_ROW_BLOCK, 0)),
                pl.BlockSpec((None, LANES, dk), lambda h, bb: (layer, 0, h)),
                pl.BlockSpec((None, 1, dk), lambda h, bb: (layer, 0, h)),
                pl.BlockSpec((None, 1, GLA_DV), lambda h, bb: (layer, 0, 0))]
    args = [proj, proj, proj, proj, alr, wup, bg, gn]
    return _sample_call(functools.partial(_gla_sample_kernel, layer=layer), 8, in_specs, args,
                        GLA_HEADS, nh, GLA_DV, 0, state_in, merged, state_prev, layer,
                        "gla_sample")


def _hgrn_sample(proj, gamma, hn, state_in, merged, state_prev, layer):
    nh = HGRN_SAMPLE_HEADS
    dk, dv = nh * HGRN_DK, nh * HGRN_DV
    in_specs = [_sample_tile(dk, COL_HQ), _sample_tile(dk, COL_HF),
                _sample_tile(dv, COL_HI), _sample_tile(dv, COL_HG),
                pl.BlockSpec((DEPTH, dk), lambda h, bb: (0, h)),
                pl.BlockSpec((None, 1, HGRN_DV), lambda h, bb: (layer, 0, 0))]
    args = [proj, proj, proj, proj, gamma, hn]
    return _sample_call(functools.partial(_hgrn_sample_kernel, layer=layer), 6, in_specs, args,
                        HGRN_HEADS, nh, HGRN_DV, GLA_WIDTH // dv, state_in, merged, state_prev,
                        layer, "hgrn_sample")


def kernel(x_prompt, x_sample, state_gla, state_hgrn, norm_gains, ffn1_w_in, ffn1_w_out,
           ffn2_w_in, ffn2_w_out, mix_w_in, gla_w_gate_up, gla_b_gate, gla_norm, hgrn_gamma,
           hgrn_norm, mix_w_out):
    xp = x_prompt.reshape(N_PROMPT, D_MODEL)
    xs = jnp.transpose(x_sample, (1, 0, 2)).reshape(N_SAMPLE, D_MODEL)
    gains = norm_gains.reshape(DEPTH * 6, 1, D_MODEL)
    w_mix_t = jnp.swapaxes(mix_w_in, 1, 2)
    wup = jnp.pad(gla_w_gate_up, ((0, 0), (0, LANES - GLA_RANK), (0, 0)))
    bg = gla_b_gate.reshape(DEPTH, 1, GLA_HEADS * GLA_DK)
    gn = gla_norm.reshape(DEPTH, 1, GLA_DV)
    hn = hgrn_norm.reshape(DEPTH, 1, HGRN_DV)

    h = _rms_cast(xp, xs, gains, 0)
    x = None
    gla_p, hgrn_p = [], []
    st_gla, st_hgrn = None, None
    for l in range(DEPTH):
        base = 6 * l
        act, wo = _ffn_in(h, ffn1_w_in, ffn1_w_out, l)
        if l == 0:
            x, h = _proj_out(act, wo, xp, gains, base + 1, base + 2, 0.5, x_sample=xs)
        else:
            x, h = _proj_out(act, wo, x, gains, base + 1, base + 2, 0.5)
        proj, alr = _mix_in(h, w_mix_t, l)
        merged, wo_mix = _mix_aux(mix_w_out, l)
        merged, sg = _gla_prompt(proj, alr, wup, bg, gn, merged, l)
        merged, sh = _hgrn_prompt(proj, hgrn_gamma, hn, merged, l)
        merged, st_gla = _gla_sample(proj, alr, wup, bg, gn, state_gla, merged, st_gla, l)
        merged, st_hgrn = _hgrn_sample(proj, hgrn_gamma, hn, state_hgrn, merged, st_hgrn, l)
        gla_p.append(sg)
        hgrn_p.append(sh)
        x, h = _proj_out(merged, wo_mix, x, gains, base + 3, base + 4, 1.0)
        act, wo = _ffn_in(h, ffn2_w_in, ffn2_w_out, l)
        if l + 1 < DEPTH:
            x, h = _proj_out(act, wo, x, gains, base + 5, base + 6, 0.5)
        else:
            y_prompt_rows, y_sample_rows = _proj_out(act, wo, x, gains, base + 5, None, 0.5)

    y_prompt = y_prompt_rows.reshape(BATCH, SEQ, D_MODEL)
    y_sample = jnp.transpose(y_sample_rows.reshape(DEC_SEQ, DEC_BATCH, D_MODEL), (1, 0, 2))
    return (y_prompt, y_sample, jnp.stack(gla_p), jnp.stack(hgrn_p), st_gla, st_hgrn)
```

```python
import functools

import jax
import jax.numpy as jnp
from jax import lax
from jax.experimental import pallas as pl
from jax.experimental.pallas import tpu as pltpu

F32 = jnp.float32
BF16 = jnp.bfloat16

D_MODEL = 2048
BATCH = 4
SEQ = 2048
DEPTH = 2
DEC_BATCH = 128
DEC_SEQ = 4
GLA_HEADS = 4
GLA_DK = 128
GLA_DV = 256
GLA_WIDTH = GLA_HEADS * GLA_DV
GLA_RANK = 16
GLA_TAU = 16.0
HGRN_HEADS = 8
HGRN_DK = 128
HGRN_DV = 128
HGRN_WIDTH = HGRN_HEADS * HGRN_DV
D_FF = 5632
EPS = 1e-6

N_PROMPT = BATCH * SEQ
N_SAMPLE = DEC_BATCH * DEC_SEQ
N_ROWS = N_PROMPT + N_SAMPLE
SAMPLE_ROW_BLOCK = N_PROMPT // N_SAMPLE

LANES = 128
SUBLANES = 8
VMEM_LIMIT = 60 * 1024 * 1024

COL_Q = 0
COL_K = 512
COL_V = 1024
COL_R = 2048
COL_HQ = 3072
COL_HF = 4096
COL_HI = 5120
COL_HG = 6144
PROJ_W = 7168
GATE_COL = 2048
MIX_ALIGNED_TILES = GATE_COL // 1024

TM_FFN = 1088
TM_IN = 1088
TN = 512
MIX_TN = 1024
FFN_NC = 256
TM_OUT = 1088
TK_CHOICES = (1408, 1024, 512)
TK_CHUNKED_TAIL_MIN = 1024
OUT_NC = 512
MIX_WO_ROWS = D_MODEL // (N_ROWS // TM_IN)
OUT_RC = 272
LAST_TILE = N_ROWS // TM_OUT - 1
TAIL_PROMPT_ROWS = N_PROMPT - LAST_TILE * TM_OUT
TM_ROW = 512
CHUNK = 256
SCAN_HEADS = 2
HGRN_SCAN_HEADS = 4
SCAN_UNROLL = 1
SAMPLE_NB = 16
GLA_SAMPLE_HEADS = 2
HGRN_SAMPLE_HEADS = 4
NBLK = DEC_BATCH // SAMPLE_NB


def _params(sem):
    return pltpu.CompilerParams(dimension_semantics=sem, vmem_limit_bytes=VMEM_LIMIT)


def _sigmoid(x):
    return 1.0 / (1.0 + jnp.exp(-x))


def _silu(x):
    return x * _sigmoid(x)


def _rms(y, g):
    return y * lax.rsqrt(jnp.mean(y * y, axis=-1, keepdims=True) + EPS) * g


def _skip_carries(kern, n_in, n_carry):
    def wrapped(*refs):
        return kern(*refs[:n_in], *refs[n_in + n_carry:])
    return wrapped


def _rms_cast_kernel(xp_ref, xs_ref, g_ref, h_ref):
    prompt_tiles = N_PROMPT // TM_ROW

    @pl.when(pl.program_id(0) < prompt_tiles)
    def _():
        h_ref[...] = _rms(xp_ref[...], g_ref[...]).astype(BF16)

    @pl.when(pl.program_id(0) >= prompt_tiles)
    def _():
        h_ref[...] = _rms(xs_ref[...], g_ref[...]).astype(BF16)


def _rms_cast(x_prompt, x_sample, gains, gi):
    prompt_tiles = N_PROMPT // TM_ROW
    return pl.pallas_call(
        _rms_cast_kernel,
        grid=(N_ROWS // TM_ROW,),
        in_specs=[pl.BlockSpec((TM_ROW, D_MODEL), lambda m: (jnp.minimum(m, prompt_tiles - 1), 0)),
                  pl.BlockSpec((TM_ROW, D_MODEL), lambda m: (jnp.maximum(m - prompt_tiles, 0), 0)),
                  pl.BlockSpec((None, 1, D_MODEL), lambda m: (gi, 0, 0))],
        out_specs=pl.BlockSpec((TM_ROW, D_MODEL), lambda m: (m, 0)),
        out_shape=jax.ShapeDtypeStruct((N_ROWS, D_MODEL), BF16),
        compiler_params=_params(("arbitrary",)),
        name="rms_cast",
    )(x_prompt, x_sample, gains)


def _ffn_in_kernel(h_ref, wg_ref, wu_ref, wo_ref, o_ref, wob_ref, wg_s, wu_s):
    @pl.when(pl.program_id(1) == 0)
    def _():
        wg_s[...] = wg_ref[...].astype(BF16)
        wu_s[...] = wu_ref[...].astype(BF16)

    wob_ref[...] = wo_ref[...].astype(BF16)
    h = h_ref[...]
    for c in range(TN // FFN_NC):
        cols = slice(c * FFN_NC, (c + 1) * FFN_NC)
        g = jnp.dot(h, wg_s[:, cols], preferred_element_type=F32)
        u = jnp.dot(h, wu_s[:, cols], preferred_element_type=F32)
        o_ref[:, cols] = (g * _sigmoid(g) * u).astype(BF16)


def _ffn_in(h, w_in, w_out, layer):
    nt = D_FF // TN
    nm = N_ROWS // TM_FFN
    slab = D_FF // (nt * nm)
    return pl.pallas_call(
        _ffn_in_kernel,
        grid=(nt, nm),
        in_specs=[pl.BlockSpec((TM_FFN, D_MODEL), lambda n, m: (m, 0)),
                  pl.BlockSpec((None, D_MODEL, TN), lambda n, m: (layer, 0, n)),
                  pl.BlockSpec((None, D_MODEL, TN), lambda n, m: (layer, 0, n + nt)),
                  pl.BlockSpec((None, slab, D_MODEL), lambda n, m: (layer, n * nm + m, 0))],
        out_specs=[pl.BlockSpec((TM_FFN, TN), lambda n, m: (m, n)),
                   pl.BlockSpec((slab, D_MODEL), lambda n, m: (n * nm + m, 0))],
        out_shape=[jax.ShapeDtypeStruct((N_ROWS, D_FF), BF16),
                   jax.ShapeDtypeStruct((D_FF, D_MODEL), BF16)],
        scratch_shapes=[pltpu.VMEM((D_MODEL, TN), BF16), pltpu.VMEM((D_MODEL, TN), BF16)],
        compiler_params=_params(("arbitrary", "arbitrary")),
        name="ffn_in",
    )(h, w_in, w_in, w_out)


def _mix_in_kernel(h_ref, wa_ref, wb_ref, o_ref, a_ref, w_s, wg_s):
    nt = (((1,), (1,)), ((), ()))
    n = pl.program_id(0)
    first_row_tile = pl.program_id(1) == 0
    gate_tile = n == MIX_ALIGNED_TILES
    kept = MIX_TN - GLA_RANK

    @pl.when(jnp.logical_and(first_row_tile, n < MIX_ALIGNED_TILES))
    def _():
        w_s[...] = wa_ref[...].astype(BF16)

    @pl.when(jnp.logical_and(first_row_tile, n >= MIX_ALIGNED_TILES))
    def _():
        w_s[0:kept, :] = wa_ref[GLA_RANK:MIX_TN, :].astype(BF16)
        w_s[kept:MIX_TN, :] = wb_ref[0:GLA_RANK, :].astype(BF16)

    @pl.when(jnp.logical_and(first_row_tile, gate_tile))
    def _():
        wg_s[...] = jnp.zeros_like(wg_s)
        wg_s[0:GLA_RANK, :] = wa_ref[0:GLA_RANK, :].astype(BF16)

    h = h_ref[...]
    o_ref[...] = lax.dot_general(h, w_s[...], nt, preferred_element_type=F32)

    @pl.when(gate_tile)
    def _():
        a_ref[...] = lax.dot_general(h, wg_s[...], nt, preferred_element_type=F32)


def _mix_in(h, w_in_t, layer):
    last = N_ROWS // TM_IN - 1
    slabs_per_tile = MIX_TN // LANES

    def gate_block(n, m):
        return jnp.where(n == MIX_ALIGNED_TILES, m, jnp.where(n < MIX_ALIGNED_TILES, 0, last))

    return pl.pallas_call(
        _mix_in_kernel,
        grid=(PROJ_W // MIX_TN, N_ROWS // TM_IN),
        in_specs=[pl.BlockSpec((TM_IN, D_MODEL), lambda n, m: (m, 0)),
                  pl.BlockSpec((None, MIX_TN, D_MODEL), lambda n, m: (layer, n, 0)),
                  pl.BlockSpec((None, LANES, D_MODEL),
                               lambda n, m: (layer, (n + 1) * slabs_per_tile, 0))],
        out_specs=[pl.BlockSpec((TM_IN, MIX_TN), lambda n, m: (m, n)),
                   pl.BlockSpec((TM_IN, LANES), lambda n, m: (gate_block(n, m), 0))],
        out_shape=[jax.ShapeDtypeStruct((N_ROWS, PROJ_W), F32),
                   jax.ShapeDtypeStruct((N_ROWS, LANES), F32)],
        scratch_shapes=[pltpu.VMEM((MIX_TN, D_MODEL), BF16), pltpu.VMEM((LANES, D_MODEL), BF16)],
        compiler_params=_params(("arbitrary", "arbitrary")),
        name="mix_in",
    )(h, w_in_t, w_in_t)


def _mix_aux_kernel(wo_ref, z_ref, wob_ref):
    z_ref[...] = jnp.zeros_like(z_ref)
    wob_ref[...] = wo_ref[...].astype(BF16)


def _mix_aux(w_out, layer):
    return pl.pallas_call(
        _mix_aux_kernel,
        grid=(N_ROWS // TM_IN,),
        in_specs=[pl.BlockSpec((None, MIX_WO_ROWS, D_MODEL), lambda m: (layer, m, 0))],
        out_specs=[pl.BlockSpec((TM_IN, D_MODEL), lambda m: (m, 0)),
                   pl.BlockSpec((MIX_WO_ROWS, D_MODEL), lambda m: (m, 0))],
        out_shape=[jax.ShapeDtypeStruct((N_ROWS, D_MODEL), BF16),
                   jax.ShapeDtypeStruct((D_MODEL, D_MODEL), BF16)],
        compiler_params=_params(("arbitrary",)),
        name="mix_aux",
    )(w_out)


def _proj_out_kernel(*refs, alpha, nk, split_x, final, chunked_tail):
    refs = list(refs)
    a_ref, w_ref, x_hbm = refs[:3]
    xs_hbm = refs[3] if split_x else None
    gp_ref = refs[3 + split_x]
    gn_ref = None if final else refs[4 + split_x]
    xo_ref, o2_ref, xbuf, sem = refs[-4:]
    m = pl.program_id(0)
    k = pl.program_id(1)

    def tile_copy():
        rows = pl.ds(pl.multiple_of(m * TM_OUT, TM_OUT), TM_OUT)
        return pltpu.make_async_copy(x_hbm.at[rows, :], xbuf, sem.at[0])

    def tail_copies():
        head = pltpu.make_async_copy(x_hbm.at[pl.ds(LAST_TILE * TM_OUT, TAIL_PROMPT_ROWS), :],
                                     xbuf.at[pl.ds(0, TAIL_PROMPT_ROWS), :], sem.at[0])
        tail = pltpu.make_async_copy(xs_hbm, xbuf.at[pl.ds(TAIL_PROMPT_ROWS, N_SAMPLE), :],
                                     sem.at[1])
        return head, tail

    def residual_copies(act):
        if not split_x:
            act(tile_copy())
            return

        @pl.when(m < LAST_TILE)
        def _():
            act(tile_copy())

        @pl.when(m == LAST_TILE)
        def _():
            for c in tail_copies():
                act(c)

    def accumulate(rows, first):
        a = a_ref[rows, :]
        for c in range(D_MODEL // OUT_NC):
            cols = slice(c * OUT_NC, (c + 1) * OUT_NC)
            p = jnp.dot(a, w_ref[:, cols], preferred_element_type=F32)
            xo_ref[rows, cols] = p if first else xo_ref[rows, cols] + p

    whole = slice(0, TM_OUT)

    @pl.when(k == 0)
    def _():
        residual_copies(lambda c: c.start())
        accumulate(whole, True)

    @pl.when(jnp.logical_and(k > 0, k < nk - 1))
    def _():
        accumulate(whole, False)

    def epilogue(rows):
        xn = xbuf[rows, :] + alpha * _rms(xo_ref[rows, :], gp_ref[...])
        xo_ref[rows, :] = xn
        if not final:
            o2_ref[rows, :] = _rms(xn, gn_ref[...]).astype(BF16)

    @pl.when(k == nk - 1)
    def _():
        residual_copies(lambda c: c.wait())
        if chunked_tail:
            for i in range(TM_OUT // OUT_RC):
                rows = slice(i * OUT_RC, (i + 1) * OUT_RC)
                accumulate(rows, False)
                epilogue(rows)
        else:
            accumulate(whole, False)

            def rows_body(i, carry):
                epilogue(pl.ds(pl.multiple_of(i * OUT_RC, OUT_RC), OUT_RC))
                return carry

            lax.fori_loop(0, TM_OUT // OUT_RC, rows_body, 0)
        if final:
            @pl.when(m == LAST_TILE)
            def _():
                o2_ref[...] = xo_ref[pl.ds(TAIL_PROMPT_ROWS, N_SAMPLE), :]


def _proj_out(a, w, x, gains, gi_post, gi_next, alpha, x_sample=None):
    kdim = a.shape[1]
    tk = next(t for t in TK_CHOICES if kdim % t == 0)
    nk = kdim // tk
    split_x = x_sample is not None
    final = gi_next is None
    chunked_tail = tk > TK_CHUNKED_TAIL_MIN
    kern = functools.partial(_proj_out_kernel, alpha=alpha, nk=nk, split_x=split_x, final=final,
                             chunked_tail=chunked_tail)
    gain = lambda gi: pl.BlockSpec((None, 1, D_MODEL), lambda m, k: (gi, 0, 0))
    in_specs = [pl.BlockSpec((TM_OUT, tk), lambda m, k: (m, k)),
                pl.BlockSpec((tk, D_MODEL), lambda m, k: (k, 0)),
                pl.BlockSpec(memory_space=pl.ANY)]
    args = [a, w, x]
    if split_x:
        in_specs.append(pl.BlockSpec(memory_space=pl.ANY))
        args.append(x_sample)
    in_specs.append(gain(gi_post))
    args.append(gains)
    row_tile = pl.BlockSpec((TM_OUT, D_MODEL), lambda m, k: (m, 0))
    if final:
        out_specs = [row_tile, pl.BlockSpec((N_SAMPLE, D_MODEL), lambda m, k: (0, 0))]
        out_shape = [jax.ShapeDtypeStruct((N_PROMPT, D_MODEL), F32),
                     jax.ShapeDtypeStruct((N_SAMPLE, D_MODEL), F32)]
    else:
        in_specs.append(gain(gi_next))
        args.append(gains)
        out_specs = [row_tile, row_tile]
        out_shape = [jax.ShapeDtypeStruct((N_ROWS, D_MODEL), F32),
                     jax.ShapeDtypeStruct((N_ROWS, D_MODEL), BF16)]
    return pl.pallas_call(
        kern,
        grid=(N_ROWS // TM_OUT, nk),
        in_specs=in_specs,
        out_specs=out_specs,
        out_shape=out_shape,
        scratch_shapes=[pltpu.VMEM((TM_OUT, D_MODEL), F32), pltpu.SemaphoreType.DMA((2,))],
        compiler_params=_params(("arbitrary", "arbitrary")),
        name="proj_out",
    )(*args)


def _log_sigmoid(x):
    return jnp.minimum(x, 0.0) - jnp.log(1.0 + jnp.exp(-jnp.abs(x)))


def _hgrn_lower_bound(gam_ref, layer):
    gam = gam_ref[...]
    mx = jnp.max(gam, axis=0, keepdims=True)
    e = jnp.exp(gam - mx)
    probs = e / jnp.sum(e, axis=0, keepdims=True)
    acc = probs[0:1, :]
    for j in range(1, layer + 1):
        acc = acc + probs[j:j + 1, :]
    return acc - probs[0:1, :]


def _hgrn_gate(z, lb):
    e = jnp.exp(-jnp.abs(z))
    r = 1.0 / (1.0 + e)
    er = e * r
    pos = z >= 0.0
    sig = jnp.where(pos, r, er)
    sig_neg = jnp.where(pos, er, r)
    return (1.0 - lb) * sig_neg, jnp.log(lb + (1.0 - lb) * sig)


def _chunk_levels(c):
    levels, m = [], 2
    while m <= c:
        levels.append(m)
        m *= 2
    return levels


def _midpoint_distance(cum, g, m):
    c = cum.shape[0]
    hm = m // 2
    if m == 2:
        odd = jnp.bitwise_and(lax.broadcasted_iota(jnp.int32, cum.shape, 0), 1) == 1
        return jnp.where(odd, jnp.abs(g), 0.0)
    if m < 2 * SUBLANES:
        cum3 = cum.reshape(c // SUBLANES, SUBLANES, LANES)
        sub = lax.broadcasted_iota(jnp.int32, cum3.shape, 1)
        mid = None
        for b0 in range(0, SUBLANES, m):
            cand = jnp.broadcast_to(cum3[:, b0 + hm - 1:b0 + hm, :], cum3.shape)
            mid = cand if mid is None else jnp.where(sub >= b0, cand, mid)
        return jnp.abs(cum3 - mid).reshape(c, LANES)
    mid = cum.reshape(c // m, m, LANES)[:, hm - 1:hm, :]
    mid = jnp.broadcast_to(mid, (c // m, m, LANES)).reshape(c, LANES)
    return jnp.abs(cum - mid)


def _pair_levels(c):
    row = lax.broadcasted_iota(jnp.int32, (c, c), 0)
    col = lax.broadcasted_iota(jnp.int32, (c, c), 1)
    diff = jnp.bitwise_xor(row, col)
    lvl = jnp.zeros((c, c), jnp.int32)
    for j in range(c.bit_length() - 1):
        lvl = lvl + (diff >= (1 << j)).astype(jnp.int32)
    return jnp.where(col > row, -1, lvl)


def _chunk_scan(qs, ks, vs, gs, s_refs, lvl):
    heads = range(len(qs))
    c = qs[0].shape[0]
    nt = (((1,), (1,)), ((), ()))
    eye = lvl == 0
    tri = (lvl >= 0).astype(BF16)
    g_hi = [g.astype(BF16) for g in gs]
    g_lo = [(gs[i] - g_hi[i].astype(F32)).astype(BF16) for i in heads]
    cums = [jnp.dot(tri, g_hi[i], preferred_element_type=F32)
            + jnp.dot(tri, g_lo[i], preferred_element_type=F32) for i in heads]

    s_old = [s_refs[i][...] for i in heads]
    vb = [v.astype(BF16) for v in vs]
    o = [jnp.dot((qs[i] * jnp.exp(cums[i])).astype(BF16), s_old[i].astype(BF16),
                 preferred_element_type=F32) for i in heads]

    for i in heads:
        last = cums[i][c - 1:c, :]
        kt_t = jnp.transpose(ks[i] * jnp.exp(last - cums[i])).astype(BF16)
        dec = jnp.broadcast_to(jnp.exp(last), (LANES, LANES))
        dec_col = jnp.sum(jnp.where(eye[:LANES, :LANES], dec, 0.0), axis=1, keepdims=True)
        s_refs[i][...] = s_old[i] * dec_col + jnp.dot(kt_t, vb[i], preferred_element_type=F32)

    rowl = lax.broadcasted_iota(jnp.int32, (c, LANES), 0)
    a = [jnp.where(eye, lax.dot_general(qs[i].astype(BF16), ks[i].astype(BF16), nt,
                                        preferred_element_type=F32), 0.0) for i in heads]
    for m in _chunk_levels(c):
        second = jnp.bitwise_and(rowl, m // 2) != 0
        mask = lvl == m.bit_length() - 1
        x = [(jnp.where(second, qs[i], ks[i])
              * jnp.exp(-_midpoint_distance(cums[i], gs[i], m))).astype(BF16) for i in heads]
        am = [lax.dot_general(x[i], x[i], nt, preferred_element_type=F32) for i in heads]
        a = [jnp.where(mask, am[i], a[i]) for i in heads]
    return o, [x.astype(BF16) for x in a], vb


def _head_cols(ref, rows, i, width):
    return ref[rows, i * width:(i + 1) * width]


def _skewed_chunk_loop(front, finish, pend_refs):
    oi_s, pa_s, vb_s = pend_refs
    heads = range(oi_s.shape[0])

    def stash(parts):
        ois, pas, vbs = parts
        for i in heads:
            oi_s[i] = ois[i]
            pa_s[i] = pas[i]
            vb_s[i] = vbs[i]

    def issue():
        return [oi_s[i] + jnp.dot(pa_s[i], vb_s[i], preferred_element_type=F32) for i in heads]

    stash(front(0))

    def body(ci, carry):
        os = issue()
        stash(front(ci))
        finish(ci - 1, os)
        return carry

    n = SEQ // CHUNK
    lax.fori_loop(1, n, body, 0, unroll=SCAN_UNROLL)
    finish(n - 1, issue())


def _chunk_rows(ci):
    return pl.ds(pl.multiple_of(ci * CHUNK, CHUNK), CHUNK)


def _gla_prompt_kernel(q_ref, k_ref, v_ref, r_ref, a_ref, wup_ref, bg_ref, gn_ref, o_ref, s_ref,
                       oi_s, pa_s, vb_s):
    heads = range(SCAN_HEADS)
    s_ref[...] = jnp.zeros_like(s_ref)
    lvl = _pair_levels(CHUNK)

    def front(ci):
        rows = _chunk_rows(ci)
        a = a_ref[rows, :].astype(BF16)
        pre = jnp.dot(a, wup_ref[...].astype(BF16), preferred_element_type=F32) + bg_ref[...]
        gs = [_log_sigmoid(pre[:, i * GLA_DK:(i + 1) * GLA_DK]) * (1.0 / GLA_TAU) for i in heads]
        qs = [_head_cols(q_ref, rows, i, GLA_DK) * (GLA_DK ** -0.5) for i in heads]
        ks = [_head_cols(k_ref, rows, i, GLA_DK) for i in heads]
        vs = [_head_cols(v_ref, rows, i, GLA_DV) for i in heads]
        return _chunk_scan(qs, ks, vs, gs, [s_ref.at[i] for i in heads], lvl)

    def finish(ci, os):
        rows = _chunk_rows(ci)
        for i in heads:
            gate = _silu(_head_cols(r_ref, rows, i, GLA_DV))
            o_ref[rows, i * GLA_DV:(i + 1) * GLA_DV] = (_rms(os[i], gn_ref[...]) * gate).astype(BF16)

    _skewed_chunk_loop(front, finish, (oi_s, pa_s, vb_s))


def _pending_scratch(heads, dv):
    return [pltpu.VMEM((heads, CHUNK, dv), F32),
            pltpu.VMEM((heads, CHUNK, CHUNK), BF16),
            pltpu.VMEM((heads, CHUNK, dv), BF16)]


def _gla_prompt(proj, alr, wup, bg, gn, merged, layer):
    p = SCAN_HEADS
    dk, dv = p * GLA_DK, p * GLA_DV
    cq, ck, cv, cr = (COL_Q // dk, COL_K // dk, COL_V // dv, COL_R // dv)
    return pl.pallas_call(
        _skip_carries(_gla_prompt_kernel, 8, 1),
        grid=(BATCH, GLA_HEADS // p),
        in_specs=[pl.BlockSpec((SEQ, dk), lambda b, h: (b, cq + h)),
                  pl.BlockSpec((SEQ, dk), lambda b, h: (b, ck + h)),
                  pl.BlockSpec((SEQ, dv), lambda b, h: (b, cv + h)),
                  pl.BlockSpec((SEQ, dv), lambda b, h: (b, cr + h)),
                  pl.BlockSpec((SEQ, LANES), lambda b, h: (b, 0)),
                  pl.BlockSpec((None, LANES, dk), lambda b, h: (layer, 0, h)),
                  pl.BlockSpec((None, 1, dk), lambda b, h: (layer, 0, h)),
                  pl.BlockSpec((None, 1, GLA_DV), lambda b, h: (layer, 0, 0)),
                  pl.BlockSpec(memory_space=pl.ANY)],
        out_specs=[pl.BlockSpec((SEQ, dv), lambda b, h: (b, h)),
                   pl.BlockSpec((None, p, GLA_DK, GLA_DV), lambda b, h: (b, h, 0, 0))],
        out_shape=[jax.ShapeDtypeStruct((N_ROWS, D_MODEL), BF16),
                   jax.ShapeDtypeStruct((BATCH, GLA_HEADS, GLA_DK, GLA_DV), F32)],
        scratch_shapes=_pending_scratch(SCAN_HEADS, GLA_DV),
        input_output_aliases={8: 0},
        compiler_params=_params(("arbitrary", "arbitrary")),
        name="gla_prompt",
    )(proj, proj, proj, proj, alr, wup, bg, gn, merged)


def _hgrn_prompt_kernel(hq_ref, hf_ref, hi_ref, hg_ref, gam_ref, hn_ref, o_ref, s_ref,
                        oi_s, pa_s, vb_s, *, layer):
    heads = range(HGRN_SCAN_HEADS)
    s_ref[...] = jnp.zeros_like(s_ref)
    lvl = _pair_levels(CHUNK)
    lb = _hgrn_lower_bound(gam_ref, layer)

    def front(ci):
        rows = _chunk_rows(ci)
        qs = [_silu(_head_cols(hq_ref, rows, i, HGRN_DK)) * (HGRN_DK ** -0.5) for i in heads]
        kg = [_hgrn_gate(_head_cols(hf_ref, rows, i, HGRN_DK),
                         lb[:, i * HGRN_DK:(i + 1) * HGRN_DK]) for i in heads]
        vs = [_head_cols(hi_ref, rows, i, HGRN_DV) for i in heads]
        return _chunk_scan(qs, [x[0] for x in kg], vs, [x[1] for x in kg],
                           [s_ref.at[i] for i in heads], lvl)

    def finish(ci, os):
        rows = _chunk_rows(ci)
        for i in heads:
            gate = _silu(_head_cols(hg_ref, rows, i, HGRN_DV))
            o_ref[rows, i * HGRN_DV:(i + 1) * HGRN_DV] = (
                _rms(os[i], hn_ref[...]) * gate).astype(BF16)

    _skewed_chunk_loop(front, finish, (oi_s, pa_s, vb_s))


def _hgrn_prompt(proj, gamma, hn, merged, layer):
    p = HGRN_SCAN_HEADS
    dk, dv = p * HGRN_DK, p * HGRN_DV
    cq, cf, ci, cg = (COL_HQ // dk, COL_HF // dk, COL_HI // dv, COL_HG // dv)
    co = GLA_WIDTH // dv
    kern = _skip_carries(functools.partial(_hgrn_prompt_kernel, layer=layer), 6, 1)
    return pl.pallas_call(
        kern,
        grid=(BATCH, HGRN_HEADS // p),
        in_specs=[pl.BlockSpec((SEQ, dk), lambda b, h: (b, cq + h)),
                  pl.BlockSpec((SEQ, dk), lambda b, h: (b, cf + h)),
                  pl.BlockSpec((SEQ, dv), lambda b, h: (b, ci + h)),
                  pl.BlockSpec((SEQ, dv), lambda b, h: (b, cg + h)),
                  pl.BlockSpec((DEPTH, dk), lambda b, h: (0, h)),
                  pl.BlockSpec((None, 1, HGRN_DV), lambda b, h: (layer, 0, 0)),
                  pl.BlockSpec(memory_space=pl.ANY)],
        out_specs=[pl.BlockSpec((SEQ, dv), lambda b, h: (b, co + h)),
                   pl.BlockSpec((None, p, HGRN_DK, HGRN_DV), lambda b, h: (b, h, 0, 0))],
        out_shape=[jax.ShapeDtypeStruct((N_ROWS, D_MODEL), BF16),
                   jax.ShapeDtypeStruct((BATCH, HGRN_HEADS, HGRN_DK, HGRN_DV), F32)],
        scratch_shapes=_pending_scratch(HGRN_SCAN_HEADS, HGRN_DV),
        input_output_aliases={6: 0},
        compiler_params=_params(("arbitrary", "arbitrary")),
        name="hgrn_prompt",
    )(proj, proj, proj, proj, gamma, hn, merged)


def _sample_setup(q, k, v, g, oin_s, qh_s, kt_s, dt_s):
    cum = [g[0]]
    for t in range(1, DEC_SEQ):
        cum.append(cum[t - 1] + g[t])
    for t in range(DEC_SEQ):
        o = jnp.zeros_like(v[0])
        for s in range(t + 1):
            w = q[t] * k[s] if s == t else q[t] * (k[s] * jnp.exp(cum[t] - cum[s]))
            o = o + jnp.sum(w, axis=-1, keepdims=True) * v[s]
        oin_s[t] = o
    last = cum[DEC_SEQ - 1]
    for t in range(DEC_SEQ):
        qh_s[t] = q[t] * jnp.exp(cum[t])
        kt = jnp.transpose(k[t] * jnp.exp(last - cum[t]))
        for blk in range(NBLK):
            kt_s[blk, :, t * SAMPLE_NB:(t + 1) * SAMPLE_NB] = (
                kt[:, blk * SAMPLE_NB:(blk + 1) * SAMPLE_NB])
    dt = jnp.transpose(jnp.exp(last))
    for blk in range(NBLK):
        dt_s[blk] = dt[:, blk * SAMPLE_NB:(blk + 1) * SAMPLE_NB]


def _sample_rows(bb, t):
    return pl.ds(pl.multiple_of(t * DEC_BATCH + bb * SAMPLE_NB, SAMPLE_NB), SAMPLE_NB)


def _sample_step(bb, i, v_ref, sin_ref, sout_ref, qh_s, kt_s, dt_s):
    nb = SAMPLE_NB
    nrow = DEC_SEQ * nb
    dv = sin_ref.shape[-1]
    r0 = pl.multiple_of(bb * nb, nb)
    qh = jnp.concatenate([qh_s[t, pl.ds(r0, nb), :] for t in range(DEC_SEQ)], axis=0)
    vv = jnp.concatenate([v_ref[_sample_rows(bb, t), i * dv:(i + 1) * dv]
                          for t in range(DEC_SEQ)], axis=0)
    rj = jnp.bitwise_and(lax.broadcasted_iota(jnp.int32, (nrow, nb * LANES), 0), nb - 1)
    cb = lax.shift_right_logical(lax.broadcasted_iota(jnp.int32, (nrow, nb * LANES), 1), 7)
    q_bd = jnp.where(rj == cb, jnp.concatenate([qh] * nb, axis=1), 0.0).astype(BF16)
    s_old = sin_ref[:, i]
    s_stack = s_old.reshape(nb * LANES, dv)
    o = jnp.dot(q_bd, s_stack.astype(BF16), preferred_element_type=F32)

    kt64 = kt_s[bb]
    lane_j = jnp.bitwise_and(lax.broadcasted_iota(jnp.int32, kt64.shape, 1), nb - 1)
    kt_bd = jnp.concatenate([jnp.where(lane_j == j, kt64, 0.0) for j in range(nb)],
                            axis=0).astype(BF16)
    upd = jnp.dot(kt_bd, vv.astype(BF16), preferred_element_type=F32)
    dcols = dt_s[bb]
    for j in range(nb):
        sout_ref[j, i] = s_old[j] * dcols[:, j:j + 1] + upd[j * LANES:(j + 1) * LANES, :]
    return o


def _sample_emit(bb, i, o, oin_s, gate_ref, norm_ref, o_ref):
    nb = SAMPLE_NB
    dv = o.shape[-1]
    cols = slice(i * dv, (i + 1) * dv)
    for t in range(DEC_SEQ):
        rows = _sample_rows(bb, t)
        ot = o[t * nb:(t + 1) * nb, :] + oin_s[t, pl.ds(pl.multiple_of(bb * nb, nb), nb), :]
        o_ref[rows, cols] = (_rms(ot, norm_ref[...]) * _silu(gate_ref[rows, cols])).astype(BF16)


def _tiles(ref, i, width):
    return [ref[t * DEC_BATCH:(t + 1) * DEC_BATCH, i * width:(i + 1) * width]
            for t in range(DEC_SEQ)]


def _first_layer_state_out(sout_ref, layer):
    if layer != 0:
        return sout_ref
    for later in range(1, DEPTH):
        sout_ref[later] = jnp.zeros(sout_ref.shape[1:], F32)
    return sout_ref.at[0]


def _gla_sample_kernel(q_ref, k_ref, v_ref, r_ref, a_ref, wup_ref, bg_ref, gn_ref, sin_ref,
                       o_ref, sout_ref, oin_s, qh_s, kt_s, dt_s, *, layer):
    bb = pl.program_id(1)
    heads = range(GLA_SAMPLE_HEADS)

    @pl.when(bb == 0)
    def _():
        for i in heads:
            q = [x * (GLA_DK ** -0.5) for x in _tiles(q_ref, i, GLA_DK)]
            pre = [jnp.dot(x.astype(BF16),
                           wup_ref[:, i * GLA_DK:(i + 1) * GLA_DK].astype(BF16),
                           preferred_element_type=F32) + bg_ref[:, i * GLA_DK:(i + 1) * GLA_DK]
                   for x in _tiles(a_ref, 0, LANES)]
            g = [_log_sigmoid(x) * (1.0 / GLA_TAU) for x in pre]
            _sample_setup(q, _tiles(k_ref, i, GLA_DK), _tiles(v_ref, i, GLA_DV), g,
                          oin_s.at[i], qh_s.at[i], kt_s.at[i], dt_s.at[i])

    sout = _first_layer_state_out(sout_ref, layer)
    for i in heads:
        o = _sample_step(bb, i, v_ref, sin_ref, sout, qh_s.at[i], kt_s.at[i], dt_s.at[i])
        _sample_emit(bb, i, o, oin_s.at[i], r_ref, gn_ref, o_ref)


def _hgrn_sample_kernel(hq_ref, hf_ref, hi_ref, hg_ref, gam_ref, hn_ref, sin_ref,
                        o_ref, sout_ref, oin_s, qh_s, kt_s, dt_s, *, layer):
    bb = pl.program_id(1)
    heads = range(HGRN_SAMPLE_HEADS)

    @pl.when(bb == 0)
    def _():
        lb = _hgrn_lower_bound(gam_ref, layer)
        for i in heads:
            q = [_silu(x) * (HGRN_DK ** -0.5) for x in _tiles(hq_ref, i, HGRN_DK)]
            kg = [_hgrn_gate(x, lb[:, i * HGRN_DK:(i + 1) * HGRN_DK])
                  for x in _tiles(hf_ref, i, HGRN_DK)]
            _sample_setup(q, [x[0] for x in kg], _tiles(hi_ref, i, HGRN_DV), [x[1] for x in kg],
                          oin_s.at[i], qh_s.at[i], kt_s.at[i], dt_s.at[i])

    sout = _first_layer_state_out(sout_ref, layer)
    for i in heads:
        o = _sample_step(bb, i, hi_ref, sin_ref, sout, qh_s.at[i], kt_s.at[i], dt_s.at[i])
        _sample_emit(bb, i, o, oin_s.at[i], hg_ref, hn_ref, o_ref)


def _sample_scratch(nh, dv):
    return [pltpu.VMEM((nh, DEC_SEQ, DEC_BATCH, dv), F32),
            pltpu.VMEM((nh, DEC_SEQ, DEC_BATCH, LANES), F32),
            pltpu.VMEM((nh, NBLK, LANES, DEC_SEQ * SAMPLE_NB), F32),
            pltpu.VMEM((nh, NBLK, LANES, SAMPLE_NB), F32)]


def _sample_call(kern, n_in, in_specs, args, heads, nh, dv, col_out, state_in, merged,
                 state_prev, layer, name):
    st_spec = pl.BlockSpec((None, SAMPLE_NB, nh, LANES, dv), lambda h, bb: (layer, bb, h, 0, 0))
    st_out_spec = st_spec
    if state_prev is None:
        st_out_spec = pl.BlockSpec((DEPTH, SAMPLE_NB, nh, LANES, dv),
                                   lambda h, bb: (0, bb, h, 0, 0))
    carries = [merged] if state_prev is None else [merged, state_prev]
    in_specs = in_specs + [st_spec] + [pl.BlockSpec(memory_space=pl.ANY)] * len(carries)
    args = args + [state_in] + carries
    aliases = {n_in + 1 + i: i for i in range(len(carries))}
    return pl.pallas_call(
        _skip_carries(kern, n_in + 1, len(carries)),
        grid=(heads // nh, NBLK),
        in_specs=in_specs,
        out_specs=[pl.BlockSpec((N_SAMPLE, nh * dv),
                                lambda h, bb: (SAMPLE_ROW_BLOCK, col_out + h)),
                   st_out_spec],
        out_shape=[jax.ShapeDtypeStruct((N_ROWS, D_MODEL), BF16),
                   jax.ShapeDtypeStruct(state_in.shape, F32)],
        scratch_shapes=_sample_scratch(nh, dv),
        input_output_aliases=aliases,
        compiler_params=_params(("arbitrary", "arbitrary")),
        name=name,
    )(*args)


def _sample_tile(width, col):
    return pl.BlockSpec((N_SAMPLE, width), lambda h, bb: (SAMPLE_ROW_BLOCK, col // width + h))


def _gla_sample(proj, alr, wup, bg, gn, state_in, merged, state_prev, layer):
    nh = GLA_SAMPLE_HEADS
    dk, dv = nh * GLA_DK, nh * GLA_DV
    in_specs = [_sample_tile(dk, COL_Q), _sample_tile(dk, COL_K),
                _sample_tile(dv, COL_V), _sample_tile(dv, COL_R),
                pl.BlockSpec((N_SAMPLE, LANES), lambda h, bb: (SAMPLE_ROW_BLOCK, 0)),
                pl.BlockSpec((None, LANES, dk), lambda h, bb: (layer, 0, h)),
                pl.BlockSpec((None, 1, dk), lambda h, bb: (layer, 0, h)),
                pl.BlockSpec((None, 1, GLA_DV), lambda h, bb: (layer, 0, 0))]
    args = [proj, proj, proj, proj, alr, wup, bg, gn]
    return _sample_call(functools.partial(_gla_sample_kernel, layer=layer), 8, in_specs, args,
                        GLA_HEADS, nh, GLA_DV, 0, state_in, merged, state_prev, layer,
                        "gla_sample")


def _hgrn_sample(proj, gamma, hn, state_in, merged, state_prev, layer):
    nh = HGRN_SAMPLE_HEADS
    dk, dv = nh * HGRN_DK, nh * HGRN_DV
    in_specs = [_sample_tile(dk, COL_HQ), _sample_tile(dk, COL_HF),
                _sample_tile(dv, COL_HI), _sample_tile(dv, COL_HG),
                pl.BlockSpec((DEPTH, dk), lambda h, bb: (0, h)),
                pl.BlockSpec((None, 1, HGRN_DV), lambda h, bb: (layer, 0, 0))]
    args = [proj, proj, proj, proj, gamma, hn]
    return _sample_call(functools.partial(_hgrn_sample_kernel, layer=layer), 6, in_specs, args,
                        HGRN_HEADS, nh, HGRN_DV, GLA_WIDTH // dv, state_in, merged, state_prev,
                        layer, "hgrn_sample")


def kernel(x_prompt, x_sample, state_gla, state_hgrn, norm_gains, ffn1_w_in, ffn1_w_out,
           ffn2_w_in, ffn2_w_out, mix_w_in, gla_w_gate_up, gla_b_gate, gla_norm, hgrn_gamma,
           hgrn_norm, mix_w_out):
    xp = x_prompt.reshape(N_PROMPT, D_MODEL)
    xs = jnp.transpose(x_sample, (1, 0, 2)).reshape(N_SAMPLE, D_MODEL)
    gains = norm_gains.reshape(DEPTH * 6, 1, D_MODEL)
    w_mix_t = jnp.swapaxes(mix_w_in, 1, 2)
    wup = jnp.pad(gla_w_gate_up, ((0, 0), (0, LANES - GLA_RANK), (0, 0)))
    bg = gla_b_gate.reshape(DEPTH, 1, GLA_HEADS * GLA_DK)
    gn = gla_norm.reshape(DEPTH, 1, GLA_DV)
    hn = hgrn_norm.reshape(DEPTH, 1, HGRN_DV)

    h = _rms_cast(xp, xs, gains, 0)
    x = None
    gla_p, hgrn_p = [], []
    st_gla, st_hgrn = None, None
    for l in range(DEPTH):
        base = 6 * l
        act, wo = _ffn_in(h, ffn1_w_in, ffn1_w_out, l)
        if l == 0:
            x, h = _proj_out(act, wo, xp, gains, base + 1, base + 2, 0.5, x_sample=xs)
        else:
            x, h = _proj_out(act, wo, x, gains, base + 1, base + 2, 0.5)
        proj, alr = _mix_in(h, w_mix_t, l)
        merged, wo_mix = _mix_aux(mix_w_out, l)
        merged, sg = _gla_prompt(proj, alr, wup, bg, gn, merged, l)
        merged, sh = _hgrn_prompt(proj, hgrn_gamma, hn, merged, l)
        merged, st_gla = _gla_sample(proj, alr, wup, bg, gn, state_gla, merged, st_gla, l)
        merged, st_hgrn = _hgrn_sample(proj, hgrn_gamma, hn, state_hgrn, merged, st_hgrn, l)
        gla_p.append(sg)
        hgrn_p.append(sh)
        x, h = _proj_out(merged, wo_mix, x, gains, base + 3, base + 4, 1.0)
        act, wo = _ffn_in(h, ffn2_w_in, ffn2_w_out, l)
        if l + 1 < DEPTH:
            x, h = _proj_out(act, wo, x, gains, base + 5, base + 6, 0.5)
        else:
            y_prompt_rows, y_sample_rows = _proj_out(act, wo, x, gains, base + 5, None, 0.5)

    y_prompt = y_prompt_rows.reshape(BATCH, SEQ, D_MODEL)
    y_sample = jnp.transpose(y_sample_rows.reshape(DEC_SEQ, DEC_BATCH, D_MODEL), (1, 0, 2))
    return (y_prompt, y_sample, jnp.stack(gla_p), jnp.stack(hgrn_p), st_gla, st_hgrn)
```
